```python
import math
import jax, jax.numpy as jnp
from jax import lax
import numpy as np

D_MODEL = 1024
BATCH = 2
SEQ = 8192
DEPTH = 1
DEC_BATCH = 32
DEC_SEQ = 4
PAST_LEN = 8192
PAGE_SIZE = 128

N_HEADS_A = 8
HEAD_DIM_A = 64
WIDTH_A = N_HEADS_A * HEAD_DIM_A
N_HEADS_IDX = 4
HEAD_DIM_IDX = 64
TOPK_MAX = 256
Q_BLOCK = 128
N_HEADS_R = 8
DK_R = 64
DV_R = 64
WIDTH_R = N_HEADS_R * DV_R
CHUNK_R = 128
ROPE_BASE = 10000.0
EPS = 1e-6
IN_SIZES = (WIDTH_A, WIDTH_A, WIDTH_A, WIDTH_A,
            N_HEADS_IDX * HEAD_DIM_IDX, HEAD_DIM_IDX, N_HEADS_IDX,
            N_HEADS_R * DK_R, N_HEADS_R * DK_R, WIDTH_R, WIDTH_R,
            D_MODEL, D_MODEL)
N_IN = sum(IN_SIZES)

kernel_name = "dsa_retention_gated_hybrid_step"


def rms_norm(x, g):
    xf = x.astype(jnp.float32)
    r = lax.rsqrt(jnp.mean(xf * xf, axis=-1, keepdims=True) + EPS)
    return (xf * r).astype(x.dtype) * g


def modulated_input(x, c, w_ada, b_ada, g_norm):
    mod = jax.nn.silu(c) @ w_ada + b_ada
    shift, scale, gate = jnp.split(mod, 3, axis=-1)
    h = rms_norm(x, g_norm) * (1 + scale[:, None, :]) + shift[:, None, :]
    return h, gate


def split_projection(u):
    offs, o = [], 0
    for s in IN_SIZES[:-1]:
        o += s
        offs.append(o)
    return jnp.split(u, offs, axis=-1)


def rotary(x, pos):
    half = x.shape[-1] // 2
    inv = ROPE_BASE ** (-jnp.arange(half, dtype=jnp.float32) / half)
    ang = pos.astype(jnp.float32)[:, None] * inv[None, :]
    cos = jnp.cos(ang)[None, :, None, :]
    sin = jnp.sin(ang)[None, :, None, :]
    xf = x.astype(jnp.float32)
    x1, x2 = xf[..., :half], xf[..., half:]
    return jnp.concatenate([x1 * cos - x2 * sin, x1 * sin + x2 * cos], axis=-1)


def retention_log_decay():
    return jnp.log1p(-jnp.exp2(-5.0 - jnp.arange(N_HEADS_R, dtype=jnp.float32)))


def retention_chunk(state, q, k, v):
    C = q.shape[1]
    lg = retention_log_decay()
    i = jnp.arange(C, dtype=jnp.float32)
    diff = i[:, None] - i[None, :]
    dmat = jnp.where(diff >= 0, jnp.exp(lg[:, None, None] * jnp.maximum(diff, 0.0)), 0.0)
    scores = jnp.einsum('bihd,bjhd->bhij', q, k) * dmat
    out = jnp.einsum('bhij,bjhe->bihe', scores, v)
    q_decay = jnp.exp(lg[None, :] * (i[:, None] + 1.0))
    out = out + jnp.einsum('bihd,bhde->bihe', q, state) * q_decay[None, :, :, None]
    k_decay = jnp.exp(lg[None, :] * (C - 1.0 - i)[:, None])
    new_state = (jnp.exp(lg * C)[None, :, None, None] * state
                 + jnp.einsum('bjhd,bjhe->bhde', k * k_decay[None, :, :, None], v))
    return new_state, out


def retention_prompt(q, k, v):
    B, T = q.shape[:2]
    n = T // CHUNK_R
    chunks = lambda a: jnp.moveaxis(a.reshape(B, n, CHUNK_R, *a.shape[2:]), 1, 0)
    state0 = jnp.zeros((B, N_HEADS_R, DK_R, DV_R), jnp.float32)
    state, outs = lax.scan(lambda s, xs: retention_chunk(s, *xs), state0,
                           (chunks(q), chunks(k), chunks(v)))
    return jnp.moveaxis(outs, 0, 1).reshape(B, T, N_HEADS_R, DV_R), state


def group_norm_heads(o):
    mu = jnp.mean(o, axis=-1, keepdims=True)
    var = jnp.mean(jnp.square(o - mu), axis=-1, keepdims=True)
    y = (o - mu) * lax.rsqrt(var + EPS)
    return y.reshape(*o.shape[:2], -1)


def index_scores(qi, wi, ki):
    s = jnp.einsum('bqhd,bsd->bqhs', qi.astype(jnp.float32), ki.astype(jnp.float32))
    w = wi.astype(jnp.float32) * (N_HEADS_IDX ** -0.5 * HEAD_DIM_IDX ** -0.5)
    return jnp.einsum('bqhs,bqh->bqs', jax.nn.relu(s), w)


def select_keys(scores, q_pos, topk):
    S = scores.shape[-1]
    admissible = jnp.arange(S)[None, None, :] <= q_pos[None, :, None]
    _, idx = lax.top_k(jnp.where(admissible, scores, -jnp.inf), topk)
    valid = idx <= q_pos[None, :, None]
    return idx, valid


def sparse_attend(q, kg, vg, valid):
    logits = jnp.einsum('bqhd,bqkhd->bhqk', q.astype(jnp.float32), kg.astype(jnp.float32)) * HEAD_DIM_A ** -0.5
    logits = jnp.where(valid[:, None], logits, -jnp.inf)
    p = jax.nn.softmax(logits, axis=-1)
    return jnp.einsum('bhqk,bqkhd->bqhd', p, vg.astype(jnp.float32)).astype(q.dtype)


take_rows = jax.vmap(lambda a, i: a[i])


def dsa_prompt(q, k, v, qi, ki, wi):
    B, T = q.shape[:2]
    nb = T // Q_BLOCK
    topk = min(TOPK_MAX, T // 4)
    blocks = lambda a: jnp.moveaxis(a.reshape(B, nb, Q_BLOCK, *a.shape[2:]), 1, 0)

    def one_block(args):
        n, qb, qib, wib = args
        pos = n * Q_BLOCK + jnp.arange(Q_BLOCK)
        idx, valid = select_keys(index_scores(qib, wib, ki), pos, topk)
        return sparse_attend(qb, take_rows(k, idx), take_rows(v, idx), valid)

    out = lax.map(one_block, (jnp.arange(nb), blocks(q), blocks(qi), blocks(wi)))
    return jnp.moveaxis(out, 0, 1).reshape(B, T, WIDTH_A)


def dsa_sample(q, k_new, v_new, qi, ki_new, wi, cache_k, cache_v, cache_kidx, page_table):
    DB, T = q.shape[:2]
    topk = min(TOPK_MAX, (PAST_LEN + T) // 4)
    ki_past = cache_kidx[page_table].reshape(DB, PAST_LEN, HEAD_DIM_IDX)
    ki_all = jnp.concatenate([ki_past.astype(ki_new.dtype), ki_new], axis=1)
    pos = PAST_LEN + jnp.arange(T)
    idx, valid = select_keys(index_scores(qi, wi, ki_all), pos, topk)
    in_past = idx < PAST_LEN
    idx_p = jnp.minimum(idx, PAST_LEN - 1)
    phys = jnp.take_along_axis(page_table, (idx_p // PAGE_SIZE).reshape(DB, -1), axis=1).reshape(idx.shape)
    slot = idx_p % PAGE_SIZE
    idx_n = jnp.clip(idx - PAST_LEN, 0, T - 1)
    kg = jnp.where(in_past[..., None, None], cache_k[phys, slot].astype(k_new.dtype), take_rows(k_new, idx_n))
    vg = jnp.where(in_past[..., None, None], cache_v[phys, slot].astype(v_new.dtype), take_rows(v_new, idx_n))
    return sparse_attend(q, kg, vg, valid).reshape(DB, T, WIDTH_A)


def hybrid_layer(x, c, pos, attn_fn, ret_fn, w_ada, b_ada, g_norm, w_in, g_ret, w_pa, w_pr, w_out):
    B, T, _ = x.shape
    h, gate = modulated_input(x, c, w_ada, b_ada, g_norm)
    qa, ka, va, za, qi, ki, wi, qr, kr, vr, zr, ga, gr = split_projection(h @ w_in)
    heads = lambda a, n: a.reshape(B, T, n, -1)
    ka, va = heads(ka, N_HEADS_A), heads(va, N_HEADS_A)
    y_a = attn_fn(heads(qa, N_HEADS_A), ka, va, heads(qi, N_HEADS_IDX), ki, wi) * jax.nn.silu(za)
    q_r = rotary(heads(qr, N_HEADS_R), pos)
    k_r = rotary(heads(kr, N_HEADS_R), pos) * DK_R ** -0.5
    ret, ret_state = ret_fn(q_r, k_r, heads(vr, N_HEADS_R).astype(jnp.float32))
    y_r = group_norm_heads(ret).astype(x.dtype) * g_ret * jax.nn.silu(zr)
    merged = jax.nn.sigmoid(ga) * (y_a @ w_pa) + jax.nn.sigmoid(gr) * (y_r @ w_pr)
    x = x + gate[:, None, :] * (merged @ w_out)
    return x, (ka, va, ki, ret_state)


def setup_inputs(seed: int = 0) -> dict:
    key = jax.random.key(seed)
    ks = jax.random.split(key, 20)
    n_pages = PAST_LEN // PAGE_SIZE
    n_pool = (DEC_BATCH * n_pages * 5) // 4
    nrm = lambda k, shape, s=1.0: jax.random.normal(k, shape, jnp.float32) * s
    page_table = jax.random.permutation(ks[0], n_pool)[:DEC_BATCH * n_pages].reshape(DEC_BATCH, n_pages).astype(jnp.int32)
    return {
        "x_prompt": nrm(ks[1], (BATCH, SEQ, D_MODEL)),
        "x_sample": nrm(ks[2], (DEC_BATCH, DEC_SEQ, D_MODEL)),
        "cache_k": nrm(ks[3], (DEPTH, n_pool, PAGE_SIZE, N_HEADS_A, HEAD_DIM_A)),
        "cache_v": nrm(ks[4], (DEPTH, n_pool, PAGE_SIZE, N_HEADS_A, HEAD_DIM_A)),
        "cache_kidx": nrm(ks[5], (DEPTH, n_pool, PAGE_SIZE, HEAD_DIM_IDX)),
        "state_ret": nrm(ks[6], (DEPTH, DEC_BATCH, N_HEADS_R, DK_R, DV_R), 0.5),
        "page_table": page_table,
        "c_prompt": nrm(ks[7], (BATCH, D_MODEL)),
        "c_sample": nrm(ks[8], (DEC_BATCH, D_MODEL)),
        "w_ada": nrm(ks[9], (DEPTH, D_MODEL, 3 * D_MODEL), 0.2 * D_MODEL ** -0.5),
        "b_ada": nrm(ks[10], (DEPTH, 3 * D_MODEL), 0.01),
        "g_norm": 1.0 + nrm(ks[11], (DEPTH, D_MODEL), 0.02),
        "w_in": nrm(ks[12], (DEPTH, D_MODEL, N_IN), D_MODEL ** -0.5),
        "g_ret": 1.0 + nrm(ks[13], (DEPTH, WIDTH_R), 0.02),
        "w_pa": nrm(ks[14], (DEPTH, WIDTH_A, D_MODEL), WIDTH_A ** -0.5),
        "w_pr": nrm(ks[15], (DEPTH, WIDTH_R, D_MODEL), WIDTH_R ** -0.5),
        "w_out": nrm(ks[16], (DEPTH, D_MODEL, D_MODEL), D_MODEL ** -0.5),
        "g_final": 1.0 + nrm(ks[17], (D_MODEL,), 0.02),
    }


def reference(x_prompt, x_sample, cache_k, cache_v, cache_kidx, state_ret, page_table, c_prompt, c_sample,
              w_ada, b_ada, g_norm, w_in, g_ret, w_pa, w_pr, w_out, g_final):
    pos_p = jnp.arange(x_prompt.shape[1])
    pos_s = PAST_LEN + jnp.arange(x_sample.shape[1])
    hp, hs = x_prompt, x_sample
    kp, vp, kip, sp, kss, vss, kis, sss = [], [], [], [], [], [], [], []
    for l in range(DEPTH):
        lw = (w_ada[l], b_ada[l], g_norm[l], w_in[l], g_ret[l], w_pa[l], w_pr[l], w_out[l])
        hp, (k1, v1, ki1, s1) = hybrid_layer(hp, c_prompt, pos_p, dsa_prompt, retention_prompt, *lw)
        ck, cv, cki, st = cache_k[l], cache_v[l], cache_kidx[l], state_ret[l].astype(jnp.float32)
        attn_s = lambda q, k, v, qi, ki, wi: dsa_sample(q, k, v, qi, ki, wi, ck, cv, cki, page_table)
        def ret_s(q, k, v):
            new_state, out = retention_chunk(st, q, k, v)
            return out, new_state
        hs, (k2, v2, ki2, s2) = hybrid_layer(hs, c_sample, pos_s, attn_s, ret_s, *lw)
        kp.append(k1); vp.append(v1); kip.append(ki1); sp.append(s1)
        kss.append(k2); vss.append(v2); kis.append(ki2); sss.append(s2)
    y_prompt = rms_norm(hp, g_final)
    y_sample = rms_norm(hs, g_final)
    return (y_prompt, y_sample, jnp.stack(kp), jnp.stack(vp), jnp.stack(kip), jnp.stack(sp),
            jnp.stack(kss), jnp.stack(vss), jnp.stack(kis), jnp.stack(sss))
```

```python
import functools

import numpy as np
import jax
import jax.numpy as jnp
from jax import lax
from jax.experimental import pallas as pl
from jax.experimental.pallas import tpu as pltpu

D_MODEL = 1024
SEQ = 8192
DEC_SEQ = 4
PAST_LEN = 8192
PAGE_SIZE = 128
N_HEADS_A = 8
HEAD_DIM_A = 64
WIDTH_A = 512
N_HEADS_IDX = 4
HEAD_DIM_IDX = 64
TOPK = 256
N_HEADS_R = 8
DK_R = 64
WIDTH_R = 512
CHUNK_R = 128
ROPE_BASE = 10000.0
EPS = 1e-6

LANES = 128
SUBLANES = 8
DEC_PAD = SUBLANES
N_PAGES = PAST_LEN // PAGE_SIZE

_O_QA, _O_KA, _O_VA, _O_ZA = 0, 512, 1024, 1536
_O_QI, _O_KI, _O_WI = 2048, 2304, 2368
_O_QR, _O_KR, _O_VR, _O_ZR = 2372, 2884, 3396, 3908
_O_GA, _O_GR, _N_IN = 4420, 5444, 6468

N_MAIN = 6144
N_IDX = 768

INT_MIN = np.int32(-2 ** 31)
KEY_NEG_INF = np.int32(np.array(0xFF800000, np.uint32).view(np.int32) ^ np.int32(0x7FFFFFFF))
NEG_BIG = -1e30

F32 = jnp.float32
BF16 = jnp.bfloat16


def _dot(a, b):
    return jnp.dot(a, b, preferred_element_type=F32)


def _dot_nt(a, b):
    return lax.dot_general(a, b, (((1,), (1,)), ((), ())), preferred_element_type=F32)


def _dot_tn(a, b):
    return lax.dot_general(a, b, (((0,), (0,)), ((), ())), preferred_element_type=F32)


def _split2(x):
    hi = x.astype(BF16)
    lo = (x - hi.astype(F32)).astype(BF16)
    return hi, lo


def _split3(x):
    hi = x.astype(BF16)
    r = x - hi.astype(F32)
    mid = r.astype(BF16)
    lo = (r - mid.astype(F32)).astype(BF16)
    return hi, mid, lo


def _sort_key(score):
    bits = pltpu.bitcast(score, jnp.int32)
    return bits ^ ((bits >> 31) & np.int32(0x7FFFFFFF))


def _const_spec(shape):
    nd = len(shape)
    return pl.BlockSpec(shape, lambda *_: (0,) * nd, pipeline_mode=pl.Buffered(1))


def _adaln_kernel(c_ref, w_ref, b_ref, o_ref):
    c = c_ref[...]
    a1, a2, a3 = _split3(c * jax.nn.sigmoid(c))
    w1, w2, w3 = _split3(w_ref[...])
    small = _dot(a1, w3) + _dot(a2, w2) + _dot(a3, w1)
    mid = _dot(a1, w2) + _dot(a2, w1)
    o_ref[...] = (small + mid) + _dot(a1, w1) + b_ref[...]


def _adaln(c_all, w_ada, b_ada):
    rows = c_all.shape[0]
    tn = 512
    return pl.pallas_call(
        _adaln_kernel,
        grid=(3 * D_MODEL // tn,),
        in_specs=[pl.BlockSpec((rows, D_MODEL), lambda j: (0, 0)),
                  pl.BlockSpec((D_MODEL, tn), lambda j: (0, j)),
                  pl.BlockSpec((1, tn), lambda j: (0, j))],
        out_specs=pl.BlockSpec((rows, tn), lambda j: (0, j)),
        out_shape=jax.ShapeDtypeStruct((rows, 3 * D_MODEL), F32),
        name="adaln",
    )(c_all, w_ada, b_ada.reshape(1, -1))


_PROJ_OUTS = (
    ("qa", 512, BF16), ("k32", 512, F32), ("v32", 512, F32), ("k16", 512, BF16), ("v16", 512, BF16),
    ("sza", 512, BF16), ("qcat", 1024, BF16), ("kcat", 256, BF16), ("ki32", 64, F32), ("wi", 128, F32),
    ("qr", 512, BF16), ("kr", 512, BF16), ("vr", 512, BF16), ("szr", 512, BF16),
    ("sga", 1024, BF16), ("sgr", 1024, BF16),
)


def _proj_kernel(x_ref, scale_ref, shift_ref, gn_ref, cos_ref, sin_ref, wm_ref, wi1_ref, wi2_ref,
                 qa_ref, k32_ref, v32_ref, k16_ref, v16_ref, sza_ref, qcat_ref, kcat_ref, ki32_ref, wi_ref,
                 qr_ref, kr_ref, vr_ref, szr_ref, sga_ref, sgr_ref):
    x = x_ref[0]
    r = lax.rsqrt(jnp.mean(x * x, axis=-1, keepdims=True) + EPS)
    h = (x * r) * gn_ref[...] * (1.0 + scale_ref[0]) + shift_ref[0]
    h1, h2 = _split2(h)

    def main(g):
        return _dot(h1, wm_ref[:, g * 512:(g + 1) * 512])

    qa_ref[0] = main(0).astype(BF16)
    u = main(1)
    k32_ref[0] = u
    k16_ref[0] = u.astype(BF16)
    u = main(2)
    v32_ref[0] = u
    v16_ref[0] = u.astype(BF16)
    u = main(3)
    sza_ref[0] = (u * jax.nn.sigmoid(u)).astype(BF16)

    cos = cos_ref[...]
    sin = sin_ref[...]
    for g, o_ref in ((4, qr_ref), (5, kr_ref)):
        u = main(g)
        for grp in range(2):
            x1 = u[:, 256 * grp:256 * grp + 128]
            x2 = u[:, 256 * grp + 128:256 * grp + 256]
            o_ref[0, :, 256 * grp:256 * grp + 128] = (x1 * cos - x2 * sin).astype(BF16)
            o_ref[0, :, 256 * grp + 128:256 * grp + 256] = (x1 * sin + x2 * cos).astype(BF16)
    vr_ref[0] = main(6).astype(BF16)
    u = main(7)
    szr_ref[0] = (u * jax.nn.sigmoid(u)).astype(BF16)
    for j in range(2):
        sga_ref[0, :, 512 * j:512 * (j + 1)] = jax.nn.sigmoid(main(8 + j)).astype(BF16)
        sgr_ref[0, :, 512 * j:512 * (j + 1)] = jax.nn.sigmoid(main(10 + j)).astype(BF16)

    w1 = wi1_ref[...]
    w2 = wi2_ref[...]
    ui = (_dot(h2, w2) + _dot(h2, w1) + _dot(h1, w2)) + _dot(h1, w1)
    qd = ui[:, :512]
    q_hi, q_lo = _split2(qd)
    for hh in range(N_HEADS_IDX):
        qcat_ref[0, :, 256 * hh:256 * hh + 128] = q_hi[:, 128 * hh:128 * hh + 128]
        qcat_ref[0, :, 256 * hh + 128:256 * hh + 256] = q_lo[:, 128 * hh:128 * hh + 128]
    kd = ui[:, 512:640]
    k_hi, k_lo = _split2(kd)
    lane = lax.broadcasted_iota(jnp.int32, kd.shape, 1)
    sel = jnp.where(lane < HEAD_DIM_IDX, k_hi, k_lo)
    kcat_ref[0, :, 0:128] = sel
    kcat_ref[0, :, 128:256] = sel
    ki32_ref[0] = kd[:, :HEAD_DIM_IDX]
    wi_ref[0] = ui[:, 640:768]


def _project(x3, scale3, shift3, g_norm, cos_t, sin_t, w_main, wi1, wi2, tm):
    nb, rows, _ = x3.shape
    srows = scale3.shape[1]
    stile = 1 if srows == 1 else tm
    grid = (nb, rows // tm)
    row_spec = lambda w: pl.BlockSpec((1, tm, w), lambda b, i: (b, i, 0))
    s_spec = pl.BlockSpec((1, stile, D_MODEL), (lambda b, i: (b, 0, 0)) if srows == 1 else (lambda b, i: (b, i, 0)))
    in_specs = [row_spec(D_MODEL), s_spec, s_spec, _const_spec((1, D_MODEL)),
                pl.BlockSpec((tm, LANES), lambda b, i: (i, 0)), pl.BlockSpec((tm, LANES), lambda b, i: (i, 0)),
                _const_spec((D_MODEL, N_MAIN)), _const_spec((D_MODEL, N_IDX)), _const_spec((D_MODEL, N_IDX))]
    outs = pl.pallas_call(
        _proj_kernel,
        grid=grid,
        in_specs=in_specs,
        out_specs=[row_spec(w) for _, w, _ in _PROJ_OUTS],
        out_shape=[jax.ShapeDtypeStruct((nb, rows, w), dt) for _, w, dt in _PROJ_OUTS],
        compiler_params=pltpu.CompilerParams(dimension_semantics=("parallel", "parallel"),
                                             vmem_limit_bytes=52 * 1024 * 1024),
        name="proj",
    )(x3, scale3, shift3, g_norm.reshape(1, -1), cos_t, sin_t, w_main, wi1, wi2)
    return {name: o for (name, _, _), o in zip(_PROJ_OUTS, outs)}


TQ = 256


def _kth_largest_key(count_ge, rows):
    prefix = jnp.full((rows, 1), INT_MIN, jnp.int32)
    for b in range(32):
        bit = INT_MIN if b == 0 else np.int32(1 << (31 - b))
        cand = prefix ^ bit
        prefix = jnp.where(count_ge(cand) >= float(TOPK), cand, prefix)
    return prefix


def _attn_prompt_kernel(qcat_ref, wi_ref, qa_ref, sza_ref, kcat_ref, k_ref, v_ref, tri_ref, o_ref,
                        keys_ref, qz_ref, m_ref, l_ref, acc_ref):
    qi = pl.program_id(1)
    nch = qi + 1
    w = wi_ref[0]

    def chunk(ref, c):
        return ref[0, pl.ds(pl.multiple_of(c * TQ, TQ), TQ), :]

    def score_chunk(c, diagonal):
        kc = chunk(kcat_ref, c)
        acc = jnp.zeros((TQ, TQ), F32)
        for hh in range(N_HEADS_IDX):
            s = _dot_nt(qcat_ref[0, :, 256 * hh:256 * (hh + 1)], kc)
            acc = acc + jnp.maximum(s, 0.0) * w[:, hh:hh + 1]
        if diagonal:
            row = lax.broadcasted_iota(jnp.int32, (TQ, TQ), 0)
            col = lax.broadcasted_iota(jnp.int32, (TQ, TQ), 1)
            acc = jnp.where(col <= row, acc, -jnp.inf)
        keys_ref[c] = _sort_key(acc)

    def score_body(c, carry):
        score_chunk(c, False)
        return carry

    lax.fori_loop(0, qi, score_body, 0)
    score_chunk(qi, True)

    def count(pred):
        def body(c, acc):
            m = jnp.where(pred(keys_ref[c]), 1.0, 0.0)
            return acc + (m[:, :LANES] + m[:, LANES:])
        acc = lax.fori_loop(0, nch, body, jnp.zeros((TQ, LANES), F32))
        return jnp.sum(acc, axis=1, keepdims=True)

    thr = _kth_largest_key(lambda cand: count(lambda kk: kk >= cand), TQ)
    need = float(TOPK) - count(lambda kk: kk > thr)

    lane = lax.broadcasted_iota(jnp.int32, (TQ, LANES), 1)
    for h in range(N_HEADS_A):
        pair = qa_ref[0, :, LANES * (h // 2):LANES * (h // 2 + 1)]
        mine = (lane < HEAD_DIM_A) if h % 2 == 0 else (lane >= HEAD_DIM_A)
        qz_ref[h] = jnp.where(mine, pair, jnp.zeros_like(pair))
    m_ref[...] = jnp.full(m_ref.shape, NEG_BIG, F32)
    l_ref[...] = jnp.zeros(l_ref.shape, F32)
    acc_ref[...] = jnp.zeros(acc_ref.shape, F32)

    def attn_body(c, ties_before):
        kk = keys_ref[c]
        eq = kk == thr
        eqf = jnp.where(eq, 1.0, 0.0)
        rank = _dot(eqf.astype(BF16), tri_ref[...]) + ties_before
        sel = ((kk > thr) | (eq & (rank < need))) & (kk != KEY_NEG_INF)
        kc = chunk(k_ref, c)
        vc = chunk(v_ref, c)
        for h in range(N_HEADS_A):
            lo = LANES * (h // 2)
            s = _dot_nt(qz_ref[h], kc[:, lo:lo + LANES])
            m_old = m_ref[h]
            m_new = jnp.maximum(m_old, jnp.max(jnp.where(sel, s, NEG_BIG), axis=1, keepdims=True))
            p = jnp.where(sel, jnp.exp(s - m_new), 0.0)
            alpha = jnp.exp(m_old - m_new)
            l_ref[h] = alpha * l_ref[h] + jnp.sum(p, axis=1, keepdims=True)
            acc_ref[h] = alpha * acc_ref[h] + _dot(p.astype(BF16), vc[:, lo:lo + LANES])
            m_ref[h] = m_new
        return ties_before + jnp.sum(eqf, axis=1, keepdims=True)

    lax.fori_loop(0, nch, attn_body, jnp.zeros((TQ, 1), F32))

    for j in range(N_HEADS_A // 2):
        even = acc_ref[2 * j] / l_ref[2 * j]
        odd = acc_ref[2 * j + 1] / l_ref[2 * j + 1]
        o = jnp.where(lane < HEAD_DIM_A, even, odd)
        gate = sza_ref[0, :, LANES * j:LANES * (j + 1)].astype(F32)
        o_ref[0, :, LANES * j:LANES * (j + 1)] = (o * gate).astype(BF16)


def _attn_prompt(p, tri):
    nb, t, _ = p["qa"].shape
    nq = t // TQ
    tile = lambda w: pl.BlockSpec((1, TQ, w), lambda b, i: (b, i, 0))
    full = lambda w: pl.BlockSpec((1, t, w), lambda b, i: (b, 0, 0), pipeline_mode=pl.Buffered(1))
    return pl.pallas_call(
        _attn_prompt_kernel,
        grid=(nb, nq),
        in_specs=[tile(1024), tile(LANES), tile(WIDTH_A), tile(WIDTH_A),
                  full(256), full(WIDTH_A), full(WIDTH_A), _const_spec((TQ, TQ))],
        out_specs=tile(WIDTH_A),
        out_shape=jax.ShapeDtypeStruct((nb, t, WIDTH_A), BF16),
        scratch_shapes=[pltpu.VMEM((nq, TQ, TQ), jnp.int32),
                        pltpu.VMEM((N_HEADS_A, TQ, LANES), BF16),
                        pltpu.VMEM((N_HEADS_A, TQ, 1), F32),
                        pltpu.VMEM((N_HEADS_A, TQ, 1), F32),
                        pltpu.VMEM((N_HEADS_A, TQ, LANES), F32)],
        compiler_params=pltpu.CompilerParams(dimension_semantics=("parallel", "arbitrary"),
                                             vmem_limit_bytes=52 * 1024 * 1024),
        name="attn_prompt",
    )(p["qcat"], p["wi"], p["qa"], p["sza"], p["kcat"], p["k16"], p["v16"], tri)


N_CH_S = N_PAGES + 1
ROWS_A = N_HEADS_A * DEC_PAD
ROWS_I = N_HEADS_IDX * DEC_PAD


def _attn_sample_kernel(pt_ref, qcat_ref, wi_ref, qa_ref, sza_ref, kin_ref, kn_ref, vn_ref, tri_ref,
                        ckidx_ref, ck_ref, cv_ref, o_ref,
                        kidx_buf, k_buf, v_buf, keys_ref, sel_ref, logit_ref, sems):
    b = pl.program_id(0)

    def page_copies(pg):
        phys = pt_ref[b * N_PAGES + pg]
        return (pltpu.make_async_copy(ckidx_ref.at[phys], kidx_buf.at[pg], sems.at[0]),
                pltpu.make_async_copy(ck_ref.at[phys], k_buf.at[pg], sems.at[1]),
                pltpu.make_async_copy(cv_ref.at[phys], v_buf.at[pg], sems.at[2]))

    def start_body(pg, carry):
        for cp in page_copies(pg):
            cp.start()
        return carry

    lax.fori_loop(0, N_PAGES, start_body, 0)

    @pl.when(b == 0)
    def _():
        kidx_buf[N_PAGES] = jnp.zeros((PAGE_SIZE, HEAD_DIM_IDX), F32)
        k_buf[N_PAGES] = jnp.zeros((PAGE_SIZE, WIDTH_A), F32)
        v_buf[N_PAGES] = jnp.zeros((PAGE_SIZE, WIDTH_A), F32)

    kidx_buf[N_PAGES, 0:DEC_PAD, :] = kin_ref[0]
    k_buf[N_PAGES, 0:DEC_PAD, :] = kn_ref[0]
    v_buf[N_PAGES, 0:DEC_PAD, :] = vn_ref[0]

    def wait_body(pg, carry):
        for cp in page_copies(pg):
            cp.wait()
        return carry

    lax.fori_loop(0, N_PAGES, wait_body, 0)

    w = wi_ref[0]
    qcat = qcat_ref[0].astype(F32)
    q_hi = jnp.concatenate([qcat[:, 256 * hh:256 * hh + 64] for hh in range(N_HEADS_IDX)], axis=0).astype(BF16)
    q_lo = jnp.concatenate([qcat[:, 256 * hh + 128:256 * hh + 192] for hh in range(N_HEADS_IDX)], axis=0).astype(BF16)
    trow = lax.broadcasted_iota(jnp.int32, (DEC_PAD, PAGE_SIZE), 0)
    tcol = lax.broadcasted_iota(jnp.int32, (DEC_PAD, PAGE_SIZE), 1)

    def score_page(pg, new_page):
        k_hi, k_lo = _split2(kidx_buf[pg])
        s = (_dot_nt(q_lo, k_lo) + _dot_nt(q_lo, k_hi) + _dot_nt(q_hi, k_lo)) + _dot_nt(q_hi, k_hi)
        acc = jnp.zeros((DEC_PAD, PAGE_SIZE), F32)
        for hh in range(N_HEADS_IDX):
            acc = acc + jnp.maximum(s[DEC_PAD * hh:DEC_PAD * (hh + 1)], 0.0) * w[:, hh:hh + 1]
        if new_page:
            acc = jnp.where((tcol <= trow) & (tcol < DEC_SEQ), acc, -jnp.inf)
        keys_ref[pg] = _sort_key(acc)

    def score_body(pg, carry):
        score_page(pg, False)
        return carry

    lax.fori_loop(0, N_PAGES, score_body, 0)
    score_page(N_PAGES, True)

    def count(pred):
        m = jnp.where(pred(keys_ref[...]), 1.0, 0.0)
        return jnp.sum(jnp.sum(m, axis=0), axis=1, keepdims=True)

    thr = _kth_largest_key(lambda cand: count(lambda kk: kk >= cand[None]), DEC_PAD)
    need = float(TOPK) - count(lambda kk: kk > thr[None])

    lane5 = lax.broadcasted_iota(jnp.int32, (DEC_PAD, WIDTH_A), 1)
    qa = qa_ref[0].astype(F32)
    qbd = jnp.concatenate(
        [jnp.where(lane5 // HEAD_DIM_A == h, qa, 0.0) for h in range(N_HEADS_A)], axis=0).astype(BF16)

    def logit_body(pg, ties_before):
        kk = keys_ref[pg]
        eq = kk == thr
        eqf = jnp.where(eq, 1.0, 0.0)
        rank = _dot(eqf.astype(BF16), tri_ref[...]) + ties_before
        sel = ((kk > thr) | (eq & (rank < need))) & (kk != KEY_NEG_INF)
        self_ = jnp.where(sel, 1.0, 0.0)
        sel_ref[pg] = self_
        s = _dot_nt(qbd, k_buf[pg].astype(BF16))
        mask = jnp.concatenate([self_] * N_HEADS_A, axis=0) > 0.5
        logit_ref[pg] = jnp.where(mask, s, NEG_BIG)
        return ties_before + jnp.sum(eqf, axis=1, keepdims=True)

    lax.fori_loop(0, N_CH_S, logit_body, jnp.zeros((DEC_PAD, 1), F32))

    m = jnp.max(jnp.max(logit_ref[...], axis=0), axis=1, keepdims=True)

    def pv_body(pg, carry):
        l, acc = carry
        mask = jnp.concatenate([sel_ref[pg]] * N_HEADS_A, axis=0) > 0.5
        p = jnp.where(mask, jnp.exp(logit_ref[pg] - m), 0.0)
        l = l + jnp.sum(p, axis=1, keepdims=True)
        acc = acc + _dot(p.astype(BF16), v_buf[pg].astype(BF16))
        return l, acc

    l, acc = lax.fori_loop(0, N_CH_S, pv_body,
                           (jnp.zeros((ROWS_A, 1), F32), jnp.zeros((ROWS_A, WIDTH_A), F32)))
    o = acc / l
    out = jnp.zeros((DEC_PAD, WIDTH_A), F32)
    for h in range(N_HEADS_A):
        out = jnp.where(lane5 // HEAD_DIM_A == h, o[DEC_PAD * h:DEC_PAD * (h + 1)], out)
    o_ref[0] = (out * sza_ref[0].astype(F32)).astype(BF16)


def _attn_sample(p, page_table, cache_kidx, cache_k, cache_v, tri):
    nb = p["qa"].shape[0]
    row = lambda w: pl.BlockSpec((1, DEC_PAD, w), lambda b, pt: (b, 0, 0))
    any_spec = pl.BlockSpec(memory_space=pl.ANY)
    grid_spec = pltpu.PrefetchScalarGridSpec(
        num_scalar_prefetch=1,
        grid=(nb,),
        in_specs=[row(1024), row(LANES), row(WIDTH_A), row(WIDTH_A), row(HEAD_DIM_IDX), row(WIDTH_A), row(WIDTH_A),
                  pl.BlockSpec((PAGE_SIZE, PAGE_SIZE), lambda b, pt: (0, 0)),
                  any_spec, any_spec, any_spec],
        out_specs=row(WIDTH_A),
        scratch_shapes=[pltpu.VMEM((N_CH_S, PAGE_SIZE, HEAD_DIM_IDX), F32),
                        pltpu.VMEM((N_CH_S, PAGE_SIZE, WIDTH_A), F32),
                        pltpu.VMEM((N_CH_S, PAGE_SIZE, WIDTH_A), F32),
                        pltpu.VMEM((N_CH_S, DEC_PAD, PAGE_SIZE), jnp.int32),
                        pltpu.VMEM((N_CH_S, DEC_PAD, PAGE_SIZE), F32),
                        pltpu.VMEM((N_CH_S, ROWS_A, PAGE_SIZE), F32),
                        pltpu.SemaphoreType.DMA((3,))],
    )
    return pl.pallas_call(
        _attn_sample_kernel,
        grid_spec=grid_spec,
        out_shape=jax.ShapeDtypeStruct((nb, DEC_PAD, WIDTH_A), BF16),
        compiler_params=pltpu.CompilerParams(dimension_semantics=("arbitrary",),
                                             vmem_limit_bytes=52 * 1024 * 1024),
        name="attn_sample",
    )(page_table.reshape(-1), p["qcat"], p["wi"], p["qa"], p["sza"], p["ki32"], p["k32"], p["v32"], tri,
      cache_kidx, cache_k, cache_v)


def _retention_kernel(q_ref, k_ref, v_ref, s0_ref, dmat_ref, qdec_ref, kdec_ref, gc_ref, bd_ref, hm_ref,
                      ret_ref, sout_ref, state_ref):
    c = pl.program_id(1)

    @pl.when(c == 0)
    def _():
        state_ref[...] = s0_ref[0]

    q = q_ref[0]
    k = k_ref[0]
    v = v_ref[0]
    rows = q.shape[0]
    vk = (v.astype(F32) * kdec_ref[...]).astype(BF16)
    lane = lax.broadcasted_iota(jnp.int32, (rows, LANES), 1)
    for g in range(2):
        gs = slice(256 * g, 256 * (g + 1))
        qg = q[:, gs]
        kg = k[:, gs]
        sg = state_ref[g]
        cross = _dot(qg, sg.astype(BF16)) * qdec_ref[:, gs]
        for j in range(2):
            pair = 2 * g + j
            vp = v[:, LANES * pair:LANES * (pair + 1)]
            halves = []
            for e in range(2):
                hl = 2 * j + e
                qz = qg * hm_ref[hl]
                sc = _dot_nt(qz, kg) * dmat_ref[4 * g + hl]
                halves.append(_dot(sc.astype(BF16), vp))
            inner = jnp.where(lane < DK_R, halves[0], halves[1])
            ret_ref[0, :, LANES * pair:LANES * (pair + 1)] = inner + cross[:, LANES * j:LANES * (j + 1)]
        kv = _dot_tn(kg, vk[:, gs])
        state_ref[g] = sg * gc_ref[:, gs] + kv * bd_ref[...]

    @pl.when(c == pl.num_programs(1) - 1)
    def _():
        sout_ref[0] = state_ref[...]


def _retention(q, k, v, state0, tables, chunk):
    nb, rows, _ = q.shape
    nchunk = rows // chunk
    dmat, qdec, kdec, gc, bd, hm = tables
    tile = pl.BlockSpec((1, chunk, WIDTH_R), lambda b, c: (b, c, 0))
    st = pl.BlockSpec((1, 2, 256, 256), lambda b, c: (b, 0, 0, 0))
    return pl.pallas_call(
        _retention_kernel,
        grid=(nb, nchunk),
        in_specs=[tile, tile, tile, st,
                  _const_spec((N_HEADS_R, chunk, chunk)), _const_spec((chunk, WIDTH_R)), _const_spec((chunk, WIDTH_R)),
                  _const_spec((1, WIDTH_R)), _const_spec((256, 256)), _const_spec((4, 1, 256))],
        out_specs=[tile, st],
        out_shape=[jax.ShapeDtypeStruct((nb, rows, WIDTH_R), F32),
                   jax.ShapeDtypeStruct((nb, 2, 256, 256), F32)],
        scratch_shapes=[pltpu.VMEM((2, 256, 256), F32)],
        compiler_params=pltpu.CompilerParams(dimension_semantics=("parallel", "arbitrary")),
        name="retention",
    )(q, k, v, state0, dmat, qdec, kdec, gc, bd, hm)


def _retention_tables(chunk, n_real):
    lg = jnp.log1p(-jnp.exp2(-5.0 - jnp.arange(N_HEADS_R, dtype=F32)))
    i = jnp.arange(chunk, dtype=F32)
    real = jnp.arange(chunk) < n_real
    diff = i[:, None] - i[None, :]
    dmat = jnp.where(diff >= 0, jnp.exp(lg[:, None, None] * jnp.maximum(diff, 0.0)), 0.0)
    dmat = jnp.where(real[None, None, :], dmat, 0.0)
    q_decay = jnp.exp(lg[None, :] * (i[:, None] + 1.0))
    k_decay = jnp.where(real[:, None], jnp.exp(lg[None, :] * (n_real - 1.0 - i)[:, None]), 0.0)
    per_lane = lambda a: jnp.repeat(a, WIDTH_R // N_HEADS_R, axis=-1)
    gc = per_lane(jnp.exp(lg * n_real)[None, :])
    r = np.arange(256)
    bd = ((r[:, None] % 128) // 32 == (r[None, :] // 64)).astype(np.float32)
    hm = np.stack([((r % 128) // 32 == hl) for hl in range(4)]).astype(np.float32).reshape(4, 1, 256)
    return (dmat, per_lane(q_decay), per_lane(k_decay), gc, jnp.asarray(bd), jnp.asarray(hm, dtype=BF16))


def _merge_kernel(x_ref, gate_ref, ya_ref, ret_ref, szr_ref, sga_ref, sgr_ref,
                  avg_ref, gret_ref, wpa_ref, wpr_ref, wout_ref, gfin_ref, y_ref):
    ret = ret_ref[0]
    avg = avg_ref[...]
    r1, r2 = _split2(ret)
    dev = ret - (_dot(r2, avg) + _dot(r1, avg))
    e1, e2 = _split2(dev * dev)
    var = _dot(e2, avg) + _dot(e1, avg)
    yr = (dev * lax.rsqrt(var + EPS)) * gret_ref[...] * szr_ref[0].astype(F32)
    merged = (sga_ref[0].astype(F32) * _dot(ya_ref[0], wpa_ref[...])
              + sgr_ref[0].astype(F32) * _dot(yr.astype(BF16), wpr_ref[...]))
    xo = x_ref[0] + gate_ref[0] * _dot(merged.astype(BF16), wout_ref[...])
    r = lax.rsqrt(jnp.mean(xo * xo, axis=-1, keepdims=True) + EPS)
    y_ref[0] = (xo * r) * gfin_ref[...]


def _merge(x3, gate3, ya, ret, p, avg, g_ret, w_pa, w_pr, w_out, g_final, tm):
    nb, rows, _ = x3.shape
    srows = gate3.shape[1]
    row = lambda w: pl.BlockSpec((1, tm, w), lambda b, i: (b, i, 0))
    g_spec = pl.BlockSpec((1, 1 if srows == 1 else tm, D_MODEL),
                          (lambda b, i: (b, 0, 0)) if srows == 1 else (lambda b, i: (b, i, 0)))
    return pl.pallas_call(
        _merge_kernel,
        grid=(nb, rows // tm),
        in_specs=[row(D_MODEL), g_spec, row(WIDTH_A), row(WIDTH_R), row(WIDTH_R), row(D_MODEL), row(D_MODEL),
                  _const_spec((WIDTH_R, WIDTH_R)), _const_spec((1, WIDTH_R)),
                  _const_spec((WIDTH_A, D_MODEL)), _const_spec((WIDTH_R, D_MODEL)), _const_spec((D_MODEL, D_MODEL)),
                  _const_spec((1, D_MODEL))],
        out_specs=row(D_MODEL),
        out_shape=jax.ShapeDtypeStruct((nb, rows, D_MODEL), F32),
        compiler_params=pltpu.CompilerParams(dimension_semantics=("parallel", "parallel"),
                                             vmem_limit_bytes=48 * 1024 * 1024),
        name="merge",
    )(x3, gate3, ya, ret, p["szr"], p["sga"], p["sgr"], avg, g_ret.reshape(1, -1),
      w_pa, w_pr, w_out, g_final.reshape(1, -1))


def _group_cols(w):
    d = w.shape[0]
    return w.reshape(d, 2, 4, 2, 32).transpose(0, 1, 3, 2, 4).reshape(d, 512)


def _prep_weights(w_in):
    s = lambda o, n: w_in[:, o:o + n]
    main = jnp.concatenate([
        s(_O_QA, 512) * HEAD_DIM_A ** -0.5, s(_O_KA, 512), s(_O_VA, 512), s(_O_ZA, 512),
        _group_cols(s(_O_QR, 512)), _group_cols(s(_O_KR, 512)) * DK_R ** -0.5, s(_O_VR, 512), s(_O_ZR, 512),
        s(_O_GA, 1024), s(_O_GR, 1024)], axis=1).astype(BF16)
    qi = s(_O_QI, 256).reshape(D_MODEL, N_HEADS_IDX, 1, HEAD_DIM_IDX)
    qi = jnp.broadcast_to(qi, (D_MODEL, N_HEADS_IDX, 2, HEAD_DIM_IDX)).reshape(D_MODEL, 512)
    ki = s(_O_KI, 64)
    wi = s(_O_WI, N_HEADS_IDX) * (N_HEADS_IDX ** -0.5 * HEAD_DIM_IDX ** -0.5)
    idx = jnp.concatenate([qi, ki, ki, wi, jnp.zeros((D_MODEL, LANES - N_HEADS_IDX), F32)], axis=1)
    i1 = idx.astype(BF16)
    i2 = (idx - i1.astype(F32)).astype(BF16)
    return main, i1, i2


def _rope_tables(pos):
    half = DK_R // 2
    inv = ROPE_BASE ** (-jnp.arange(half, dtype=F32) / half)
    ang = pos.astype(F32)[:, None] * inv[None, :]
    return jnp.tile(jnp.cos(ang), (1, 4)), jnp.tile(jnp.sin(ang), (1, 4))


def _state_to_groups(state):
    nb = state.shape[0]
    st = state.reshape(nb, 2, 4, 2, 32, 64)
    eye = jnp.eye(4, dtype=state.dtype)
    out = st.transpose(0, 1, 3, 2, 4, 5)[:, :, :, :, :, None, :] * eye[None, None, None, :, None, :, None]
    return out.reshape(nb, 2, 256, 256)


def _groups_to_state(sg):
    nb = sg.shape[0]
    s7 = sg.reshape(nb, 2, 2, 4, 32, 4, 64)
    diag = jnp.stack([s7[:, :, :, hl, :, hl, :] for hl in range(4)], axis=2)
    return diag.reshape(nb, N_HEADS_R, DK_R, 64)


def kernel(x_prompt, x_sample, cache_k, cache_v, cache_kidx, state_ret, page_table, c_prompt, c_sample,
           w_ada, b_ada, g_norm, w_in, g_ret, w_pa, w_pr, w_out, g_final):
    nbp, t, _ = x_prompt.shape
    nbs = x_sample.shape[0]
    depth = w_in.shape[0]
    assert depth == 1

    r = np.arange(TQ)
    tri256 = jnp.asarray((r[:, None] < r[None, :]).astype(np.float32), dtype=BF16)
    tri128 = tri256[:PAGE_SIZE, :PAGE_SIZE]
    c5 = np.arange(WIDTH_R)
    avg = jnp.asarray((c5[:, None] // 64 == c5[None, :] // 64).astype(np.float32) / 64.0, dtype=BF16)
    cos_p, sin_p = _rope_tables(jnp.arange(t))
    cos_s, sin_s = _rope_tables(jnp.tile(PAST_LEN + jnp.arange(DEC_PAD), nbs))
    tab_p = _retention_tables(CHUNK_R, CHUNK_R)
    tab_s = _retention_tables(DEC_PAD, DEC_SEQ)

    hp = x_prompt
    hs = jnp.pad(x_sample, ((0, 0), (0, DEC_PAD - DEC_SEQ), (0, 0))).reshape(1, nbs * DEC_PAD, D_MODEL)
    outs_p, outs_s = [], []
    for l in range(depth):
        w_main, wi1, wi2 = _prep_weights(w_in[l])
        wpa, wpr, wout = w_pa[l].astype(BF16), w_pr[l].astype(BF16), w_out[l].astype(BF16)

        c_all = jnp.concatenate([c_prompt, c_sample, jnp.zeros((6, D_MODEL), F32)], axis=0)
        mod = _adaln(c_all, w_ada[l], b_ada[l])
        shift, scale, gate = mod[:, :D_MODEL], mod[:, D_MODEL:2 * D_MODEL], mod[:, 2 * D_MODEL:]
        per_p = lambda a: a[:nbp].reshape(nbp, 1, D_MODEL)
        per_s = lambda a: jnp.broadcast_to(a[nbp:nbp + nbs, None, :], (nbs, DEC_PAD, D_MODEL)).reshape(
            1, nbs * DEC_PAD, D_MODEL)

        pp = _project(hp, per_p(scale), per_p(shift), g_norm[l], cos_p, sin_p, w_main, wi1, wi2, tm=256)
        ya_p = _attn_prompt(pp, tri256)
        ret_p, sg_p = _retention(pp["qr"], pp["kr"], pp["vr"], jnp.zeros((nbp, 2, 256, 256), F32), tab_p, CHUNK_R)
        hp = _merge(hp, per_p(gate), ya_p, ret_p, pp, avg, g_ret[l], wpa, wpr, wout, g_final, tm=256)

        ps = _project(hs, per_s(scale), per_s(shift), g_norm[l], cos_s, sin_s, w_main, wi1, wi2, tm=nbs * DEC_PAD)
        ps3 = {k_: v_.reshape(nbs, DEC_PAD, v_.shape[-1]) for k_, v_ in ps.items()}
        ya_s = _attn_sample(ps3, page_table, cache_kidx[l],
                            cache_k[l].reshape(-1, PAGE_SIZE, WIDTH_A), cache_v[l].reshape(-1, PAGE_SIZE, WIDTH_A),
                            tri128)
        ret_s, sg_s = _retention(ps3["qr"], ps3["kr"], ps3["vr"], _state_to_groups(state_ret[l].astype(F32)),
                                 tab_s, DEC_PAD)
        hs = _merge(hs, per_s(gate), ya_s.reshape(1, nbs * DEC_PAD, WIDTH_A),
                    ret_s.reshape(1, nbs * DEC_PAD, WIDTH_R), ps, avg, g_ret[l], wpa, wpr, wout, g_final,
                    tm=nbs * DEC_PAD)

        heads = lambda a, n: a.reshape(n, -1, N_HEADS_A, HEAD_DIM_A)
        outs_p.append((heads(pp["k32"], nbp), heads(pp["v32"], nbp), pp["ki32"], _groups_to_state(sg_p)))
        tok = lambda a: a[:, :DEC_SEQ]
        outs_s.append((heads(tok(ps3["k32"]), nbs), heads(tok(ps3["v32"]), nbs), tok(ps3["ki32"]),
                       _groups_to_state(sg_s)))

    y_prompt = hp
    y_sample = hs.reshape(nbs, DEC_PAD, D_MODEL)[:, :DEC_SEQ]
    stack = lambda items, i: jnp.stack([it[i] for it in items])
    return (y_prompt, y_sample,
            stack(outs_p, 0), stack(outs_p, 1), stack(outs_p, 2), stack(outs_p, 3),
            stack(outs_s, 0), stack(outs_s, 1), stack(outs_s, 2), stack(outs_s, 3))
```

```python
import functools

import numpy as np
import jax
import jax.numpy as jnp
from jax import lax
from jax.experimental import pallas as pl
from jax.experimental.pallas import tpu as pltpu

D_MODEL = 1024
SEQ = 8192
DEC_SEQ = 4
PAST_LEN = 8192
PAGE_SIZE = 128
N_HEADS_A = 8
HEAD_DIM_A = 64
WIDTH_A = 512
N_HEADS_IDX = 4
HEAD_DIM_IDX = 64
TOPK = 256
N_HEADS_R = 8
DK_R = 64
WIDTH_R = 512
CHUNK_R = 128
ROPE_BASE = 10000.0
EPS = 1e-6

LANES = 128
SUBLANES = 8
DEC_PAD = SUBLANES
N_PAGES = PAST_LEN // PAGE_SIZE

_O_QA, _O_KA, _O_VA, _O_ZA = 0, 512, 1024, 1536
_O_QI, _O_KI, _O_WI = 2048, 2304, 2368
_O_QR, _O_KR, _O_VR, _O_ZR = 2372, 2884, 3396, 3908
_O_GA, _O_GR, _N_IN = 4420, 5444, 6468

N_MAIN = 6144
N_IDX = 768

INT_MIN = np.int32(-2 ** 31)
KEY_NEG_INF = np.int32(np.array(0xFF800000, np.uint32).view(np.int32) ^ np.int32(0x7FFFFFFF))
NEG_BIG = -1e30
M_INIT = -1e29
LOG2E = 1.4426950408889634
KSLAB = 128
PACK16 = 16

F32 = jnp.float32
BF16 = jnp.bfloat16


def _dot(a, b):
    return jnp.dot(a, b, preferred_element_type=F32)


def _dot_nt(a, b):
    return lax.dot_general(a, b, (((1,), (1,)), ((), ())), preferred_element_type=F32)


def _dot_tn(a, b):
    return lax.dot_general(a, b, (((0,), (0,)), ((), ())), preferred_element_type=F32)


def _split2(x):
    hi = x.astype(BF16)
    lo = (x - hi.astype(F32)).astype(BF16)
    return hi, lo


def _split3(x):
    hi = x.astype(BF16)
    r = x - hi.astype(F32)
    mid = r.astype(BF16)
    lo = (r - mid.astype(F32)).astype(BF16)
    return hi, mid, lo


def _sort_key(score):
    bits = pltpu.bitcast(score, jnp.int32)
    return bits ^ ((bits >> 31) & np.int32(0x7FFFFFFF))


def _const_spec(shape):
    nd = len(shape)
    return pl.BlockSpec(shape, lambda *_: (0,) * nd, pipeline_mode=pl.Buffered(1))


def _adaln_kernel(c_ref, w_ref, b_ref, o_ref):
    c = c_ref[...]
    a1, a2, a3 = _split3(c * jax.nn.sigmoid(c))
    w1, w2, w3 = _split3(w_ref[...])
    small = _dot(a1, w3) + _dot(a2, w2) + _dot(a3, w1)
    mid = _dot(a1, w2) + _dot(a2, w1)
    o_ref[...] = (small + mid) + _dot(a1, w1) + b_ref[...]


def _adaln(c_all, w_ada, b_ada):
    rows = c_all.shape[0]
    tn = 512
    return pl.pallas_call(
        _adaln_kernel,
        grid=(3 * D_MODEL // tn,),
        in_specs=[pl.BlockSpec((rows, D_MODEL), lambda j: (0, 0)),
                  pl.BlockSpec((D_MODEL, tn), lambda j: (0, j)),
                  pl.BlockSpec((1, tn), lambda j: (0, j))],
        out_specs=pl.BlockSpec((rows, tn), lambda j: (0, j)),
        out_shape=jax.ShapeDtypeStruct((rows, 3 * D_MODEL), F32),
        name="adaln",
    )(c_all, w_ada, b_ada.reshape(1, -1))


_PROJ_OUTS = (
    ("qa", 512, BF16), ("k32", 512, F32), ("v32", 512, F32), ("k16", 512, BF16), ("vt", None, BF16),
    ("sza", 512, BF16), ("qcat", 1024, BF16), ("kcat", 256, BF16), ("ki32", 64, F32), ("wi", 128, F32),
    ("wit", None, F32),
    ("qr", 512, BF16), ("kr", 512, BF16), ("vr", 512, BF16), ("szr", 512, BF16),
    ("sga", 1024, BF16), ("sgr", 1024, BF16),
)
_PROJ_T = {"vt": (WIDTH_A, KSLAB), "wit": (SUBLANES, None)}


def _proj_kernel(x_ref, scale_ref, shift_ref, gn_ref, cos_ref, sin_ref, wm_ref, wi1_ref, wi2_ref,
                 qa_ref, k32_ref, v32_ref, k16_ref, vt_ref, sza_ref, qcat_ref, kcat_ref, ki32_ref, wi_ref,
                 wit_ref, qr_ref, kr_ref, vr_ref, szr_ref, sga_ref, sgr_ref):
    x = x_ref[0]
    r = lax.rsqrt(jnp.mean(x * x, axis=-1, keepdims=True) + EPS)
    h = (x * r) * gn_ref[...] * (1.0 + scale_ref[0]) + shift_ref[0]
    h1, h2 = _split2(h)

    def main(g):
        return _dot(h1, wm_ref[:, g * 512:(g + 1) * 512])

    qa_ref[0] = main(0).astype(BF16)
    u = main(1)
    k32_ref[0] = u
    k16_ref[0] = u.astype(BF16)
    u = main(2)
    v32_ref[0] = u
    for sl in range(u.shape[0] // KSLAB):
        vt_ref[0, sl] = u[KSLAB * sl:KSLAB * (sl + 1)].T.astype(BF16)
    u = main(3)
    sza_ref[0] = (u * jax.nn.sigmoid(u)).astype(BF16)

    cos = cos_ref[...]
    sin = sin_ref[...]
    for g, o_ref in ((4, qr_ref), (5, kr_ref)):
        u = main(g)
        for grp in range(2):
            x1 = u[:, 256 * grp:256 * grp + 128]
            x2 = u[:, 256 * grp + 128:256 * grp + 256]
            o_ref[0, :, 256 * grp:256 * grp + 128] = (x1 * cos - x2 * sin).astype(BF16)
            o_ref[0, :, 256 * grp + 128:256 * grp + 256] = (x1 * sin + x2 * cos).astype(BF16)
    vr_ref[0] = main(6).astype(BF16)
    u = main(7)
    szr_ref[0] = (u * jax.nn.sigmoid(u)).astype(BF16)
    for j in range(2):
        sga_ref[0, :, 512 * j:512 * (j + 1)] = jax.nn.sigmoid(main(8 + j)).astype(BF16)
        sgr_ref[0, :, 512 * j:512 * (j + 1)] = jax.nn.sigmoid(main(10 + j)).astype(BF16)

    w1 = wi1_ref[...]
    w2 = wi2_ref[...]
    ui = (_dot(h2, w2) + _dot(h2, w1) + _dot(h1, w2)) + _dot(h1, w1)
    qd = ui[:, :512]
    q_hi, q_lo = _split2(qd)
    for hh in range(N_HEADS_IDX):
        qcat_ref[0, :, 256 * hh:256 * hh + 128] = q_hi[:, 128 * hh:128 * hh + 128]
        qcat_ref[0, :, 256 * hh + 128:256 * hh + 256] = q_lo[:, 128 * hh:128 * hh + 128]
    kd = ui[:, 512:640]
    k_hi, k_lo = _split2(kd)
    lane = lax.broadcasted_iota(jnp.int32, kd.shape, 1)
    sel = jnp.where(lane < HEAD_DIM_IDX, k_hi, k_lo)
    kcat_ref[0, :, 0:128] = sel
    kcat_ref[0, :, 128:256] = sel
    ki32_ref[0] = kd[:, :HEAD_DIM_IDX]
    wi_ref[0] = ui[:, 640:768]
    wit_ref[0, 0] = ui[:, 640:768].T[0:SUBLANES, :]


def _project(x3, scale3, shift3, g_norm, cos_t, sin_t, w_main, wi1, wi2, tm):
    nb, rows, _ = x3.shape
    srows = scale3.shape[1]
    stile = 1 if srows == 1 else tm
    grid = (nb, rows // tm)
    row_spec = lambda w: pl.BlockSpec((1, tm, w), lambda b, i: (b, i, 0))
    s_spec = pl.BlockSpec((1, stile, D_MODEL), (lambda b, i: (b, 0, 0)) if srows == 1 else (lambda b, i: (b, i, 0)))
    t_spec = lambda r, w: pl.BlockSpec((1, tm // (w or tm), r, w or tm), lambda b, i: (b, i, 0, 0))
    t_shape = lambda r, w: (nb, rows // (w or tm), r, w or tm)
    in_specs = [row_spec(D_MODEL), s_spec, s_spec, _const_spec((1, D_MODEL)),
                pl.BlockSpec((tm, LANES), lambda b, i: (i, 0)), pl.BlockSpec((tm, LANES), lambda b, i: (i, 0)),
                _const_spec((D_MODEL, N_MAIN)), _const_spec((D_MODEL, N_IDX)), _const_spec((D_MODEL, N_IDX))]
    outs = pl.pallas_call(
        _proj_kernel,
        grid=grid,
        in_specs=in_specs,
        out_specs=[row_spec(w) if w else t_spec(*_PROJ_T[n]) for n, w, _ in _PROJ_OUTS],
        out_shape=[jax.ShapeDtypeStruct((nb, rows, w) if w else t_shape(*_PROJ_T[n]), dt)
                   for n, w, dt in _PROJ_OUTS],
        compiler_params=pltpu.CompilerParams(dimension_semantics=("parallel", "parallel"),
                                             vmem_limit_bytes=52 * 1024 * 1024),
        name="proj",
    )(x3, scale3, shift3, g_norm.reshape(1, -1), cos_t, sin_t, w_main, wi1, wi2)
    return {name: o for (name, _, _), o in zip(_PROJ_OUTS, outs)}


TQ = 256


def _kth_largest_key(count_ge, shape, k=float(TOPK), nbits=32):
    lowest = -(1 << (nbits - 1))
    prefix = jnp.full(shape, lowest, jnp.int32)
    for b in range(nbits):
        bit = np.int32(lowest if b == 0 else 1 << (nbits - 1 - b))
        cand = prefix ^ bit
        prefix = jnp.where(count_ge(cand) >= k, cand, prefix)
    return prefix


def _attn_prompt_kernel(qcat_ref, wit_ref, qa_ref, sza_ref, kcat_ref, k_ref, vt_ref, tri_ref, o_ref,
                        keys_ref, hi_ref, lo_ref, qz_ref, m_ref, acc_ref):
    qi = pl.program_id(1)
    nch = qi + 1
    wt = wit_ref[0, 0]

    def rows(ref, c, n):
        return ref[0, pl.ds(pl.multiple_of(c * n, n), n), :]

    def score_chunk(c, diagonal):
        kc = rows(kcat_ref, c, TQ)
        acc = jnp.zeros((TQ, TQ), F32)
        for hh in range(N_HEADS_IDX):
            s = _dot_nt(kc, qcat_ref[0, :, 256 * hh:256 * (hh + 1)])
            acc = acc + jnp.maximum(s, 0.0) * wt[hh:hh + 1, :]
        if diagonal:
            key = lax.broadcasted_iota(jnp.int32, (TQ, TQ), 0)
            qry = lax.broadcasted_iota(jnp.int32, (TQ, TQ), 1)
            acc = jnp.where(key <= qry, acc, -jnp.inf)
        kk = _sort_key(acc)
        keys_ref[2 * c] = kk[:KSLAB]
        keys_ref[2 * c + 1] = kk[KSLAB:]
        hi_ref[c] = (kk >> 16).astype(jnp.int16)
        lo_ref[c] = (((kk ^ np.int32(0x8000)) << 16) >> 16).astype(jnp.int16)

    def score_body(c, carry):
        score_chunk(c, False)
        return carry

    lax.fori_loop(0, qi, score_body, 0)
    score_chunk(qi, True)

    def count16(ref, pred):
        def body(c, acc):
            m = jnp.where(pred(ref[c]), jnp.int16(1), jnp.int16(0))
            parts = [m[PACK16 * i:PACK16 * (i + 1)] for i in range(TQ // PACK16)]
            while len(parts) > 1:
                parts = [parts[i] + parts[i + 1] for i in range(0, len(parts), 2)]
            return acc + parts[0].astype(jnp.int32)
        acc = lax.fori_loop(0, nch, body, jnp.zeros((PACK16, TQ), jnp.int32))
        return jnp.sum(acc.astype(F32), axis=0, keepdims=True)

    as16 = lambda v: v.astype(jnp.int16)
    hi_thr = _kth_largest_key(lambda cand: count16(hi_ref, lambda x: x >= as16(cand)), (1, TQ), nbits=16)
    above = count16(hi_ref, lambda x: x > as16(hi_thr))

    def keep_tied_lows(c, carry):
        lo_ref[c] = jnp.where(hi_ref[c] == as16(hi_thr), lo_ref[c], jnp.int16(-2 ** 15))
        return carry

    lax.fori_loop(0, nch, keep_tied_lows, 0)
    lo_thr = _kth_largest_key(lambda cand: count16(lo_ref, lambda x: x >= as16(cand)), (1, TQ),
                              k=float(TOPK) - above, nbits=16)
    thr = (hi_thr << 16) | ((lo_thr + 2 ** 15) & np.int32(0xFFFF))
    need = float(TOPK) - above - count16(lo_ref, lambda x: x > as16(lo_thr))

    lane = lax.broadcasted_iota(jnp.int32, (TQ, LANES), 1)
    for h in range(N_HEADS_A):
        pair = qa_ref[0, :, LANES * (h // 2):LANES * (h // 2 + 1)]
        mine = (lane < HEAD_DIM_A) if h % 2 == 0 else (lane >= HEAD_DIM_A)
        qz_ref[h] = jnp.where(mine, pair, jnp.zeros_like(pair))
    m_ref[...] = jnp.full(m_ref.shape, M_INIT, F32)
    acc_ref[...] = jnp.zeros(acc_ref.shape, F32)
    dim16 = lax.broadcasted_iota(jnp.int32, (LANES, KSLAB), 0)

    def attn_body(sl, ties_before):
        kk = keys_ref[sl]
        eq = kk == thr
        eqf = jnp.where(eq, 1.0, 0.0)
        rank = _dot(tri_ref[...], eqf.astype(BF16)) + ties_before
        sel = ((kk > thr) | (eq & (rank < need))) & (kk != KEY_NEG_INF)
        kc = rows(k_ref, sl, KSLAB)
        vts = vt_ref[0, sl]
        for h in range(N_HEADS_A):
            lo = LANES * (h // 2)
            s = jnp.where(sel, _dot_nt(kc[:, lo:lo + LANES], qz_ref[h]), NEG_BIG)
            m_old = m_ref[h]
            m_new = jnp.maximum(m_old, jnp.max(s, axis=0, keepdims=True))
            p = jnp.exp2(s - m_new)
            mine = (dim16 < HEAD_DIM_A) if h % 2 == 0 else (dim16 >= HEAD_DIM_A)
            v1 = jnp.where(mine, vts[lo:lo + LANES, :], jnp.ones((), BF16))
            acc_ref[h] = jnp.exp2(m_old - m_new) * acc_ref[h] + _dot(v1, p.astype(BF16))
            m_ref[h] = m_new
        return ties_before + jnp.sum(eqf, axis=0, keepdims=True)

    lax.fori_loop(0, 2 * nch, attn_body, jnp.zeros((1, TQ), F32))

    dim = lax.broadcasted_iota(jnp.int32, (LANES, TQ), 0)
    for j in range(N_HEADS_A // 2):
        a0 = acc_ref[2 * j]
        a1 = acc_ref[2 * j + 1]
        even = a0 / a0[HEAD_DIM_A:HEAD_DIM_A + 1]
        odd = a1 / a1[0:1]
        o = jnp.where(dim < HEAD_DIM_A, even, odd).T
        gate = sza_ref[0, :, LANES * j:LANES * (j + 1)].astype(F32)
        o_ref[0, :, LANES * j:LANES * (j + 1)] = (o * gate).astype(BF16)


def _attn_prompt(p, tri):
    nb, t, _ = p["qa"].shape
    nq = t // TQ
    tile = lambda w: pl.BlockSpec((1, TQ, w), lambda b, i: (b, i, 0))
    full = lambda w: pl.BlockSpec((1, t, w), lambda b, i: (b, 0, 0), pipeline_mode=pl.Buffered(1))
    return pl.pallas_call(
        _attn_prompt_kernel,
        grid=(nb, nq),
        in_specs=[tile(1024), pl.BlockSpec((1, 1, SUBLANES, TQ), lambda b, i: (b, i, 0, 0)),
                  tile(WIDTH_A), tile(WIDTH_A), full(256), full(WIDTH_A),
                  pl.BlockSpec((1, t // KSLAB, WIDTH_A, KSLAB), lambda b, i: (b, 0, 0, 0),
                               pipeline_mode=pl.Buffered(1)),
                  _const_spec((KSLAB, KSLAB))],
        out_specs=tile(WIDTH_A),
        out_shape=jax.ShapeDtypeStruct((nb, t, WIDTH_A), BF16),
        scratch_shapes=[pltpu.VMEM((t // KSLAB, KSLAB, TQ), jnp.int32),
                        pltpu.VMEM((nq, TQ, TQ), jnp.int16),
                        pltpu.VMEM((nq, TQ, TQ), jnp.int16),
                        pltpu.VMEM((N_HEADS_A, TQ, LANES), BF16),
                        pltpu.VMEM((N_HEADS_A, 1, TQ), F32),
                        pltpu.VMEM((N_HEADS_A, LANES, TQ), F32)],
        compiler_params=pltpu.CompilerParams(dimension_semantics=("parallel", "arbitrary"),
                                             vmem_limit_bytes=52 * 1024 * 1024),
        name="attn_prompt",
    )(p["qcat"], p["wit"], p["qa"], p["sza"], p["kcat"], p["k16"], p["vt"], tri)


N_CH_S = N_PAGES + 1
ROWS_A = N_HEADS_A * DEC_PAD
ROWS_I = N_HEADS_IDX * DEC_PAD


def _attn_sample_kernel(pt_ref, qcat_ref, wi_ref, qa_ref, sza_ref, kin_ref, kn_ref, vn_ref, tri_ref,
                        ckidx_ref, ck_ref, cv_ref, o_ref,
                        kidx_buf, k_buf, v_buf, keys_ref, sel_ref, logit_ref, sems):
    b = pl.program_id(0)

    def page_copies(pg):
        phys = pt_ref[b * N_PAGES + pg]
        return (pltpu.make_async_copy(ckidx_ref.at[phys], kidx_buf.at[pg], sems.at[0]),
                pltpu.make_async_copy(ck_ref.at[phys], k_buf.at[pg], sems.at[1]),
                pltpu.make_async_copy(cv_ref.at[phys], v_buf.at[pg], sems.at[2]))

    def start_body(pg, carry):
        for cp in page_copies(pg):
            cp.start()
        return carry

    lax.fori_loop(0, N_PAGES, start_body, 0)

    @pl.when(b == 0)
    def _():
        kidx_buf[N_PAGES] = jnp.zeros((PAGE_SIZE, HEAD_DIM_IDX), F32)
        k_buf[N_PAGES] = jnp.zeros((PAGE_SIZE, WIDTH_A), F32)
        v_buf[N_PAGES] = jnp.zeros((PAGE_SIZE, WIDTH_A), F32)

    kidx_buf[N_PAGES, 0:DEC_PAD, :] = kin_ref[0]
    k_buf[N_PAGES, 0:DEC_PAD, :] = kn_ref[0]
    v_buf[N_PAGES, 0:DEC_PAD, :] = vn_ref[0]

    def wait_body(pg, carry):
        for cp in page_copies(pg):
            cp.wait()
        return carry

    lax.fori_loop(0, N_PAGES, wait_body, 0)

    w = wi_ref[0]
    qcat = qcat_ref[0].astype(F32)
    q_hi = jnp.concatenate([qcat[:, 256 * hh:256 * hh + 64] for hh in range(N_HEADS_IDX)], axis=0).astype(BF16)
    q_lo = jnp.concatenate([qcat[:, 256 * hh + 128:256 * hh + 192] for hh in range(N_HEADS_IDX)], axis=0).astype(BF16)
    trow = lax.broadcasted_iota(jnp.int32, (DEC_PAD, PAGE_SIZE), 0)
    tcol = lax.broadcasted_iota(jnp.int32, (DEC_PAD, PAGE_SIZE), 1)

    def score_page(pg, new_page):
        k_hi, k_lo = _split2(kidx_buf[pg])
        s = (_dot_nt(q_lo, k_lo) + _dot_nt(q_lo, k_hi) + _dot_nt(q_hi, k_lo)) + _dot_nt(q_hi, k_hi)
        acc = jnp.zeros((DEC_PAD, PAGE_SIZE), F32)
        for hh in range(N_HEADS_IDX):
            acc = acc + jnp.maximum(s[DEC_PAD * hh:DEC_PAD * (hh + 1)], 0.0) * w[:, hh:hh + 1]
        if new_page:
            acc = jnp.where((tcol <= trow) & (tcol < DEC_SEQ), acc, -jnp.inf)
        keys_ref[pg] = _sort_key(acc)

    def score_body(pg, carry):
        score_page(pg, False)
        return carry

    lax.fori_loop(0, N_PAGES, score_body, 0)
    score_page(N_PAGES, True)

    def count(pred):
        m = jnp.where(pred(keys_ref[...]), 1.0, 0.0)
        return jnp.sum(jnp.sum(m, axis=0), axis=1, keepdims=True)

    thr = _kth_largest_key(lambda cand: count(lambda kk: kk >= cand[None]), (DEC_PAD, 1))
    need = float(TOPK) - count(lambda kk: kk > thr[None])

    lane5 = lax.broadcasted_iota(jnp.int32, (DEC_PAD, WIDTH_A), 1)
    qa = qa_ref[0].astype(F32)
    qbd = jnp.concatenate(
        [jnp.where(lane5 // HEAD_DIM_A == h, qa, 0.0) for h in range(N_HEADS_A)], axis=0).astype(BF16)

    def logit_body(pg, ties_before):
        kk = keys_ref[pg]
        eq = kk == thr
        eqf = jnp.where(eq, 1.0, 0.0)
        rank = _dot(eqf.astype(BF16), tri_ref[...]) + ties_before
        sel = ((kk > thr) | (eq & (rank < need))) & (kk != KEY_NEG_INF)
        self_ = jnp.where(sel, 1.0, 0.0)
        sel_ref[pg] = self_
        s = _dot_nt(qbd, k_buf[pg].astype(BF16))
        mask = jnp.concatenate([self_] * N_HEADS_A, axis=0) > 0.5
        logit_ref[pg] = jnp.where(mask, s, NEG_BIG)
        return ties_before + jnp.sum(eqf, axis=1, keepdims=True)

    lax.fori_loop(0, N_CH_S, logit_body, jnp.zeros((DEC_PAD, 1), F32))

    m = jnp.max(jnp.max(logit_ref[...], axis=0), axis=1, keepdims=True)

    def pv_body(pg, carry):
        l, acc = carry
        mask = jnp.concatenate([sel_ref[pg]] * N_HEADS_A, axis=0) > 0.5
        p = jnp.where(mask, jnp.exp2(logit_ref[pg] - m), 0.0)
        l = l + jnp.sum(p, axis=1, keepdims=True)
        acc = acc + _dot(p.astype(BF16), v_buf[pg].astype(BF16))
        return l, acc

    l, acc = lax.fori_loop(0, N_CH_S, pv_body,
                           (jnp.zeros((ROWS_A, 1), F32), jnp.zeros((ROWS_A, WIDTH_A), F32)))
    o = acc / l
    out = jnp.zeros((DEC_PAD, WIDTH_A), F32)
    for h in range(N_HEADS_A):
        out = jnp.where(lane5 // HEAD_DIM_A == h, o[DEC_PAD * h:DEC_PAD * (h + 1)], out)
    o_ref[0] = (out * sza_ref[0].astype(F32)).astype(BF16)


def _attn_sample(p, page_table, cache_kidx, cache_k, cache_v, tri):
    nb = p["qa"].shape[0]
    row = lambda w: pl.BlockSpec((1, DEC_PAD, w), lambda b, pt: (b, 0, 0))
    any_spec = pl.BlockSpec(memory_space=pl.ANY)
    grid_spec = pltpu.PrefetchScalarGridSpec(
        num_scalar_prefetch=1,
        grid=(nb,),
        in_specs=[row(1024), row(LANES), row(WIDTH_A), row(WIDTH_A), row(HEAD_DIM_IDX), row(WIDTH_A), row(WIDTH_A),
                  pl.BlockSpec((PAGE_SIZE, PAGE_SIZE), lambda b, pt: (0, 0)),
                  any_spec, any_spec, any_spec],
        out_specs=row(WIDTH_A),
        scratch_shapes=[pltpu.VMEM((N_CH_S, PAGE_SIZE, HEAD_DIM_IDX), F32),
                        pltpu.VMEM((N_CH_S, PAGE_SIZE, WIDTH_A), F32),
                        pltpu.VMEM((N_CH_S, PAGE_SIZE, WIDTH_A), F32),
                        pltpu.VMEM((N_CH_S, DEC_PAD, PAGE_SIZE), jnp.int32),
                        pltpu.VMEM((N_CH_S, DEC_PAD, PAGE_SIZE), F32),
                        pltpu.VMEM((N_CH_S, ROWS_A, PAGE_SIZE), F32),
                        pltpu.SemaphoreType.DMA((3,))],
    )
    return pl.pallas_call(
        _attn_sample_kernel,
        grid_spec=grid_spec,
        out_shape=jax.ShapeDtypeStruct((nb, DEC_PAD, WIDTH_A), BF16),
        compiler_params=pltpu.CompilerParams(dimension_semantics=("arbitrary",),
                                             vmem_limit_bytes=52 * 1024 * 1024),
        name="attn_sample",
    )(page_table.reshape(-1), p["qcat"], p["wi"], p["qa"], p["sza"], p["ki32"], p["k32"], p["v32"], tri,
      cache_kidx, cache_k, cache_v)


def _retention_kernel(q_ref, k_ref, v_ref, s0_ref, dmat_ref, qdec_ref, kdec_ref, gc_ref, bd_ref, hm_ref,
                      ret_ref, sout_ref, state_ref):
    c = pl.program_id(1)

    @pl.when(c == 0)
    def _():
        state_ref[...] = s0_ref[0]

    q = q_ref[0]
    k = k_ref[0]
    v = v_ref[0]
    rows = q.shape[0]
    vk = (v.astype(F32) * kdec_ref[...]).astype(BF16)
    lane = lax.broadcasted_iota(jnp.int32, (rows, LANES), 1)
    for g in range(2):
        gs = slice(256 * g, 256 * (g + 1))
        qg = q[:, gs]
        kg = k[:, gs]
        sg = state_ref[g]
        cross = _dot(qg, sg.astype(BF16)) * qdec_ref[:, gs]
        for j in range(2):
            pair = 2 * g + j
            vp = v[:, LANES * pair:LANES * (pair + 1)]
            halves = []
            for e in range(2):
                hl = 2 * j + e
                qz = qg * hm_ref[hl]
                sc = _dot_nt(qz, kg) * dmat_ref[4 * g + hl]
                halves.append(_dot(sc.astype(BF16), vp))
            inner = jnp.where(lane < DK_R, halves[0], halves[1])
            ret_ref[0, :, LANES * pair:LANES * (pair + 1)] = inner + cross[:, LANES * j:LANES * (j + 1)]
        kv = _dot_tn(kg, vk[:, gs])
        state_ref[g] = sg * gc_ref[:, gs] + kv * bd_ref[...]

    @pl.when(c == pl.num_programs(1) - 1)
    def _():
        sout_ref[0] = state_ref[...]


def _retention(q, k, v, state0, tables, chunk):
    nb, rows, _ = q.shape
    nchunk = rows // chunk
    dmat, qdec, kdec, gc, bd, hm = tables
    tile = pl.BlockSpec((1, chunk, WIDTH_R), lambda b, c: (b, c, 0))
    st = pl.BlockSpec((1, 2, 256, 256), lambda b, c: (b, 0, 0, 0))
    return pl.pallas_call(
        _retention_kernel,
        grid=(nb, nchunk),
        in_specs=[tile, tile, tile, st,
                  _const_spec((N_HEADS_R, chunk, chunk)), _const_spec((chunk, WIDTH_R)), _const_spec((chunk, WIDTH_R)),
                  _const_spec((1, WIDTH_R)), _const_spec((256, 256)), _const_spec((4, 1, 256))],
        out_specs=[tile, st],
        out_shape=[jax.ShapeDtypeStruct((nb, rows, WIDTH_R), F32),
                   jax.ShapeDtypeStruct((nb, 2, 256, 256), F32)],
        scratch_shapes=[pltpu.VMEM((2, 256, 256), F32)],
        compiler_params=pltpu.CompilerParams(dimension_semantics=("parallel", "arbitrary")),
        name="retention",
    )(q, k, v, state0, dmat, qdec, kdec, gc, bd, hm)


def _retention_tables(chunk, n_real):
    lg = jnp.log1p(-jnp.exp2(-5.0 - jnp.arange(N_HEADS_R, dtype=F32)))
    i = jnp.arange(chunk, dtype=F32)
    real = jnp.arange(chunk) < n_real
    diff = i[:, None] - i[None, :]
    dmat = jnp.where(diff >= 0, jnp.exp(lg[:, None, None] * jnp.maximum(diff, 0.0)), 0.0)
    dmat = jnp.where(real[None, None, :], dmat, 0.0)
    q_decay = jnp.exp(lg[None, :] * (i[:, None] + 1.0))
    k_decay = jnp.where(real[:, None], jnp.exp(lg[None, :] * (n_real - 1.0 - i)[:, None]), 0.0)
    per_lane = lambda a: jnp.repeat(a, WIDTH_R // N_HEADS_R, axis=-1)
    gc = per_lane(jnp.exp(lg * n_real)[None, :])
    r = np.arange(256)
    bd = ((r[:, None] % 128) // 32 == (r[None, :] // 64)).astype(np.float32)
    hm = np.stack([((r % 128) // 32 == hl) for hl in range(4)]).astype(np.float32).reshape(4, 1, 256)
    return (dmat, per_lane(q_decay), per_lane(k_decay), gc, jnp.asarray(bd), jnp.asarray(hm, dtype=BF16))


def _merge_kernel(x_ref, gate_ref, ya_ref, ret_ref, szr_ref, sga_ref, sgr_ref,
                  avg_ref, gret_ref, wpa_ref, wpr_ref, wout_ref, gfin_ref, y_ref):
    ret = ret_ref[0]
    avg = avg_ref[...]
    r1, r2 = _split2(ret)
    dev = ret - (_dot(r2, avg) + _dot(r1, avg))
    e1, e2 = _split2(dev * dev)
    var = _dot(e2, avg) + _dot(e1, avg)
    yr = (dev * lax.rsqrt(var + EPS)) * gret_ref[...] * szr_ref[0].astype(F32)
    merged = (sga_ref[0].astype(F32) * _dot(ya_ref[0], wpa_ref[...])
              + sgr_ref[0].astype(F32) * _dot(yr.astype(BF16), wpr_ref[...]))
    xo = x_ref[0] + gate_ref[0] * _dot(merged.astype(BF16), wout_ref[...])
    r = lax.rsqrt(jnp.mean(xo * xo, axis=-1, keepdims=True) + EPS)
    y_ref[0] = (xo * r) * gfin_ref[...]


def _merge(x3, gate3, ya, ret, p, avg, g_ret, w_pa, w_pr, w_out, g_final, tm):
    nb, rows, _ = x3.shape
    srows = gate3.shape[1]
    row = lambda w: pl.BlockSpec((1, tm, w), lambda b, i: (b, i, 0))
    g_spec = pl.BlockSpec((1, 1 if srows == 1 else tm, D_MODEL),
                          (lambda b, i: (b, 0, 0)) if srows == 1 else (lambda b, i: (b, i, 0)))
    return pl.pallas_call(
        _merge_kernel,
        grid=(nb, rows // tm),
        in_specs=[row(D_MODEL), g_spec, row(WIDTH_A), row(WIDTH_R), row(WIDTH_R), row(D_MODEL), row(D_MODEL),
                  _const_spec((WIDTH_R, WIDTH_R)), _const_spec((1, WIDTH_R)),
                  _const_spec((WIDTH_A, D_MODEL)), _const_spec((WIDTH_R, D_MODEL)), _const_spec((D_MODEL, D_MODEL)),
                  _const_spec((1, D_MODEL))],
        out_specs=row(D_MODEL),
        out_shape=jax.ShapeDtypeStruct((nb, rows, D_MODEL), F32),
        compiler_params=pltpu.CompilerParams(dimension_semantics=("parallel", "parallel"),
                                             vmem_limit_bytes=48 * 1024 * 1024),
        name="merge",
    )(x3, gate3, ya, ret, p["szr"], p["sga"], p["sgr"], avg, g_ret.reshape(1, -1),
      w_pa, w_pr, w_out, g_final.reshape(1, -1))


def _group_cols(w):
    d = w.shape[0]
    return w.reshape(d, 2, 4, 2, 32).transpose(0, 1, 3, 2, 4).reshape(d, 512)


def _prep_weights(w_in):
    s = lambda o, n: w_in[:, o:o + n]
    main = jnp.concatenate([
        s(_O_QA, 512) * (HEAD_DIM_A ** -0.5 * LOG2E),
        s(_O_KA, 512), s(_O_VA, 512), s(_O_ZA, 512),
        _group_cols(s(_O_QR, 512)), _group_cols(s(_O_KR, 512)) * DK_R ** -0.5, s(_O_VR, 512), s(_O_ZR, 512),
        s(_O_GA, 1024), s(_O_GR, 1024)], axis=1).astype(BF16)
    qi = s(_O_QI, 256).reshape(D_MODEL, N_HEADS_IDX, 1, HEAD_DIM_IDX)
    qi = jnp.broadcast_to(qi, (D_MODEL, N_HEADS_IDX, 2, HEAD_DIM_IDX)).reshape(D_MODEL, 512)
    ki = s(_O_KI, 64)
    wi = s(_O_WI, N_HEADS_IDX) * (N_HEADS_IDX ** -0.5 * HEAD_DIM_IDX ** -0.5)
    idx = jnp.concatenate([qi, ki, ki, wi, jnp.zeros((D_MODEL, LANES - N_HEADS_IDX), F32)], axis=1)
    i1 = idx.astype(BF16)
    i2 = (idx - i1.astype(F32)).astype(BF16)
    return main, i1, i2


def _rope_tables(pos):
    half = DK_R // 2
    inv = ROPE_BASE ** (-jnp.arange(half, dtype=F32) / half)
    ang = pos.astype(F32)[:, None] * inv[None, :]
    return jnp.tile(jnp.cos(ang), (1, 4)), jnp.tile(jnp.sin(ang), (1, 4))


def _state_to_groups(state):
    nb = state.shape[0]
    st = state.reshape(nb, 2, 4, 2, 32, 64)
    eye = jnp.eye(4, dtype=state.dtype)
    out = st.transpose(0, 1, 3, 2, 4, 5)[:, :, :, :, :, None, :] * eye[None, None, None, :, None, :, None]
    return out.reshape(nb, 2, 256, 256)


def _groups_to_state(sg):
    nb = sg.shape[0]
    s7 = sg.reshape(nb, 2, 2, 4, 32, 4, 64)
    diag = jnp.stack([s7[:, :, :, hl, :, hl, :] for hl in range(4)], axis=2)
    return diag.reshape(nb, N_HEADS_R, DK_R, 64)


def kernel(x_prompt, x_sample, cache_k, cache_v, cache_kidx, state_ret, page_table, c_prompt, c_sample,
           w_ada, b_ada, g_norm, w_in, g_ret, w_pa, w_pr, w_out, g_final):
    nbp, t, _ = x_prompt.shape
    nbs = x_sample.shape[0]
    depth = w_in.shape[0]
    assert depth == 1

    r = np.arange(TQ)
    tri256 = jnp.asarray((r[:, None] < r[None, :]).astype(np.float32), dtype=BF16)
    tri128 = tri256[:PAGE_SIZE, :PAGE_SIZE]
    c5 = np.arange(WIDTH_R)
    avg = jnp.asarray((c5[:, None] // 64 == c5[None, :] // 64).astype(np.float32) / 64.0, dtype=BF16)
    cos_p, sin_p = _rope_tables(jnp.arange(t))
    cos_s, sin_s = _rope_tables(jnp.tile(PAST_LEN + jnp.arange(DEC_PAD), nbs))
    tab_p = _retention_tables(CHUNK_R, CHUNK_R)
    tab_s = _retention_tables(DEC_PAD, DEC_SEQ)

    hp = x_prompt
    hs = jnp.pad(x_sample, ((0, 0), (0, DEC_PAD - DEC_SEQ), (0, 0))).reshape(1, nbs * DEC_PAD, D_MODEL)
    outs_p, outs_s = [], []
    for l in range(depth):
        w_main, wi1, wi2 = _prep_weights(w_in[l])
        wpa, wpr, wout = w_pa[l].astype(BF16), w_pr[l].astype(BF16), w_out[l].astype(BF16)

        c_all = jnp.concatenate([c_prompt, c_sample, jnp.zeros((6, D_MODEL), F32)], axis=0)
        mod = _adaln(c_all, w_ada[l], b_ada[l])
        shift, scale, gate = mod[:, :D_MODEL], mod[:, D_MODEL:2 * D_MODEL], mod[:, 2 * D_MODEL:]
        per_p = lambda a: a[:nbp].reshape(nbp, 1, D_MODEL)
        per_s = lambda a: jnp.broadcast_to(a[nbp:nbp + nbs, None, :], (nbs, DEC_PAD, D_MODEL)).reshape(
            1, nbs * DEC_PAD, D_MODEL)

        pp = _project(hp, per_p(scale), per_p(shift), g_norm[l], cos_p, sin_p, w_main, wi1, wi2, tm=256)
        ya_p = _attn_prompt(pp, tri128.T)
        ret_p, sg_p = _retention(pp["qr"], pp["kr"], pp["vr"], jnp.zeros((nbp, 2, 256, 256), F32), tab_p, CHUNK_R)
        hp = _merge(hp, per_p(gate), ya_p, ret_p, pp, avg, g_ret[l], wpa, wpr, wout, g_final, tm=256)

        ps = _project(hs, per_s(scale), per_s(shift), g_norm[l], cos_s, sin_s, w_main, wi1, wi2, tm=nbs * DEC_PAD)
        ps3 = {k_: v_.reshape(nbs, DEC_PAD, v_.shape[-1]) for k_, v_ in ps.items() if v_.ndim == 3}
        ya_s = _attn_sample(ps3, page_table, cache_kidx[l],
                            cache_k[l].reshape(-1, PAGE_SIZE, WIDTH_A), cache_v[l].reshape(-1, PAGE_SIZE, WIDTH_A),
                            tri128)
        ret_s, sg_s = _retention(ps3["qr"], ps3["kr"], ps3["vr"], _state_to_groups(state_ret[l].astype(F32)),
                                 tab_s, DEC_PAD)
        hs = _merge(hs, per_s(gate), ya_s.reshape(1, nbs * DEC_PAD, WIDTH_A),
                    ret_s.reshape(1, nbs * DEC_PAD, WIDTH_R), ps, avg, g_ret[l], wpa, wpr, wout, g_final,
                    tm=nbs * DEC_PAD)

        heads = lambda a, n: a.reshape(n, -1, N_HEADS_A, HEAD_DIM_A)
        outs_p.append((heads(pp["k32"], nbp), heads(pp["v32"], nbp), pp["ki32"], _groups_to_state(sg_p)))
        tok = lambda a: a[:, :DEC_SEQ]
        outs_s.append((heads(tok(ps3["k32"]), nbs), heads(tok(ps3["v32"]), nbs), tok(ps3["ki32"]),
                       _groups_to_state(sg_s)))

    y_prompt = hp
    y_sample = hs.reshape(nbs, DEC_PAD, D_MODEL)[:, :DEC_SEQ]
    stack = lambda items, i: jnp.stack([it[i] for it in items])
    return (y_prompt, y_sample,
            stack(outs_p, 0), stack(outs_p, 1), stack(outs_p, 2), stack(outs_p, 3),
            stack(outs_s, 0), stack(outs_s, 1), stack(outs_s, 2), stack(outs_s, 3))
```

```python
import functools

import numpy as np
import jax
import jax.numpy as jnp
from jax import lax
from jax.experimental import pallas as pl
from jax.experimental.pallas import tpu as pltpu

D_MODEL = 1024
SEQ = 8192
DEC_SEQ = 4
PAST_LEN = 8192
PAGE_SIZE = 128
N_HEADS_A = 8
HEAD_DIM_A = 64
WIDTH_A = 512
N_HEADS_IDX = 4
HEAD_DIM_IDX = 64
TOPK = 256
N_HEADS_R = 8
DK_R = 64
WIDTH_R = 512
CHUNK_R = 128
ROPE_BASE = 10000.0
EPS = 1e-6

LANES = 128
SUBLANES = 8
DEC_PAD = SUBLANES
N_PAGES = PAST_LEN // PAGE_SIZE

_O_QA, _O_KA, _O_VA, _O_ZA = 0, 512, 1024, 1536
_O_QI, _O_KI, _O_WI = 2048, 2304, 2368
_O_QR, _O_KR, _O_VR, _O_ZR = 2372, 2884, 3396, 3908
_O_GA, _O_GR, _N_IN = 4420, 5444, 6468

N_MAIN = 6144
N_IDX = 768

INT_MIN = np.int32(-2 ** 31)
KEY_NEG_INF = np.int32(np.array(0xFF800000, np.uint32).view(np.int32) ^ np.int32(0x7FFFFFFF))
NEG_BIG = -1e30
M_INIT = -1e29
LOG2E = 1.4426950408889634
KSLAB = 128
PACK16 = 16

F32 = jnp.float32
BF16 = jnp.bfloat16


def _dot(a, b):
    return jnp.dot(a, b, preferred_element_type=F32)


def _dot_nt(a, b):
    return lax.dot_general(a, b, (((1,), (1,)), ((), ())), preferred_element_type=F32)


def _dot_tn(a, b):
    return lax.dot_general(a, b, (((0,), (0,)), ((), ())), preferred_element_type=F32)


def _split2(x):
    hi = x.astype(BF16)
    lo = (x - hi.astype(F32)).astype(BF16)
    return hi, lo


def _split3(x):
    hi = x.astype(BF16)
    r = x - hi.astype(F32)
    mid = r.astype(BF16)
    lo = (r - mid.astype(F32)).astype(BF16)
    return hi, mid, lo


def _sort_key(score):
    bits = pltpu.bitcast(score, jnp.int32)
    return bits ^ ((bits >> 31) & np.int32(0x7FFFFFFF))


def _const_spec(shape):
    nd = len(shape)
    return pl.BlockSpec(shape, lambda *_: (0,) * nd, pipeline_mode=pl.Buffered(1))


def _adaln_kernel(c_ref, w_ref, b_ref, o_ref):
    c = c_ref[...]
    a1, a2, a3 = _split3(c * jax.nn.sigmoid(c))
    w1, w2, w3 = _split3(w_ref[...])
    small = _dot(a1, w3) + _dot(a2, w2) + _dot(a3, w1)
    mid = _dot(a1, w2) + _dot(a2, w1)
    o_ref[...] = (small + mid) + _dot(a1, w1) + b_ref[...]


def _adaln(c_all, w_ada, b_ada):
    rows = c_all.shape[0]
    tn = 512
    return pl.pallas_call(
        _adaln_kernel,
        grid=(3 * D_MODEL // tn,),
        in_specs=[pl.BlockSpec((rows, D_MODEL), lambda j: (0, 0)),
                  pl.BlockSpec((D_MODEL, tn), lambda j: (0, j)),
                  pl.BlockSpec((1, tn), lambda j: (0, j))],
        out_specs=pl.BlockSpec((rows, tn), lambda j: (0, j)),
        out_shape=jax.ShapeDtypeStruct((rows, 3 * D_MODEL), F32),
        name="adaln",
    )(c_all, w_ada, b_ada.reshape(1, -1))


_PROJ_OUTS = (
    ("qa", 512, BF16), ("k32", 512, F32), ("v32", 512, F32), ("k16", 512, BF16), ("vt", None, BF16),
    ("sza", 512, BF16), ("qcat", 1024, BF16), ("kcat", 256, BF16), ("ki32", 64, F32), ("wi", 128, F32),
    ("wit", None, F32),
    ("qr", 512, BF16), ("kr", 512, BF16), ("vr", 512, BF16), ("szr", 512, BF16),
    ("sga", 1024, BF16), ("sgr", 1024, BF16),
)
_PROJ_T = {"vt": (WIDTH_A, KSLAB), "wit": (SUBLANES, None)}


def _proj_kernel(x_ref, scale_ref, shift_ref, gn_ref, cos_ref, sin_ref, wm_ref, wi1_ref, wi2_ref,
                 qa_ref, k32_ref, v32_ref, k16_ref, vt_ref, sza_ref, qcat_ref, kcat_ref, ki32_ref, wi_ref,
                 wit_ref, qr_ref, kr_ref, vr_ref, szr_ref, sga_ref, sgr_ref):
    x = x_ref[0]
    r = lax.rsqrt(jnp.mean(x * x, axis=-1, keepdims=True) + EPS)
    h = (x * r) * gn_ref[...] * (1.0 + scale_ref[0]) + shift_ref[0]
    h1, h2 = _split2(h)

    def main(g):
        return _dot(h1, wm_ref[:, g * 512:(g + 1) * 512])

    qa_ref[0] = main(0).astype(BF16)
    u = main(1)
    k32_ref[0] = u
    k16_ref[0] = u.astype(BF16)
    u = main(2)
    v32_ref[0] = u
    for sl in range(u.shape[0] // KSLAB):
        vt_ref[0, sl] = u[KSLAB * sl:KSLAB * (sl + 1)].T.astype(BF16)
    u = main(3)
    sza_ref[0] = (u * jax.nn.sigmoid(u)).astype(BF16)

    cos = cos_ref[...]
    sin = sin_ref[...]
    for g, o_ref in ((4, qr_ref), (5, kr_ref)):
        u = main(g)
        for grp in range(2):
            x1 = u[:, 256 * grp:256 * grp + 128]
            x2 = u[:, 256 * grp + 128:256 * grp + 256]
            o_ref[0, :, 256 * grp:256 * grp + 128] = (x1 * cos - x2 * sin).astype(BF16)
            o_ref[0, :, 256 * grp + 128:256 * grp + 256] = (x1 * sin + x2 * cos).astype(BF16)
    vr_ref[0] = main(6).astype(BF16)
    u = main(7)
    szr_ref[0] = (u * jax.nn.sigmoid(u)).astype(BF16)
    for j in range(2):
        sga_ref[0, :, 512 * j:512 * (j + 1)] = jax.nn.sigmoid(main(8 + j)).astype(BF16)
        sgr_ref[0, :, 512 * j:512 * (j + 1)] = jax.nn.sigmoid(main(10 + j)).astype(BF16)

    w1 = wi1_ref[...]
    w2 = wi2_ref[...]
    ui = (_dot(h2, w2) + _dot(h2, w1) + _dot(h1, w2)) + _dot(h1, w1)
    qd = ui[:, :512]
    q_hi, q_lo = _split2(qd)
    for hh in range(N_HEADS_IDX):
        qcat_ref[0, :, 256 * hh:256 * hh + 128] = q_hi[:, 128 * hh:128 * hh + 128]
        qcat_ref[0, :, 256 * hh + 128:256 * hh + 256] = q_lo[:, 128 * hh:128 * hh + 128]
    kd = ui[:, 512:640]
    k_hi, k_lo = _split2(kd)
    lane = lax.broadcasted_iota(jnp.int32, kd.shape, 1)
    sel = jnp.where(lane < HEAD_DIM_IDX, k_hi, k_lo)
    kcat_ref[0, :, 0:128] = sel
    kcat_ref[0, :, 128:256] = sel
    ki32_ref[0] = kd[:, :HEAD_DIM_IDX]
    wi_ref[0] = ui[:, 640:768]
    wit_ref[0, 0] = ui[:, 640:768].T[0:SUBLANES, :]


def _project(x3, scale3, shift3, g_norm, cos_t, sin_t, w_main, wi1, wi2, tm):
    nb, rows, _ = x3.shape
    srows = scale3.shape[1]
    stile = 1 if srows == 1 else tm
    grid = (nb, rows // tm)
    row_spec = lambda w: pl.BlockSpec((1, tm, w), lambda b, i: (b, i, 0))
    s_spec = pl.BlockSpec((1, stile, D_MODEL), (lambda b, i: (b, 0, 0)) if srows == 1 else (lambda b, i: (b, i, 0)))
    t_spec = lambda r, w: pl.BlockSpec((1, tm // (w or tm), r, w or tm), lambda b, i: (b, i, 0, 0))
    t_shape = lambda r, w: (nb, rows // (w or tm), r, w or tm)
    in_specs = [row_spec(D_MODEL), s_spec, s_spec, _const_spec((1, D_MODEL)),
                pl.BlockSpec((tm, LANES), lambda b, i: (i, 0)), pl.BlockSpec((tm, LANES), lambda b, i: (i, 0)),
                _const_spec((D_MODEL, N_MAIN)), _const_spec((D_MODEL, N_IDX)), _const_spec((D_MODEL, N_IDX))]
    outs = pl.pallas_call(
        _proj_kernel,
        grid=grid,
        in_specs=in_specs,
        out_specs=[row_spec(w) if w else t_spec(*_PROJ_T[n]) for n, w, _ in _PROJ_OUTS],
        out_shape=[jax.ShapeDtypeStruct((nb, rows, w) if w else t_shape(*_PROJ_T[n]), dt)
                   for n, w, dt in _PROJ_OUTS],
        compiler_params=pltpu.CompilerParams(dimension_semantics=("parallel", "parallel"),
                                             vmem_limit_bytes=52 * 1024 * 1024),
        name="proj",
    )(x3, scale3, shift3, g_norm.reshape(1, -1), cos_t, sin_t, w_main, wi1, wi2)
    return {name: o for (name, _, _), o in zip(_PROJ_OUTS, outs)}


TQ = 256


def _kth_largest_key(count_ge, shape, k=float(TOPK), nbits=32):
    lowest = -(1 << (nbits - 1))
    prefix = jnp.full(shape, lowest, jnp.int32)
    for b in range(nbits):
        bit = np.int32(lowest if b == 0 else 1 << (nbits - 1 - b))
        cand = prefix ^ bit
        prefix = jnp.where(count_ge(cand) >= k, cand, prefix)
    return prefix


def _attn_prompt_kernel(qcat_ref, wit_ref, qa_ref, kcat_ref, k_ref, vt_ref, tri_ref, o_ref,
                        keys_ref, hi_ref, lo_ref, qz_ref, m_ref, acc_ref):
    qi = pl.program_id(1)
    nch = qi + 1
    wt = wit_ref[0, 0]

    def rows(ref, c, n):
        return ref[0, pl.ds(pl.multiple_of(c * n, n), n), :]

    def score_chunk(c, diagonal):
        kc = rows(kcat_ref, c, TQ)
        acc = jnp.zeros((TQ, TQ), F32)
        for hh in range(N_HEADS_IDX):
            s = _dot_nt(kc, qcat_ref[0, :, 256 * hh:256 * (hh + 1)])
            acc = acc + jnp.maximum(s, 0.0) * wt[hh:hh + 1, :]
        if diagonal:
            key = lax.broadcasted_iota(jnp.int32, (TQ, TQ), 0)
            qry = lax.broadcasted_iota(jnp.int32, (TQ, TQ), 1)
            acc = jnp.where(key <= qry, acc, -jnp.inf)
        kk = _sort_key(acc)
        keys_ref[2 * c] = kk[:KSLAB]
        keys_ref[2 * c + 1] = kk[KSLAB:]
        hi_ref[c] = (kk >> 16).astype(jnp.int16)
        lo_ref[c] = (((kk ^ np.int32(0x8000)) << 16) >> 16).astype(jnp.int16)

    def score_body(c, carry):
        score_chunk(c, False)
        return carry

    lax.fori_loop(0, qi, score_body, 0)
    score_chunk(qi, True)

    def count16(ref, pred):
        def body(c, acc):
            m = jnp.where(pred(ref[c]), jnp.int16(1), jnp.int16(0))
            parts = [m[PACK16 * i:PACK16 * (i + 1)] for i in range(TQ // PACK16)]
            while len(parts) > 1:
                parts = [parts[i] + parts[i + 1] for i in range(0, len(parts), 2)]
            return acc + parts[0].astype(jnp.int32)
        acc = lax.fori_loop(0, nch, body, jnp.zeros((PACK16, TQ), jnp.int32))
        return jnp.sum(acc.astype(F32), axis=0, keepdims=True)

    as16 = lambda v: v.astype(jnp.int16)
    hi_thr = _kth_largest_key(lambda cand: count16(hi_ref, lambda x: x >= as16(cand)), (1, TQ), nbits=16)
    above = count16(hi_ref, lambda x: x > as16(hi_thr))

    def keep_tied_lows(c, carry):
        lo_ref[c] = jnp.where(hi_ref[c] == as16(hi_thr), lo_ref[c], jnp.int16(-2 ** 15))
        return carry

    lax.fori_loop(0, nch, keep_tied_lows, 0)
    lo_thr = _kth_largest_key(lambda cand: count16(lo_ref, lambda x: x >= as16(cand)), (1, TQ),
                              k=float(TOPK) - above, nbits=16)
    thr = (hi_thr << 16) | ((lo_thr + 2 ** 15) & np.int32(0xFFFF))
    need = float(TOPK) - above - count16(lo_ref, lambda x: x > as16(lo_thr))

    lane = lax.broadcasted_iota(jnp.int32, (TQ, LANES), 1)
    for h in range(N_HEADS_A):
        pair = qa_ref[0, :, LANES * (h // 2):LANES * (h // 2 + 1)]
        mine = (lane < HEAD_DIM_A) if h % 2 == 0 else (lane >= HEAD_DIM_A)
        qz_ref[h] = jnp.where(mine, pair, jnp.zeros_like(pair))
    m_ref[...] = jnp.full(m_ref.shape, M_INIT, F32)
    acc_ref[...] = jnp.zeros(acc_ref.shape, F32)
    dim16 = lax.broadcasted_iota(jnp.int32, (LANES, KSLAB), 0)

    def attn_body(sl, ties_before):
        kk = keys_ref[sl]
        eq = kk == thr
        eqf = jnp.where(eq, 1.0, 0.0)
        rank = _dot(tri_ref[...], eqf.astype(BF16)) + ties_before
        sel = ((kk > thr) | (eq & (rank < need))) & (kk != KEY_NEG_INF)
        kc = rows(k_ref, sl, KSLAB)
        vts = vt_ref[0, sl]
        for h in range(N_HEADS_A):
            lo = LANES * (h // 2)
            s = jnp.where(sel, _dot_nt(kc[:, lo:lo + LANES], qz_ref[h]), NEG_BIG)
            m_old = m_ref[h]
            m_new = jnp.maximum(m_old, jnp.max(s, axis=0, keepdims=True))
            p = jnp.exp2(s - m_new)
            mine = (dim16 < HEAD_DIM_A) if h % 2 == 0 else (dim16 >= HEAD_DIM_A)
            v1 = jnp.where(mine, vts[lo:lo + LANES, :], jnp.ones((), BF16))
            acc_ref[h] = jnp.exp2(m_old - m_new) * acc_ref[h] + _dot(v1, p.astype(BF16))
            m_ref[h] = m_new
        return ties_before + jnp.sum(eqf, axis=0, keepdims=True)

    lax.fori_loop(0, 2 * nch, attn_body, jnp.zeros((1, TQ), F32))

    dim = lax.broadcasted_iota(jnp.int32, (LANES, TQ), 0)
    for j in range(N_HEADS_A // 2):
        a0 = acc_ref[2 * j]
        a1 = acc_ref[2 * j + 1]
        even = a0 / a0[HEAD_DIM_A:HEAD_DIM_A + 1]
        odd = a1 / a1[0:1]
        o_ref[0, :, LANES * j:LANES * (j + 1)] = jnp.where(dim < HEAD_DIM_A, even, odd).T.astype(BF16)


def _attn_prompt(p, tri):
    nb, t, _ = p["qa"].shape
    nq = t // TQ
    tile = lambda w: pl.BlockSpec((1, TQ, w), lambda b, i: (b, i, 0))
    full = lambda w: pl.BlockSpec((1, t, w), lambda b, i: (b, 0, 0), pipeline_mode=pl.Buffered(1))
    return pl.pallas_call(
        _attn_prompt_kernel,
        grid=(nb, nq),
        in_specs=[tile(1024), pl.BlockSpec((1, 1, SUBLANES, TQ), lambda b, i: (b, i, 0, 0)),
                  tile(WIDTH_A), full(256), full(WIDTH_A),
                  pl.BlockSpec((1, t // KSLAB, WIDTH_A, KSLAB), lambda b, i: (b, 0, 0, 0),
                               pipeline_mode=pl.Buffered(1)),
                  _const_spec((KSLAB, KSLAB))],
        out_specs=tile(WIDTH_A),
        out_shape=jax.ShapeDtypeStruct((nb, t, WIDTH_A), BF16),
        scratch_shapes=[pltpu.VMEM((t // KSLAB, KSLAB, TQ), jnp.int32),
                        pltpu.VMEM((nq, TQ, TQ), jnp.int16),
                        pltpu.VMEM((nq, TQ, TQ), jnp.int16),
                        pltpu.VMEM((N_HEADS_A, TQ, LANES), BF16),
                        pltpu.VMEM((N_HEADS_A, 1, TQ), F32),
                        pltpu.VMEM((N_HEADS_A, LANES, TQ), F32)],
        compiler_params=pltpu.CompilerParams(dimension_semantics=("parallel", "arbitrary"),
                                             vmem_limit_bytes=52 * 1024 * 1024),
        name="attn_prompt",
    )(p["qcat"], p["wit"], p["qa"], p["kcat"], p["k16"], p["vt"], tri)


N_CH_S = N_PAGES + 1
ROWS_A = N_HEADS_A * DEC_PAD
ROWS_I = N_HEADS_IDX * DEC_PAD
FLAT = PAGE_SIZE * N_HEADS_A
NBUF = 8
N_SEL = N_CH_S + 1
SCORE_PAGES = 8
ATT_PAGES = 4


def _attn_sample_kernel(pt_ref, qih_ref, qil_ref, wi_ref, qall_ref, kin_ref, kn_ref, vn_ref,
                        tri_ref, before_ref, exp_ref, hm_ref, ckidx_ref, ck_ref, cv_ref, o_ref,
                        kidx_buf, knew_buf, k_ring, v_ring, keys_ref, selx_ref, sems):
    b = pl.program_id(0)
    nb = pl.num_programs(0)
    cur = b % 2

    def kidx_copy(seq, pg, half):
        return pltpu.make_async_copy(ckidx_ref.at[pt_ref[seq * N_PAGES + pg]], kidx_buf.at[half, pg], sems.at[half])

    def kv_copies(g):
        slot = g % NBUF
        phys = pt_ref[g]
        return (pltpu.make_async_copy(ck_ref.at[phys], k_ring.at[slot], sems.at[2 + slot]),
                pltpu.make_async_copy(cv_ref.at[phys], v_ring.at[slot], sems.at[2 + NBUF + slot]))

    def for_pages(fn):
        def body(pg, carry):
            fn(pg)
            return carry
        lax.fori_loop(0, N_PAGES, body, 0)

    @pl.when(b == 0)
    def _():
        for_pages(lambda pg: kidx_copy(0, pg, 0).start())
        for g in range(NBUF):
            for cp in kv_copies(g):
                cp.start()
        knew_buf[...] = jnp.zeros(knew_buf.shape, F32)
        keys_ref[N_CH_S] = jnp.full((DEC_PAD, PAGE_SIZE), KEY_NEG_INF, jnp.int32)

    knew_buf[0:DEC_PAD, :] = kin_ref[0]
    for_pages(lambda pg: kidx_copy(b, pg, cur).wait())

    @pl.when(b + 1 < nb)
    def _():
        for_pages(lambda pg: kidx_copy(b + 1, pg, 1 - cur).start())

    w = wi_ref[0]
    q_hi = qih_ref[0]
    q_lo = qil_ref[0]
    trow = lax.broadcasted_iota(jnp.int32, (DEC_PAD, PAGE_SIZE), 0)
    tcol = lax.broadcasted_iota(jnp.int32, (DEC_PAD, PAGE_SIZE), 1)

    def score_pages(kpages, first, new_page):
        n = kpages.shape[0] // PAGE_SIZE
        k_hi, k_lo = _split2(kpages)
        s = (_dot_nt(q_lo, k_lo) + _dot_nt(q_lo, k_hi) + _dot_nt(q_hi, k_lo)) + _dot_nt(q_hi, k_hi)
        acc = jnp.zeros((DEC_PAD, n * PAGE_SIZE), F32)
        for hh in range(N_HEADS_IDX):
            acc = acc + jnp.maximum(s[DEC_PAD * hh:DEC_PAD * (hh + 1)], 0.0) * w[:, hh:hh + 1]
        if new_page:
            acc = jnp.where((tcol <= trow) & (tcol < DEC_SEQ), acc, -jnp.inf)
        kk = _sort_key(acc)
        for j in range(n):
            keys_ref[first + j] = kk[:, PAGE_SIZE * j:PAGE_SIZE * (j + 1)]

    def score_body(i, carry):
        kpages = kidx_buf[cur, pl.ds(pl.multiple_of(i * SCORE_PAGES, SCORE_PAGES), SCORE_PAGES)]
        score_pages(kpages.reshape(SCORE_PAGES * PAGE_SIZE, HEAD_DIM_IDX), i * SCORE_PAGES, False)
        return carry

    lax.fori_loop(0, N_PAGES // SCORE_PAGES, score_body, 0)
    score_pages(knew_buf[...], N_PAGES, True)

    def count(pred):
        m = jnp.where(pred(keys_ref[...]), 1.0, 0.0)
        return jnp.sum(jnp.sum(m, axis=0), axis=1, keepdims=True)

    thr = _kth_largest_key(lambda cand: count(lambda kk: kk >= cand[None]), (DEC_PAD, 1))
    need = float(TOPK) - count(lambda kk: kk > thr[None])

    kk = keys_ref[...]
    eq = kk == thr[None]
    eq2 = jnp.where(eq, 1.0, 0.0).reshape(N_SEL * DEC_PAD, PAGE_SIZE).astype(BF16)
    in_page = _dot(eq2, tri_ref[...])
    per_page = _dot(eq2, jnp.ones((PAGE_SIZE, PAGE_SIZE), BF16))
    rank = (in_page + _dot(before_ref[...], per_page.astype(BF16))).reshape(N_SEL, DEC_PAD, PAGE_SIZE)
    sel = ((kk > thr[None]) | (eq & (rank < need[None]))) & (kk != KEY_NEG_INF)
    sel2 = jnp.where(sel, 1.0, 0.0).reshape(N_SEL * DEC_PAD, PAGE_SIZE).astype(BF16)
    selx_ref[...] = _dot(sel2, exp_ref[...]).reshape(N_SEL, DEC_PAD, FLAT)

    qall = qall_ref[0]

    def attend(carry, kflat, vflat, selx, hm):
        m_old, l_old, acc = carry
        valid = (jnp.concatenate([selx] * N_HEADS_A, axis=0) * hm) > 0.5
        s = jnp.where(valid, _dot_nt(qall, kflat.astype(BF16)), NEG_BIG)
        m_new = jnp.maximum(m_old, jnp.max(s, axis=1, keepdims=True))
        p = jnp.exp2(s - m_new)
        alpha = jnp.exp2(m_old - m_new)
        return (m_new, alpha * l_old + jnp.sum(p, axis=1, keepdims=True),
                alpha * acc + _dot(p.astype(BF16), vflat.astype(BF16)))

    hm = hm_ref[...]
    hm_group = jnp.concatenate([hm] * ATT_PAGES, axis=1)

    def group_body(gi, carry):
        pg0 = gi * ATT_PAGES
        g0 = b * N_PAGES + pg0
        for j in range(ATT_PAGES):
            for cp in kv_copies(g0 + j):
                cp.wait()
        slots = pl.ds(pl.multiple_of(pg0 % NBUF, ATT_PAGES), ATT_PAGES)
        selx = jnp.concatenate([selx_ref[pg0 + j] for j in range(ATT_PAGES)], axis=1)
        carry = attend(carry, k_ring[slots].reshape(ATT_PAGES * FLAT, HEAD_DIM_A),
                       v_ring[slots].reshape(ATT_PAGES * FLAT, HEAD_DIM_A), selx, hm_group)
        for j in range(ATT_PAGES):
            @pl.when(g0 + j + NBUF < nb * N_PAGES)
            def _():
                for cp in kv_copies(g0 + j + NBUF):
                    cp.start()
        return carry

    carry = lax.fori_loop(0, N_PAGES // ATT_PAGES, group_body,
                          (jnp.full((ROWS_A, 1), M_INIT, F32), jnp.zeros((ROWS_A, 1), F32),
                           jnp.zeros((ROWS_A, HEAD_DIM_A), F32)))
    n_new = DEC_PAD * N_HEADS_A
    _, l, acc = attend(carry, kn_ref[0], vn_ref[0], selx_ref[N_PAGES][:, :n_new], hm[:, :n_new])
    o_ref[0] = acc / l


def _attn_sample(p, page_table, cache_kidx, cache_k, cache_v, tri):
    nb = p["qa"].shape[0]
    heads_first = lambda a, nh: a.reshape(nb, DEC_PAD, nh, -1).transpose(0, 2, 1, 3).reshape(nb, nh * DEC_PAD, -1)
    qparts = p["qcat"].reshape(nb, DEC_PAD, N_HEADS_IDX, 4, HEAD_DIM_IDX)
    qih = heads_first(qparts[:, :, :, 0], N_HEADS_IDX)
    qil = heads_first(qparts[:, :, :, 2], N_HEADS_IDX)
    qall = heads_first(p["qa"], N_HEADS_A)
    flat = lambda a: a.reshape(nb, DEC_PAD * N_HEADS_A, HEAD_DIM_A)
    r = np.arange(FLAT)
    expand = jnp.asarray((np.arange(PAGE_SIZE)[:, None] == r[None, :] // N_HEADS_A).astype(np.float32), dtype=BF16)
    hm = jnp.asarray((np.arange(ROWS_A)[:, None] // DEC_PAD == r[None, :] % N_HEADS_A).astype(np.float32))
    rs = np.arange(N_SEL * DEC_PAD)
    before = jnp.asarray(((rs[:, None] % DEC_PAD == rs[None, :] % DEC_PAD)
                          & (rs[None, :] // DEC_PAD < rs[:, None] // DEC_PAD)).astype(np.float32), dtype=BF16)

    blk = lambda rows, w: pl.BlockSpec((1, rows, w), lambda b, pt: (b, 0, 0))
    const = lambda shape: pl.BlockSpec(shape, lambda b, pt: (0,) * len(shape))
    any_spec = pl.BlockSpec(memory_space=pl.ANY)
    grid_spec = pltpu.PrefetchScalarGridSpec(
        num_scalar_prefetch=1,
        grid=(nb,),
        in_specs=[blk(ROWS_I, HEAD_DIM_IDX), blk(ROWS_I, HEAD_DIM_IDX), blk(DEC_PAD, LANES), blk(ROWS_A, HEAD_DIM_A),
                  blk(DEC_PAD, HEAD_DIM_IDX), blk(ROWS_A, HEAD_DIM_A), blk(ROWS_A, HEAD_DIM_A),
                  const((PAGE_SIZE, PAGE_SIZE)), const((N_SEL * DEC_PAD, N_SEL * DEC_PAD)),
                  const((PAGE_SIZE, FLAT)), const((ROWS_A, FLAT)),
                  any_spec, any_spec, any_spec],
        out_specs=blk(ROWS_A, HEAD_DIM_A),
        scratch_shapes=[pltpu.VMEM((2, N_PAGES, PAGE_SIZE, HEAD_DIM_IDX), F32),
                        pltpu.VMEM((PAGE_SIZE, HEAD_DIM_IDX), F32),
                        pltpu.VMEM((NBUF, PAGE_SIZE, N_HEADS_A, HEAD_DIM_A), F32),
                        pltpu.VMEM((NBUF, PAGE_SIZE, N_HEADS_A, HEAD_DIM_A), F32),
                        pltpu.VMEM((N_SEL, DEC_PAD, PAGE_SIZE), jnp.int32),
                        pltpu.VMEM((N_SEL, DEC_PAD, FLAT), F32),
                        pltpu.SemaphoreType.DMA((2 + 2 * NBUF,))],
    )
    out = pl.pallas_call(
        _attn_sample_kernel,
        grid_spec=grid_spec,
        out_shape=jax.ShapeDtypeStruct((nb, ROWS_A, HEAD_DIM_A), F32),
        compiler_params=pltpu.CompilerParams(dimension_semantics=("arbitrary",),
                                             vmem_limit_bytes=40 * 1024 * 1024),
        name="attn_sample",
    )(page_table.reshape(-1), qih, qil, p["wi"], qall, p["ki32"], flat(p["k32"]), flat(p["v32"]),
      tri, before, expand, hm,
      cache_kidx, cache_k, cache_v)
    return out.reshape(nb, N_HEADS_A, DEC_PAD, HEAD_DIM_A).transpose(0, 2, 1, 3).reshape(nb, DEC_PAD, WIDTH_A)


def _retention_kernel(q_ref, k_ref, v_ref, s0_ref, dmat_ref, qdec_ref, kdec_ref, gc_ref, bd_ref, hm_ref,
                      ret_ref, sout_ref, state_ref):
    c = pl.program_id(1)

    @pl.when(c == 0)
    def _():
        state_ref[...] = s0_ref[0]

    q = q_ref[0]
    k = k_ref[0]
    v = v_ref[0]
    rows = q.shape[0]
    vk = (v.astype(F32) * kdec_ref[...]).astype(BF16)
    lane = lax.broadcasted_iota(jnp.int32, (rows, LANES), 1)
    for g in range(2):
        gs = slice(256 * g, 256 * (g + 1))
        qg = q[:, gs]
        kg = k[:, gs]
        sg = state_ref[g]
        cross = _dot(qg, sg.astype(BF16)) * qdec_ref[:, gs]
        for j in range(2):
            pair = 2 * g + j
            vp = v[:, LANES * pair:LANES * (pair + 1)]
            halves = []
            for e in range(2):
                hl = 2 * j + e
                qz = qg * hm_ref[hl]
                sc = _dot_nt(qz, kg) * dmat_ref[4 * g + hl]
                halves.append(_dot(sc.astype(BF16), vp))
            inner = jnp.where(lane < DK_R, halves[0], halves[1])
            ret_ref[0, :, LANES * pair:LANES * (pair + 1)] = inner + cross[:, LANES * j:LANES * (j + 1)]
        kv = _dot_tn(kg, vk[:, gs])
        state_ref[g] = sg * gc_ref[:, gs] + kv * bd_ref[...]

    @pl.when(c == pl.num_programs(1) - 1)
    def _():
        sout_ref[0] = state_ref[...]


def _retention(q, k, v, state0, tables, chunk):
    nb, rows, _ = q.shape
    nchunk = rows // chunk
    dmat, qdec, kdec, gc, bd, hm = tables
    tile = pl.BlockSpec((1, chunk, WIDTH_R), lambda b, c: (b, c, 0))
    st = pl.BlockSpec((1, 2, 256, 256), lambda b, c: (b, 0, 0, 0))
    return pl.pallas_call(
        _retention_kernel,
        grid=(nb, nchunk),
        in_specs=[tile, tile, tile, st,
                  _const_spec((N_HEADS_R, chunk, chunk)), _const_spec((chunk, WIDTH_R)), _const_spec((chunk, WIDTH_R)),
                  _const_spec((1, WIDTH_R)), _const_spec((256, 256)), _const_spec((4, 1, 256))],
        out_specs=[tile, st],
        out_shape=[jax.ShapeDtypeStruct((nb, rows, WIDTH_R), F32),
                   jax.ShapeDtypeStruct((nb, 2, 256, 256), F32)],
        scratch_shapes=[pltpu.VMEM((2, 256, 256), F32)],
        compiler_params=pltpu.CompilerParams(dimension_semantics=("parallel", "arbitrary")),
        name="retention",
    )(q, k, v, state0, dmat, qdec, kdec, gc, bd, hm)


def _retention_tables(chunk, n_real):
    lg = jnp.log1p(-jnp.exp2(-5.0 - jnp.arange(N_HEADS_R, dtype=F32)))
    i = jnp.arange(chunk, dtype=F32)
    real = jnp.arange(chunk) < n_real
    diff = i[:, None] - i[None, :]
    dmat = jnp.where(diff >= 0, jnp.exp(lg[:, None, None] * jnp.maximum(diff, 0.0)), 0.0)
    dmat = jnp.where(real[None, None, :], dmat, 0.0)
    q_decay = jnp.exp(lg[None, :] * (i[:, None] + 1.0))
    k_decay = jnp.where(real[:, None], jnp.exp(lg[None, :] * (n_real - 1.0 - i)[:, None]), 0.0)
    per_lane = lambda a: jnp.repeat(a, WIDTH_R // N_HEADS_R, axis=-1)
    gc = per_lane(jnp.exp(lg * n_real)[None, :])
    r = np.arange(256)
    bd = ((r[:, None] % 128) // 32 == (r[None, :] // 64)).astype(np.float32)
    hm = np.stack([((r % 128) // 32 == hl) for hl in range(4)]).astype(np.float32).reshape(4, 1, 256)
    return (dmat, per_lane(q_decay), per_lane(k_decay), gc, jnp.asarray(bd), jnp.asarray(hm, dtype=BF16))


def _merge_kernel(x_ref, gate_ref, ya_ref, sza_ref, ret_ref, szr_ref, sga_ref, sgr_ref,
                  avg_ref, gret_ref, wpa_ref, wpr_ref, wout_ref, gfin_ref, y_ref):
    ret = ret_ref[0]
    avg = avg_ref[...]
    r1, r2 = _split2(ret)
    dev = ret - (_dot(r2, avg) + _dot(r1, avg))
    e1, e2 = _split2(dev * dev)
    var = _dot(e2, avg) + _dot(e1, avg)
    yr = (dev * lax.rsqrt(var + EPS)) * gret_ref[...] * szr_ref[0].astype(F32)
    ya = (ya_ref[0].astype(F32) * sza_ref[0].astype(F32)).astype(BF16)
    merged = (sga_ref[0].astype(F32) * _dot(ya, wpa_ref[...])
              + sgr_ref[0].astype(F32) * _dot(yr.astype(BF16), wpr_ref[...]))
    xo = x_ref[0] + gate_ref[0] * _dot(merged.astype(BF16), wout_ref[...])
    r = lax.rsqrt(jnp.mean(xo * xo, axis=-1, keepdims=True) + EPS)
    y_ref[0] = (xo * r) * gfin_ref[...]


def _merge(x3, gate3, ya, ret, p, avg, g_ret, w_pa, w_pr, w_out, g_final, tm):
    nb, rows, _ = x3.shape
    srows = gate3.shape[1]
    row = lambda w: pl.BlockSpec((1, tm, w), lambda b, i: (b, i, 0))
    g_spec = pl.BlockSpec((1, 1 if srows == 1 else tm, D_MODEL),
                          (lambda b, i: (b, 0, 0)) if srows == 1 else (lambda b, i: (b, i, 0)))
    return pl.pallas_call(
        _merge_kernel,
        grid=(nb, rows // tm),
        in_specs=[row(D_MODEL), g_spec, row(WIDTH_A), row(WIDTH_A), row(WIDTH_R), row(WIDTH_R), row(D_MODEL),
                  row(D_MODEL),
                  _const_spec((WIDTH_R, WIDTH_R)), _const_spec((1, WIDTH_R)),
                  _const_spec((WIDTH_A, D_MODEL)), _const_spec((WIDTH_R, D_MODEL)), _const_spec((D_MODEL, D_MODEL)),
                  _const_spec((1, D_MODEL))],
        out_specs=row(D_MODEL),
        out_shape=jax.ShapeDtypeStruct((nb, rows, D_MODEL), F32),
        compiler_params=pltpu.CompilerParams(dimension_semantics=("parallel", "parallel"),
                                             vmem_limit_bytes=48 * 1024 * 1024),
        name="merge",
    )(x3, gate3, ya, p["sza"], ret, p["szr"], p["sga"], p["sgr"], avg, g_ret.reshape(1, -1),
      w_pa, w_pr, w_out, g_final.reshape(1, -1))


def _group_cols(w):
    d = w.shape[0]
    return w.reshape(d, 2, 4, 2, 32).transpose(0, 1, 3, 2, 4).reshape(d, 512)


def _prep_weights(w_in):
    s = lambda o, n: w_in[:, o:o + n]
    main = jnp.concatenate([
        s(_O_QA, 512) * (HEAD_DIM_A ** -0.5 * LOG2E),
        s(_O_KA, 512), s(_O_VA, 512), s(_O_ZA, 512),
        _group_cols(s(_O_QR, 512)), _group_cols(s(_O_KR, 512)) * DK_R ** -0.5, s(_O_VR, 512), s(_O_ZR, 512),
        s(_O_GA, 1024), s(_O_GR, 1024)], axis=1).astype(BF16)
    qi = s(_O_QI, 256).reshape(D_MODEL, N_HEADS_IDX, 1, HEAD_DIM_IDX)
    qi = jnp.broadcast_to(qi, (D_MODEL, N_HEADS_IDX, 2, HEAD_DIM_IDX)).reshape(D_MODEL, 512)
    ki = s(_O_KI, 64)
    wi = s(_O_WI, N_HEADS_IDX) * (N_HEADS_IDX ** -0.5 * HEAD_DIM_IDX ** -0.5)
    idx = jnp.concatenate([qi, ki, ki, wi, jnp.zeros((D_MODEL, LANES - N_HEADS_IDX), F32)], axis=1)
    i1 = idx.astype(BF16)
    i2 = (idx - i1.astype(F32)).astype(BF16)
    return main, i1, i2


def _rope_tables(pos):
    half = DK_R // 2
    inv = ROPE_BASE ** (-jnp.arange(half, dtype=F32) / half)
    ang = pos.astype(F32)[:, None] * inv[None, :]
    return jnp.tile(jnp.cos(ang), (1, 4)), jnp.tile(jnp.sin(ang), (1, 4))


def _state_to_groups(state):
    nb = state.shape[0]
    st = state.reshape(nb, 2, 4, 2, 32, 64)
    eye = jnp.eye(4, dtype=state.dtype)
    out = st.transpose(0, 1, 3, 2, 4, 5)[:, :, :, :, :, None, :] * eye[None, None, None, :, None, :, None]
    return out.reshape(nb, 2, 256, 256)


def _groups_to_state(sg):
    nb = sg.shape[0]
    s7 = sg.reshape(nb, 2, 2, 4, 32, 4, 64)
    diag = jnp.stack([s7[:, :, :, hl, :, hl, :] for hl in range(4)], axis=2)
    return diag.reshape(nb, N_HEADS_R, DK_R, 64)


def kernel(x_prompt, x_sample, cache_k, cache_v, cache_kidx, state_ret, page_table, c_prompt, c_sample,
           w_ada, b_ada, g_norm, w_in, g_ret, w_pa, w_pr, w_out, g_final):
    nbp, t, _ = x_prompt.shape
    nbs = x_sample.shape[0]
    depth = w_in.shape[0]
    assert depth == 1

    r = np.arange(TQ)
    tri256 = jnp.asarray((r[:, None] < r[None, :]).astype(np.float32), dtype=BF16)
    tri128 = tri256[:PAGE_SIZE, :PAGE_SIZE]
    c5 = np.arange(WIDTH_R)
    avg = jnp.asarray((c5[:, None] // 64 == c5[None, :] // 64).astype(np.float32) / 64.0, dtype=BF16)
    cos_p, sin_p = _rope_tables(jnp.arange(t))
    cos_s, sin_s = _rope_tables(jnp.tile(PAST_LEN + jnp.arange(DEC_PAD), nbs))
    tab_p = _retention_tables(CHUNK_R, CHUNK_R)
    tab_s = _retention_tables(DEC_PAD, DEC_SEQ)

    hp = x_prompt
    hs = jnp.pad(x_sample, ((0, 0), (0, DEC_PAD - DEC_SEQ), (0, 0))).reshape(1, nbs * DEC_PAD, D_MODEL)
    outs_p, outs_s = [], []
    for l in range(depth):
        w_main, wi1, wi2 = _prep_weights(w_in[l])
        wpa, wpr, wout = w_pa[l].astype(BF16), w_pr[l].astype(BF16), w_out[l].astype(BF16)

        c_all = jnp.concatenate([c_prompt, c_sample, jnp.zeros((6, D_MODEL), F32)], axis=0)
        mod = _adaln(c_all, w_ada[l], b_ada[l])
        shift, scale, gate = mod[:, :D_MODEL], mod[:, D_MODEL:2 * D_MODEL], mod[:, 2 * D_MODEL:]
        per_p = lambda a: a[:nbp].reshape(nbp, 1, D_MODEL)
        per_s = lambda a: jnp.broadcast_to(a[nbp:nbp + nbs, None, :], (nbs, DEC_PAD, D_MODEL)).reshape(
            1, nbs * DEC_PAD, D_MODEL)

        pp = _project(hp, per_p(scale), per_p(shift), g_norm[l], cos_p, sin_p, w_main, wi1, wi2, tm=256)
        ya_p = _attn_prompt(pp, tri128.T)
        ret_p, sg_p = _retention(pp["qr"], pp["kr"], pp["vr"], jnp.zeros((nbp, 2, 256, 256), F32), tab_p, CHUNK_R)
        hp = _merge(hp, per_p(gate), ya_p, ret_p, pp, avg, g_ret[l], wpa, wpr, wout, g_final, tm=256)

        ps = _project(hs, per_s(scale), per_s(shift), g_norm[l], cos_s, sin_s, w_main, wi1, wi2, tm=nbs * DEC_PAD)
        ps3 = {k_: v_.reshape(nbs, DEC_PAD, v_.shape[-1]) for k_, v_ in ps.items() if v_.ndim == 3}
        ya_s = _attn_sample(ps3, page_table, cache_kidx[l], cache_k[l], cache_v[l], tri128)
        ret_s, sg_s = _retention(ps3["qr"], ps3["kr"], ps3["vr"], _state_to_groups(state_ret[l].astype(F32)),
                                 tab_s, DEC_PAD)
        hs = _merge(hs, per_s(gate), ya_s.reshape(1, nbs * DEC_PAD, WIDTH_A),
                    ret_s.reshape(1, nbs * DEC_PAD, WIDTH_R), ps, avg, g_ret[l], wpa, wpr, wout, g_final,
                    tm=nbs * DEC_PAD)

        heads = lambda a, n: a.reshape(n, -1, N_HEADS_A, HEAD_DIM_A)
        outs_p.append((heads(pp["k32"], nbp), heads(pp["v32"], nbp), pp["ki32"], _groups_to_state(sg_p)))
        tok = lambda a: a[:, :DEC_SEQ]
        outs_s.append((heads(tok(ps3["k32"]), nbs), heads(tok(ps3["v32"]), nbs), tok(ps3["ki32"]),
                       _groups_to_state(sg_s)))

    y_prompt = hp
    y_sample = hs.reshape(nbs, DEC_PAD, D_MODEL)[:, :DEC_SEQ]
    stack = lambda items, i: jnp.stack([it[i] for it in items])
    return (y_prompt, y_sample,
            stack(outs_p, 0), stack(outs_p, 1), stack(outs_p, 2), stack(outs_p, 3),
            stack(outs_s, 0), stack(outs_s, 1), stack(outs_s, 2), stack(outs_s, 3))
```

```python
import functools

import numpy as np
import jax
import jax.numpy as jnp
from jax import lax
from jax.experimental import pallas as pl
from jax.experimental.pallas import tpu as pltpu

D_MODEL = 1024
SEQ = 8192
DEC_SEQ = 4
PAST_LEN = 8192
PAGE_SIZE = 128
N_HEADS_A = 8
HEAD_DIM_A = 64
WIDTH_A = 512
N_HEADS_IDX = 4
HEAD_DIM_IDX = 64
TOPK = 256
N_HEADS_R = 8
DK_R = 64
WIDTH_R = 512
CHUNK_R = 128
ROPE_BASE = 10000.0
EPS = 1e-6

LANES = 128
SUBLANES = 8
DEC_PAD = SUBLANES
N_PAGES = PAST_LEN // PAGE_SIZE

_O_QA, _O_KA, _O_VA, _O_ZA = 0, 512, 1024, 1536
_O_QI, _O_KI, _O_WI = 2048, 2304, 2368
_O_QR, _O_KR, _O_VR, _O_ZR = 2372, 2884, 3396, 3908
_O_GA, _O_GR, _N_IN = 4420, 5444, 6468

N_MAIN = 6144
N_IDX = 768

INT_MIN = np.int32(-2 ** 31)
KEY_NEG_INF = np.int32(np.array(0xFF800000, np.uint32).view(np.int32) ^ np.int32(0x7FFFFFFF))
NEG_BIG = -1e30
M_INIT = -1e29
LOG2E = 1.4426950408889634
KSLAB = 128
PACK16 = 16

F32 = jnp.float32
BF16 = jnp.bfloat16


def _dot(a, b):
    return jnp.dot(a, b, preferred_element_type=F32)


def _dot_nt(a, b):
    return lax.dot_general(a, b, (((1,), (1,)), ((), ())), preferred_element_type=F32)


def _dot_tn(a, b):
    return lax.dot_general(a, b, (((0,), (0,)), ((), ())), preferred_element_type=F32)


def _split2(x):
    hi = x.astype(BF16)
    lo = (x - hi.astype(F32)).astype(BF16)
    return hi, lo


def _split3(x):
    hi = x.astype(BF16)
    r = x - hi.astype(F32)
    mid = r.astype(BF16)
    lo = (r - mid.astype(F32)).astype(BF16)
    return hi, mid, lo


def _sort_key(score):
    bits = pltpu.bitcast(score, jnp.int32)
    return bits ^ ((bits >> 31) & np.int32(0x7FFFFFFF))


def _const_spec(shape):
    nd = len(shape)
    return pl.BlockSpec(shape, lambda *_: (0,) * nd, pipeline_mode=pl.Buffered(1))


def _adaln_kernel(c_ref, w_ref, b_ref, o_ref):
    c = c_ref[...]
    a1, a2, a3 = _split3(c * jax.nn.sigmoid(c))
    w1, w2, w3 = _split3(w_ref[...])
    small = _dot(a1, w3) + _dot(a2, w2) + _dot(a3, w1)
    mid = _dot(a1, w2) + _dot(a2, w1)
    o_ref[...] = (small + mid) + _dot(a1, w1) + b_ref[...]


def _adaln(c_all, w_ada, b_ada):
    rows = c_all.shape[0]
    tn = 512
    return pl.pallas_call(
        _adaln_kernel,
        grid=(3 * D_MODEL // tn,),
        in_specs=[pl.BlockSpec((rows, D_MODEL), lambda j: (0, 0)),
                  pl.BlockSpec((D_MODEL, tn), lambda j: (0, j)),
                  pl.BlockSpec((1, tn), lambda j: (0, j))],
        out_specs=pl.BlockSpec((rows, tn), lambda j: (0, j)),
        out_shape=jax.ShapeDtypeStruct((rows, 3 * D_MODEL), F32),
        name="adaln",
    )(c_all, w_ada, b_ada.reshape(1, -1))


_PROJ_OUTS = (
    ("qa", "row", 512, BF16), ("k32", "row", 512, F32), ("v32", "row", 512, F32), ("kt32", "col", 512, F32),
    ("vt32", "col", 512, F32), ("k16", "row", 512, BF16), ("vt", "slab", 512, BF16),
    ("sza", "row", 512, BF16), ("qcat", "row", 1024, BF16), ("kcat", "row", 256, BF16),
    ("ki32", "row", 64, F32), ("kit32", "col", 64, F32), ("wi", "row", 128, F32), ("wit", "slab", SUBLANES, F32),
    ("qr", "row", 512, BF16), ("kr", "row", 512, BF16), ("vr", "row", 512, BF16), ("szr", "row", 512, BF16),
    ("sga", "row", 1024, BF16), ("sgr", "row", 1024, BF16),
)
_PROJ_SLAB = {"vt": KSLAB, "wit": None}


def _proj_kernel(x_ref, scale_ref, shift_ref, gn_ref, cos_ref, sin_ref, wm_ref, wi1_ref, wi2_ref,
                 qa_ref, k32_ref, v32_ref, kt32_ref, vt32_ref, k16_ref, vt_ref, sza_ref, qcat_ref, kcat_ref,
                 ki32_ref, kit32_ref, wi_ref, wit_ref, qr_ref, kr_ref, vr_ref, szr_ref, sga_ref, sgr_ref):
    x = x_ref[0]
    r = lax.rsqrt(jnp.mean(x * x, axis=-1, keepdims=True) + EPS)
    h = (x * r) * gn_ref[...] * (1.0 + scale_ref[0]) + shift_ref[0]
    h1, h2 = _split2(h)

    def main(g):
        return _dot_nt(h1, wm_ref[g * 512:(g + 1) * 512, :])

    qa_ref[0] = main(0).astype(BF16)
    u = main(1)
    k32_ref[0] = u
    kt32_ref[0] = u.T
    k16_ref[0] = u.astype(BF16)
    u = main(2)
    v32_ref[0] = u
    ut = u.T
    vt32_ref[0] = ut
    for sl in range(u.shape[0] // KSLAB):
        vt_ref[0, sl] = ut[:, KSLAB * sl:KSLAB * (sl + 1)].astype(BF16)
    u = main(3)
    sza_ref[0] = (u * jax.nn.sigmoid(u)).astype(BF16)

    cos = cos_ref[...]
    sin = sin_ref[...]
    for g, o_ref in ((4, qr_ref), (5, kr_ref)):
        u = main(g)
        for grp in range(2):
            x1 = u[:, 256 * grp:256 * grp + 128]
            x2 = u[:, 256 * grp + 128:256 * grp + 256]
            o_ref[0, :, 256 * grp:256 * grp + 128] = (x1 * cos - x2 * sin).astype(BF16)
            o_ref[0, :, 256 * grp + 128:256 * grp + 256] = (x1 * sin + x2 * cos).astype(BF16)
    vr_ref[0] = main(6).astype(BF16)
    u = main(7)
    szr_ref[0] = (u * jax.nn.sigmoid(u)).astype(BF16)
    for j in range(2):
        sga_ref[0, :, 512 * j:512 * (j + 1)] = jax.nn.sigmoid(main(8 + j)).astype(BF16)
        sgr_ref[0, :, 512 * j:512 * (j + 1)] = jax.nn.sigmoid(main(10 + j)).astype(BF16)

    w1 = wi1_ref[...]
    w2 = wi2_ref[...]
    ui = (_dot_nt(h2, w2) + _dot_nt(h2, w1) + _dot_nt(h1, w2)) + _dot_nt(h1, w1)
    qd = ui[:, :512]
    q_hi, q_lo = _split2(qd)
    for hh in range(N_HEADS_IDX):
        qcat_ref[0, :, 256 * hh:256 * hh + 128] = q_hi[:, 128 * hh:128 * hh + 128]
        qcat_ref[0, :, 256 * hh + 128:256 * hh + 256] = q_lo[:, 128 * hh:128 * hh + 128]
    kd = ui[:, 512:640]
    k_hi, k_lo = _split2(kd)
    lane = lax.broadcasted_iota(jnp.int32, kd.shape, 1)
    sel = jnp.where(lane < HEAD_DIM_IDX, k_hi, k_lo)
    kcat_ref[0, :, 0:128] = sel
    kcat_ref[0, :, 128:256] = sel
    ki32_ref[0] = kd[:, :HEAD_DIM_IDX]
    kit32_ref[0] = kd.T[0:HEAD_DIM_IDX, :]
    wi_ref[0] = ui[:, 640:768]
    wit_ref[0, 0] = ui[:, 640:768].T[0:SUBLANES, :]


def _project(x3, scale3, shift3, g_norm, cos_t, sin_t, w_main, wi1, wi2, tm):
    nb, rows, _ = x3.shape
    srows = scale3.shape[1]
    stile = 1 if srows == 1 else tm
    grid = (nb, rows // tm)
    row_spec = lambda w: pl.BlockSpec((1, tm, w), lambda b, i: (b, i, 0))
    s_spec = pl.BlockSpec((1, stile, D_MODEL), (lambda b, i: (b, 0, 0)) if srows == 1 else (lambda b, i: (b, i, 0)))

    def out(name, layout, dim):
        if layout == "row":
            return (nb, rows, dim), pl.BlockSpec((1, tm, dim), lambda b, i: (b, i, 0))
        if layout == "col":
            return (nb, dim, rows), pl.BlockSpec((1, dim, tm), lambda b, i: (b, 0, i))
        w = _PROJ_SLAB[name] or tm
        return (nb, rows // w, dim, w), pl.BlockSpec((1, tm // w, dim, w), lambda b, i: (b, i, 0, 0))

    outs_meta = [out(n, lay, dim) for n, lay, dim, _ in _PROJ_OUTS]
    in_specs = [row_spec(D_MODEL), s_spec, s_spec, _const_spec((1, D_MODEL)),
                pl.BlockSpec((tm, LANES), lambda b, i: (i, 0)), pl.BlockSpec((tm, LANES), lambda b, i: (i, 0)),
                _const_spec((N_MAIN, D_MODEL)), _const_spec((N_IDX, D_MODEL)), _const_spec((N_IDX, D_MODEL))]
    outs = pl.pallas_call(
        _proj_kernel,
        grid=grid,
        in_specs=in_specs,
        out_specs=[spec for _, spec in outs_meta],
        out_shape=[jax.ShapeDtypeStruct(shape, o[3]) for (shape, _), o in zip(outs_meta, _PROJ_OUTS)],
        compiler_params=pltpu.CompilerParams(dimension_semantics=("parallel", "parallel"),
                                             vmem_limit_bytes=52 * 1024 * 1024),
        name="proj",
    )(x3, scale3, shift3, g_norm.reshape(1, -1), cos_t, sin_t, w_main, wi1, wi2)
    return {o[0]: a for o, a in zip(_PROJ_OUTS, outs)}


TQ = 256


def _kth_largest_key(count_ge, shape, k=float(TOPK), nbits=32):
    lowest = -(1 << (nbits - 1))
    prefix = jnp.full(shape, lowest, jnp.int32)
    for b in range(nbits):
        bit = np.int32(lowest if b == 0 else 1 << (nbits - 1 - b))
        cand = prefix ^ bit
        prefix = jnp.where(count_ge(cand) >= k, cand, prefix)
    return prefix


def _attn_prompt_kernel(qcat_ref, wit_ref, qa_ref, kcat_ref, k_ref, vt_ref, tri_ref, o_ref,
                        keys_ref, hi_ref, lo_ref, qz_ref, m_ref, acc_ref):
    qi = pl.program_id(1)
    nch = qi + 1
    wt = wit_ref[0, 0]

    def rows(ref, c, n):
        return ref[0, pl.ds(pl.multiple_of(c * n, n), n), :]

    def score_chunk(c, diagonal):
        kc = rows(kcat_ref, c, TQ)
        acc = jnp.zeros((TQ, TQ), F32)
        for hh in range(N_HEADS_IDX):
            s = _dot_nt(kc, qcat_ref[0, :, 256 * hh:256 * (hh + 1)])
            acc = acc + jnp.maximum(s, 0.0) * wt[hh:hh + 1, :]
        if diagonal:
            key = lax.broadcasted_iota(jnp.int32, (TQ, TQ), 0)
            qry = lax.broadcasted_iota(jnp.int32, (TQ, TQ), 1)
            acc = jnp.where(key <= qry, acc, -jnp.inf)
        kk = _sort_key(acc)
        keys_ref[2 * c] = kk[:KSLAB]
        keys_ref[2 * c + 1] = kk[KSLAB:]
        hi_ref[c] = (kk >> 16).astype(jnp.int16)
        lo_ref[c] = (((kk ^ np.int32(0x8000)) << 16) >> 16).astype(jnp.int16)

    def score_body(c, carry):
        score_chunk(c, False)
        return carry

    lax.fori_loop(0, qi, score_body, 0)
    score_chunk(qi, True)

    def count16(ref, pred):
        def body(c, acc):
            m = jnp.where(pred(ref[c]), jnp.int16(1), jnp.int16(0))
            parts = [m[PACK16 * i:PACK16 * (i + 1)] for i in range(TQ // PACK16)]
            while len(parts) > 1:
                parts = [parts[i] + parts[i + 1] for i in range(0, len(parts), 2)]
            return acc + parts[0].astype(jnp.int32)
        acc = lax.fori_loop(0, nch, body, jnp.zeros((PACK16, TQ), jnp.int32))
        return jnp.sum(acc.astype(F32), axis=0, keepdims=True)

    as16 = lambda v: v.astype(jnp.int16)
    hi_thr = _kth_largest_key(lambda cand: count16(hi_ref, lambda x: x >= as16(cand)), (1, TQ), nbits=16)
    above = count16(hi_ref, lambda x: x > as16(hi_thr))

    def keep_tied_lows(c, carry):
        lo_ref[c] = jnp.where(hi_ref[c] == as16(hi_thr), lo_ref[c], jnp.int16(-2 ** 15))
        return carry

    lax.fori_loop(0, nch, keep_tied_lows, 0)
    lo_thr = _kth_largest_key(lambda cand: count16(lo_ref, lambda x: x >= as16(cand)), (1, TQ),
                              k=float(TOPK) - above, nbits=16)
    thr = (hi_thr << 16) | ((lo_thr + 2 ** 15) & np.int32(0xFFFF))
    need = float(TOPK) - above - count16(lo_ref, lambda x: x > as16(lo_thr))

    lane = lax.broadcasted_iota(jnp.int32, (TQ, LANES), 1)
    for h in range(N_HEADS_A):
        pair = qa_ref[0, :, LANES * (h // 2):LANES * (h // 2 + 1)]
        mine = (lane < HEAD_DIM_A) if h % 2 == 0 else (lane >= HEAD_DIM_A)
        qz_ref[h] = jnp.where(mine, pair, jnp.zeros_like(pair))
    m_ref[...] = jnp.full(m_ref.shape, M_INIT, F32)
    acc_ref[...] = jnp.zeros(acc_ref.shape, F32)
    dim16 = lax.broadcasted_iota(jnp.int32, (LANES, KSLAB), 0)

    def attn_body(sl, ties_before):
        kk = keys_ref[sl]
        eq = kk == thr
        eqf = jnp.where(eq, 1.0, 0.0)
        rank = _dot(tri_ref[...], eqf.astype(BF16)) + ties_before
        sel = ((kk > thr) | (eq & (rank < need))) & (kk != KEY_NEG_INF)
        kc = rows(k_ref, sl, KSLAB)
        vts = vt_ref[0, sl]
        for h in range(N_HEADS_A):
            lo = LANES * (h // 2)
            s = jnp.where(sel, _dot_nt(kc[:, lo:lo + LANES], qz_ref[h]), NEG_BIG)
            m_old = m_ref[h]
            m_new = jnp.maximum(m_old, jnp.max(s, axis=0, keepdims=True))
            p = jnp.exp2(s - m_new)
            mine = (dim16 < HEAD_DIM_A) if h % 2 == 0 else (dim16 >= HEAD_DIM_A)
            v1 = jnp.where(mine, vts[lo:lo + LANES, :], jnp.ones((), BF16))
            acc_ref[h] = jnp.exp2(m_old - m_new) * acc_ref[h] + _dot(v1, p.astype(BF16))
            m_ref[h] = m_new
        return ties_before + jnp.sum(eqf, axis=0, keepdims=True)

    lax.fori_loop(0, 2 * nch, attn_body, jnp.zeros((1, TQ), F32))

    dim = lax.broadcasted_iota(jnp.int32, (LANES, TQ), 0)
    for j in range(N_HEADS_A // 2):
        a0 = acc_ref[2 * j]
        a1 = acc_ref[2 * j + 1]
        even = a0 / a0[HEAD_DIM_A:HEAD_DIM_A + 1]
        odd = a1 / a1[0:1]
        o_ref[0, :, LANES * j:LANES * (j + 1)] = jnp.where(dim < HEAD_DIM_A, even, odd).T.astype(BF16)


def _attn_prompt(p, tri):
    nb, t, _ = p["qa"].shape
    nq = t // TQ
    tile = lambda w: pl.BlockSpec((1, TQ, w), lambda b, i: (b, i, 0))
    full = lambda w: pl.BlockSpec((1, t, w), lambda b, i: (b, 0, 0), pipeline_mode=pl.Buffered(1))
    return pl.pallas_call(
        _attn_prompt_kernel,
        grid=(nb, nq),
        in_specs=[tile(1024), pl.BlockSpec((1, 1, SUBLANES, TQ), lambda b, i: (b, i, 0, 0)),
                  tile(WIDTH_A), full(256), full(WIDTH_A),
                  pl.BlockSpec((1, t // KSLAB, WIDTH_A, KSLAB), lambda b, i: (b, 0, 0, 0),
                               pipeline_mode=pl.Buffered(1)),
                  _const_spec((KSLAB, KSLAB))],
        out_specs=tile(WIDTH_A),
        out_shape=jax.ShapeDtypeStruct((nb, t, WIDTH_A), BF16),
        scratch_shapes=[pltpu.VMEM((t // KSLAB, KSLAB, TQ), jnp.int32),
                        pltpu.VMEM((nq, TQ, TQ), jnp.int16),
                        pltpu.VMEM((nq, TQ, TQ), jnp.int16),
                        pltpu.VMEM((N_HEADS_A, TQ, LANES), BF16),
                        pltpu.VMEM((N_HEADS_A, 1, TQ), F32),
                        pltpu.VMEM((N_HEADS_A, LANES, TQ), F32)],
        compiler_params=pltpu.CompilerParams(dimension_semantics=("parallel", "arbitrary"),
                                             vmem_limit_bytes=52 * 1024 * 1024),
        name="attn_prompt",
    )(p["qcat"], p["wit"], p["qa"], p["kcat"], p["k16"], p["vt"], tri)


N_CH_S = N_PAGES + 1
ROWS_A = N_HEADS_A * DEC_PAD
ROWS_I = N_HEADS_IDX * DEC_PAD
NBUF = 16
N_SEL = N_CH_S + 1
SCORE_PAGES = 8
ATT_PAGES = 4


def _attn_sample_kernel(pt_ref, qih_ref, qil_ref, wi_ref, qa_ref, kin_ref, kn_ref, vn_ref,
                        tri_ref, before_ref, ckidx_ref, ck_ref, cv_ref, o_ref,
                        kidx_buf, knew_buf, kvnew_buf, k_ring, v_ring, keys_ref, selc_ref, sems):
    b = pl.program_id(0)
    nb = pl.num_programs(0)
    cur = b % 2

    def kidx_copy(seq, pg, half):
        return pltpu.make_async_copy(ckidx_ref.at[pt_ref[seq * N_PAGES + pg]], kidx_buf.at[half, pg], sems.at[half])

    def kv_copies(g):
        slot = g % NBUF
        phys = pt_ref[g]
        return (pltpu.make_async_copy(ck_ref.at[phys], k_ring.at[slot], sems.at[2 + slot]),
                pltpu.make_async_copy(cv_ref.at[phys], v_ring.at[slot], sems.at[2 + NBUF + slot]))

    def for_pages(fn):
        def body(pg, carry):
            fn(pg)
            return carry
        lax.fori_loop(0, N_PAGES, body, 0)

    @pl.when(b == 0)
    def _():
        for_pages(lambda pg: kidx_copy(0, pg, 0).start())
        for g in range(NBUF):
            for cp in kv_copies(g):
                cp.start()
        knew_buf[...] = jnp.zeros(knew_buf.shape, F32)
        kvnew_buf[...] = jnp.zeros(kvnew_buf.shape, F32)
        keys_ref[N_CH_S] = jnp.full((DEC_PAD, PAGE_SIZE), KEY_NEG_INF, jnp.int32)

    knew_buf[0:DEC_PAD, :] = kin_ref[0]
    kvnew_buf[0, 0:DEC_PAD, :] = kn_ref[0]
    kvnew_buf[1, 0:DEC_PAD, :] = vn_ref[0]
    for_pages(lambda pg: kidx_copy(b, pg, cur).wait())

    @pl.when(b + 1 < nb)
    def _():
        for_pages(lambda pg: kidx_copy(b + 1, pg, 1 - cur).start())

    w = wi_ref[0]
    q_hi = qih_ref[0]
    q_lo = qil_ref[0]
    trow = lax.broadcasted_iota(jnp.int32, (DEC_PAD, PAGE_SIZE), 0)
    tcol = lax.broadcasted_iota(jnp.int32, (DEC_PAD, PAGE_SIZE), 1)

    def score_pages(kt, first, new_page):
        n = kt.shape[1] // PAGE_SIZE
        k_hi, k_lo = _split2(kt)
        s = (_dot(q_lo, k_lo) + _dot(q_lo, k_hi) + _dot(q_hi, k_lo)) + _dot(q_hi, k_hi)
        acc = jnp.zeros((DEC_PAD, n * PAGE_SIZE), F32)
        for hh in range(N_HEADS_IDX):
            acc = acc + jnp.maximum(s[DEC_PAD * hh:DEC_PAD * (hh + 1)], 0.0) * w[:, hh:hh + 1]
        if new_page:
            acc = jnp.where((tcol <= trow) & (tcol < DEC_SEQ), acc, -jnp.inf)
        kk = _sort_key(acc)
        for j in range(n):
            keys_ref[first + j] = kk[:, PAGE_SIZE * j:PAGE_SIZE * (j + 1)]

    def score_body(i, carry):
        first = i * SCORE_PAGES
        score_pages(jnp.concatenate([kidx_buf[cur, first + j] for j in range(SCORE_PAGES)], axis=1), first, False)
        return carry

    lax.fori_loop(0, N_PAGES // SCORE_PAGES, score_body, 0)
    score_pages(knew_buf[...].T, N_PAGES, True)

    def count(pred):
        m = jnp.where(pred(keys_ref[...]), 1.0, 0.0)
        return jnp.sum(jnp.sum(m, axis=0), axis=1, keepdims=True)

    thr = _kth_largest_key(lambda cand: count(lambda kk: kk >= cand[None]), (DEC_PAD, 1))
    need = float(TOPK) - count(lambda kk: kk > thr[None])

    kk = keys_ref[...]
    eq = kk == thr[None]
    eq2 = jnp.where(eq, 1.0, 0.0).reshape(N_SEL * DEC_PAD, PAGE_SIZE).astype(BF16)
    in_page = _dot(eq2, tri_ref[...])
    per_page = _dot(eq2, jnp.ones((PAGE_SIZE, PAGE_SIZE), BF16))
    rank = (in_page + _dot(before_ref[...], per_page.astype(BF16))).reshape(N_SEL, DEC_PAD, PAGE_SIZE)
    sel = ((kk > thr[None]) | (eq & (rank < need[None]))) & (kk != KEY_NEG_INF)
    selc_ref[...] = jnp.where(sel, 1.0, 0.0)

    lane5 = lax.broadcasted_iota(jnp.int32, (DEC_PAD, WIDTH_A), 1)
    qa = qa_ref[0].astype(F32)
    qbd = jnp.concatenate(
        [jnp.where(lane5 // HEAD_DIM_A == h, qa, 0.0) for h in range(N_HEADS_A)], axis=0).astype(BF16)

    def attend(carry, kt, vt, selc):
        m_old, l_old, acc = carry
        valid = jnp.concatenate([selc] * N_HEADS_A, axis=0) > 0.5
        s = jnp.where(valid, _dot(qbd, kt.astype(BF16)), NEG_BIG)
        m_new = jnp.maximum(m_old, jnp.max(s, axis=1, keepdims=True))
        p = jnp.exp2(s - m_new)
        alpha = jnp.exp2(m_old - m_new)
        return (m_new, alpha * l_old + jnp.sum(p, axis=1, keepdims=True),
                alpha * acc + _dot_nt(p.astype(BF16), vt.astype(BF16)))

    def group_body(gi, carry):
        pg0 = gi * ATT_PAGES
        g0 = b * N_PAGES + pg0
        for j in range(ATT_PAGES):
            for cp in kv_copies(g0 + j):
                cp.wait()
        pages = lambda ring: jnp.concatenate(
            [ring[(pg0 + j) % NBUF].reshape(WIDTH_A, PAGE_SIZE) for j in range(ATT_PAGES)], axis=1)
        selc = jnp.concatenate([selc_ref[pg0 + j] for j in range(ATT_PAGES)], axis=1)
        carry = attend(carry, pages(k_ring), pages(v_ring), selc)
        for j in range(ATT_PAGES):
            @pl.when(g0 + j + NBUF < nb * N_PAGES)
            def _():
                for cp in kv_copies(g0 + j + NBUF):
                    cp.start()
        return carry

    carry = lax.fori_loop(0, N_PAGES // ATT_PAGES, group_body,
                          (jnp.full((ROWS_A, 1), M_INIT, F32), jnp.zeros((ROWS_A, 1), F32),
                           jnp.zeros((ROWS_A, WIDTH_A), F32)))
    _, l, acc = attend(carry, kvnew_buf[0].T, kvnew_buf[1].T, selc_ref[N_PAGES])
    o = acc / l
    out = jnp.zeros((DEC_PAD, WIDTH_A), F32)
    for h in range(N_HEADS_A):
        out = jnp.where(lane5 // HEAD_DIM_A == h, o[DEC_PAD * h:DEC_PAD * (h + 1)], out)
    o_ref[0] = out


def _attn_sample(p, page_table, cache_kidx_t, cache_k_t, cache_v_t, tri):
    nb = p["qa"].shape[0]
    heads_first = lambda a, nh: a.reshape(nb, DEC_PAD, nh, -1).transpose(0, 2, 1, 3).reshape(nb, nh * DEC_PAD, -1)
    qparts = p["qcat"].reshape(nb, DEC_PAD, N_HEADS_IDX, 4, HEAD_DIM_IDX)
    qih = heads_first(qparts[:, :, :, 0], N_HEADS_IDX)
    qil = heads_first(qparts[:, :, :, 2], N_HEADS_IDX)
    rs = np.arange(N_SEL * DEC_PAD)
    before = jnp.asarray(((rs[:, None] % DEC_PAD == rs[None, :] % DEC_PAD)
                          & (rs[None, :] // DEC_PAD < rs[:, None] // DEC_PAD)).astype(np.float32), dtype=BF16)

    blk = lambda rows, w: pl.BlockSpec((1, rows, w), lambda b, pt: (b, 0, 0))
    const = lambda shape: pl.BlockSpec(shape, lambda b, pt: (0,) * len(shape))
    any_spec = pl.BlockSpec(memory_space=pl.ANY)
    grid_spec = pltpu.PrefetchScalarGridSpec(
        num_scalar_prefetch=1,
        grid=(nb,),
        in_specs=[blk(ROWS_I, HEAD_DIM_IDX), blk(ROWS_I, HEAD_DIM_IDX), blk(DEC_PAD, LANES), blk(DEC_PAD, WIDTH_A),
                  blk(DEC_PAD, HEAD_DIM_IDX), blk(DEC_PAD, WIDTH_A), blk(DEC_PAD, WIDTH_A),
                  const((PAGE_SIZE, PAGE_SIZE)), const((N_SEL * DEC_PAD, N_SEL * DEC_PAD)),
                  any_spec, any_spec, any_spec],
        out_specs=blk(DEC_PAD, WIDTH_A),
        scratch_shapes=[pltpu.VMEM((2, N_PAGES, HEAD_DIM_IDX, PAGE_SIZE), F32),
                        pltpu.VMEM((PAGE_SIZE, HEAD_DIM_IDX), F32),
                        pltpu.VMEM((2, PAGE_SIZE, WIDTH_A), F32),
                        pltpu.VMEM((NBUF, N_HEADS_A, HEAD_DIM_A, PAGE_SIZE), F32),
                        pltpu.VMEM((NBUF, N_HEADS_A, HEAD_DIM_A, PAGE_SIZE), F32),
                        pltpu.VMEM((N_SEL, DEC_PAD, PAGE_SIZE), jnp.int32),
                        pltpu.VMEM((N_SEL, DEC_PAD, PAGE_SIZE), F32),
                        pltpu.SemaphoreType.DMA((2 + 2 * NBUF,))],
    )
    return pl.pallas_call(
        _attn_sample_kernel,
        grid_spec=grid_spec,
        out_shape=jax.ShapeDtypeStruct((nb, DEC_PAD, WIDTH_A), F32),
        compiler_params=pltpu.CompilerParams(dimension_semantics=("arbitrary",),
                                             vmem_limit_bytes=40 * 1024 * 1024),
        name="attn_sample",
    )(page_table.reshape(-1), qih, qil, p["wi"], p["qa"], p["ki32"], p["k32"], p["v32"], tri, before,
      cache_kidx_t, cache_k_t, cache_v_t)


def _retention_kernel(q_ref, k_ref, v_ref, s0_ref, dmat_ref, qdec_ref, kdec_ref, gc_ref, bd_ref, hm_ref,
                      ret_ref, sout_ref, state_ref):
    c = pl.program_id(1)

    @pl.when(c == 0)
    def _():
        state_ref[...] = s0_ref[0]

    q = q_ref[0]
    k = k_ref[0]
    v = v_ref[0]
    rows = q.shape[0]
    vk = (v.astype(F32) * kdec_ref[...]).astype(BF16)
    lane = lax.broadcasted_iota(jnp.int32, (rows, LANES), 1)
    for g in range(2):
        gs = slice(256 * g, 256 * (g + 1))
        qg = q[:, gs]
        kg = k[:, gs]
        sg = state_ref[g]
        cross = _dot(qg, sg.astype(BF16)) * qdec_ref[:, gs]
        for j in range(2):
            pair = 2 * g + j
            vp = v[:, LANES * pair:LANES * (pair + 1)]
            halves = []
            for e in range(2):
                hl = 2 * j + e
                qz = qg * hm_ref[hl]
                sc = _dot_nt(qz, kg) * dmat_ref[4 * g + hl]
                halves.append(_dot(sc.astype(BF16), vp))
            inner = jnp.where(lane < DK_R, halves[0], halves[1])
            ret_ref[0, :, LANES * pair:LANES * (pair + 1)] = inner + cross[:, LANES * j:LANES * (j + 1)]
        kv = _dot_tn(kg, vk[:, gs])
        state_ref[g] = sg * gc_ref[:, gs] + kv * bd_ref[...]

    @pl.when(c == pl.num_programs(1) - 1)
    def _():
        sout_ref[0] = state_ref[...]


def _retention(q, k, v, state0, tables, chunk):
    nb, rows, _ = q.shape
    nchunk = rows // chunk
    dmat, qdec, kdec, gc, bd, hm = tables
    tile = pl.BlockSpec((1, chunk, WIDTH_R), lambda b, c: (b, c, 0))
    st = pl.BlockSpec((1, 2, 256, 256), lambda b, c: (b, 0, 0, 0))
    return pl.pallas_call(
        _retention_kernel,
        grid=(nb, nchunk),
        in_specs=[tile, tile, tile, st,
                  _const_spec((N_HEADS_R, chunk, chunk)), _const_spec((chunk, WIDTH_R)), _const_spec((chunk, WIDTH_R)),
                  _const_spec((1, WIDTH_R)), _const_spec((256, 256)), _const_spec((4, 1, 256))],
        out_specs=[tile, st],
        out_shape=[jax.ShapeDtypeStruct((nb, rows, WIDTH_R), F32),
                   jax.ShapeDtypeStruct((nb, 2, 256, 256), F32)],
        scratch_shapes=[pltpu.VMEM((2, 256, 256), F32)],
        compiler_params=pltpu.CompilerParams(dimension_semantics=("parallel", "arbitrary")),
        name="retention",
    )(q, k, v, state0, dmat, qdec, kdec, gc, bd, hm)


def _retention_tables(chunk, n_real):
    lg = jnp.log1p(-jnp.exp2(-5.0 - jnp.arange(N_HEADS_R, dtype=F32)))
    i = jnp.arange(chunk, dtype=F32)
    real = jnp.arange(chunk) < n_real
    diff = i[:, None] - i[None, :]
    dmat = jnp.where(diff >= 0, jnp.exp(lg[:, None, None] * jnp.maximum(diff, 0.0)), 0.0)
    dmat = jnp.where(real[None, None, :], dmat, 0.0)
    q_decay = jnp.exp(lg[None, :] * (i[:, None] + 1.0))
    k_decay = jnp.where(real[:, None], jnp.exp(lg[None, :] * (n_real - 1.0 - i)[:, None]), 0.0)
    per_lane = lambda a: jnp.repeat(a, WIDTH_R // N_HEADS_R, axis=-1)
    gc = per_lane(jnp.exp(lg * n_real)[None, :])
    r = np.arange(256)
    bd = ((r[:, None] % 128) // 32 == (r[None, :] // 64)).astype(np.float32)
    hm = np.stack([((r % 128) // 32 == hl) for hl in range(4)]).astype(np.float32).reshape(4, 1, 256)
    return (dmat, per_lane(q_decay), per_lane(k_decay), gc, jnp.asarray(bd), jnp.asarray(hm, dtype=BF16))


def _merge_kernel(x_ref, gate_ref, ya_ref, sza_ref, ret_ref, szr_ref, sga_ref, sgr_ref,
                  avg_ref, gret_ref, wpa_ref, wpr_ref, wout_ref, gfin_ref, y_ref):
    ret = ret_ref[0]
    avg = avg_ref[...]
    r1, r2 = _split2(ret)
    dev = ret - (_dot(r2, avg) + _dot(r1, avg))
    e1, e2 = _split2(dev * dev)
    var = _dot(e2, avg) + _dot(e1, avg)
    yr = (dev * lax.rsqrt(var + EPS)) * gret_ref[...] * szr_ref[0].astype(F32)
    ya = (ya_ref[0].astype(F32) * sza_ref[0].astype(F32)).astype(BF16)
    merged = (sga_ref[0].astype(F32) * _dot(ya, wpa_ref[...])
              + sgr_ref[0].astype(F32) * _dot(yr.astype(BF16), wpr_ref[...]))
    xo = x_ref[0] + gate_ref[0] * _dot(merged.astype(BF16), wout_ref[...])
    r = lax.rsqrt(jnp.mean(xo * xo, axis=-1, keepdims=True) + EPS)
    y_ref[0] = (xo * r) * gfin_ref[...]


def _merge(x3, gate3, ya, ret, p, avg, g_ret, w_pa, w_pr, w_out, g_final, tm):
    nb, rows, _ = x3.shape
    srows = gate3.shape[1]
    row = lambda w: pl.BlockSpec((1, tm, w), lambda b, i: (b, i, 0))
    g_spec = pl.BlockSpec((1, 1 if srows == 1 else tm, D_MODEL),
                          (lambda b, i: (b, 0, 0)) if srows == 1 else (lambda b, i: (b, i, 0)))
    return pl.pallas_call(
        _merge_kernel,
        grid=(nb, rows // tm),
        in_specs=[row(D_MODEL), g_spec, row(WIDTH_A), row(WIDTH_A), row(WIDTH_R), row(WIDTH_R), row(D_MODEL),
                  row(D_MODEL),
                  _const_spec((WIDTH_R, WIDTH_R)), _const_spec((1, WIDTH_R)),
                  _const_spec((WIDTH_A, D_MODEL)), _const_spec((WIDTH_R, D_MODEL)), _const_spec((D_MODEL, D_MODEL)),
                  _const_spec((1, D_MODEL))],
        out_specs=row(D_MODEL),
        out_shape=jax.ShapeDtypeStruct((nb, rows, D_MODEL), F32),
        compiler_params=pltpu.CompilerParams(dimension_semantics=("parallel", "parallel"),
                                             vmem_limit_bytes=48 * 1024 * 1024),
        name="merge",
    )(x3, gate3, ya, p["sza"], ret, p["szr"], p["sga"], p["sgr"], avg, g_ret.reshape(1, -1),
      w_pa, w_pr, w_out, g_final.reshape(1, -1))


def _group_rows(w):
    d = w.shape[1]
    return w.reshape(2, 4, 2, 32, d).transpose(0, 2, 1, 3, 4).reshape(512, d)


def _prep_weights(w_in):
    wt = w_in.T
    s = lambda o, n: wt[o:o + n]
    main = jnp.concatenate([
        s(_O_QA, 512) * (HEAD_DIM_A ** -0.5 * LOG2E),
        s(_O_KA, 512), s(_O_VA, 512), s(_O_ZA, 512),
        _group_rows(s(_O_QR, 512)), _group_rows(s(_O_KR, 512)) * DK_R ** -0.5, s(_O_VR, 512), s(_O_ZR, 512),
        s(_O_GA, 1024), s(_O_GR, 1024)], axis=0).astype(BF16)
    qi = jnp.broadcast_to(s(_O_QI, 256).reshape(N_HEADS_IDX, 1, HEAD_DIM_IDX, D_MODEL),
                          (N_HEADS_IDX, 2, HEAD_DIM_IDX, D_MODEL)).reshape(512, D_MODEL)
    ki = s(_O_KI, 64)
    wi = s(_O_WI, N_HEADS_IDX) * (N_HEADS_IDX ** -0.5 * HEAD_DIM_IDX ** -0.5)
    idx = jnp.concatenate([qi, ki, ki, wi, jnp.zeros((LANES - N_HEADS_IDX, D_MODEL), F32)], axis=0)
    i1 = idx.astype(BF16)
    i2 = (idx - i1.astype(F32)).astype(BF16)
    return main, i1, i2


def _rope_tables(pos):
    half = DK_R // 2
    inv = ROPE_BASE ** (-jnp.arange(half, dtype=F32) / half)
    ang = pos.astype(F32)[:, None] * inv[None, :]
    return jnp.tile(jnp.cos(ang), (1, 4)), jnp.tile(jnp.sin(ang), (1, 4))


def _state_to_groups(state):
    nb = state.shape[0]
    st = state.reshape(nb, 2, 4, 2, 32, 64)
    eye = jnp.eye(4, dtype=state.dtype)
    out = st.transpose(0, 1, 3, 2, 4, 5)[:, :, :, :, :, None, :] * eye[None, None, None, :, None, :, None]
    return out.reshape(nb, 2, 256, 256)


def _groups_to_state(sg):
    nb = sg.shape[0]
    s7 = sg.reshape(nb, 2, 2, 4, 32, 4, 64)
    diag = jnp.stack([s7[:, :, :, hl, :, hl, :] for hl in range(4)], axis=2)
    return diag.reshape(nb, N_HEADS_R, DK_R, 64)


def kernel(x_prompt, x_sample, cache_k, cache_v, cache_kidx, state_ret, page_table, c_prompt, c_sample,
           w_ada, b_ada, g_norm, w_in, g_ret, w_pa, w_pr, w_out, g_final):
    nbp, t, _ = x_prompt.shape
    nbs = x_sample.shape[0]
    depth = w_in.shape[0]
    assert depth == 1

    r = np.arange(TQ)
    tri256 = jnp.asarray((r[:, None] < r[None, :]).astype(np.float32), dtype=BF16)
    tri128 = tri256[:PAGE_SIZE, :PAGE_SIZE]
    c5 = np.arange(WIDTH_R)
    avg = jnp.asarray((c5[:, None] // 64 == c5[None, :] // 64).astype(np.float32) / 64.0, dtype=BF16)
    cos_p, sin_p = _rope_tables(jnp.arange(t))
    cos_s, sin_s = _rope_tables(jnp.tile(PAST_LEN + jnp.arange(DEC_PAD), nbs))
    tab_p = _retention_tables(CHUNK_R, CHUNK_R)
    tab_s = _retention_tables(DEC_PAD, DEC_SEQ)

    hp = x_prompt
    hs = jnp.pad(x_sample, ((0, 0), (0, DEC_PAD - DEC_SEQ), (0, 0))).reshape(1, nbs * DEC_PAD, D_MODEL)
    outs_p, outs_s = [], []
    for l in range(depth):
        w_main, wi1, wi2 = _prep_weights(w_in[l])
        wpa, wpr, wout = w_pa[l].astype(BF16), w_pr[l].astype(BF16), w_out[l].astype(BF16)

        c_all = jnp.concatenate([c_prompt, c_sample, jnp.zeros((6, D_MODEL), F32)], axis=0)
        mod = _adaln(c_all, w_ada[l], b_ada[l])
        shift, scale, gate = mod[:, :D_MODEL], mod[:, D_MODEL:2 * D_MODEL], mod[:, 2 * D_MODEL:]
        per_p = lambda a: a[:nbp].reshape(nbp, 1, D_MODEL)
        per_s = lambda a: jnp.broadcast_to(a[nbp:nbp + nbs, None, :], (nbs, DEC_PAD, D_MODEL)).reshape(
            1, nbs * DEC_PAD, D_MODEL)

        pp = _project(hp, per_p(scale), per_p(shift), g_norm[l], cos_p, sin_p, w_main, wi1, wi2, tm=256)
        ya_p = _attn_prompt(pp, tri128.T)
        ret_p, sg_p = _retention(pp["qr"], pp["kr"], pp["vr"], jnp.zeros((nbp, 2, 256, 256), F32), tab_p, CHUNK_R)
        hp = _merge(hp, per_p(gate), ya_p, ret_p, pp, avg, g_ret[l], wpa, wpr, wout, g_final, tm=256)

        ps = _project(hs, per_s(scale), per_s(shift), g_norm[l], cos_s, sin_s, w_main, wi1, wi2, tm=nbs * DEC_PAD)
        ps3 = {o[0]: ps[o[0]].reshape(nbs, DEC_PAD, o[2]) for o in _PROJ_OUTS if o[1] == "row"}
        ya_s = _attn_sample(ps3, page_table, cache_kidx[l].transpose(0, 2, 1),
                            cache_k[l].transpose(0, 2, 3, 1), cache_v[l].transpose(0, 2, 3, 1), tri128)
        ret_s, sg_s = _retention(ps3["qr"], ps3["kr"], ps3["vr"], _state_to_groups(state_ret[l].astype(F32)),
                                 tab_s, DEC_PAD)
        hs = _merge(hs, per_s(gate), ya_s.reshape(1, nbs * DEC_PAD, WIDTH_A),
                    ret_s.reshape(1, nbs * DEC_PAD, WIDTH_R), ps, avg, g_ret[l], wpa, wpr, wout, g_final,
                    tm=nbs * DEC_PAD)

        heads = lambda a, n: a.reshape(n, -1, N_HEADS_A, HEAD_DIM_A)
        tokens_first = lambda a: a.reshape(nbp, N_HEADS_A, HEAD_DIM_A, t).transpose(0, 3, 1, 2)
        outs_p.append((tokens_first(pp["kt32"]), tokens_first(pp["vt32"]), pp["kit32"].transpose(0, 2, 1),
                       _groups_to_state(sg_p)))
        tok = lambda a: a[:, :DEC_SEQ]
        outs_s.append((heads(tok(ps3["k32"]), nbs), heads(tok(ps3["v32"]), nbs), tok(ps3["ki32"]),
                       _groups_to_state(sg_s)))

    y_prompt = hp
    y_sample = hs.reshape(nbs, DEC_PAD, D_MODEL)[:, :DEC_SEQ]
    stack = lambda items, i: jnp.stack([it[i] for it in items])
    return (y_prompt, y_sample,
            stack(outs_p, 0), stack(outs_p, 1), stack(outs_p, 2), stack(outs_p, 3),
            stack(outs_s, 0), stack(outs_s, 1), stack(outs_s, 2), stack(outs_s, 3))
```

```python
import functools

import numpy as np
import jax
import jax.numpy as jnp
from jax import lax
from jax.experimental import pallas as pl
from jax.experimental.pallas import tpu as pltpu

D_MODEL = 1024
SEQ = 8192
DEC_SEQ = 4
PAST_LEN = 8192
PAGE_SIZE = 128
N_HEADS_A = 8
HEAD_DIM_A = 64
WIDTH_A = 512
N_HEADS_IDX = 4
HEAD_DIM_IDX = 64
TOPK = 256
N_HEADS_R = 8
DK_R = 64
WIDTH_R = 512
CHUNK_R = 128
ROPE_BASE = 10000.0
EPS = 1e-6

LANES = 128
SUBLANES = 8
DEC_PAD = SUBLANES
N_PAGES = PAST_LEN // PAGE_SIZE

_O_QA, _O_KA, _O_VA, _O_ZA = 0, 512, 1024, 1536
_O_QI, _O_KI, _O_WI = 2048, 2304, 2368
_O_QR, _O_KR, _O_VR, _O_ZR = 2372, 2884, 3396, 3908
_O_GA, _O_GR, _N_IN = 4420, 5444, 6468

N_MAIN = 6144
N_IDX = 768

INT_MIN = np.int32(-2 ** 31)
KEY_NEG_INF = np.int32(np.array(0xFF800000, np.uint32).view(np.int32) ^ np.int32(0x7FFFFFFF))
NEG_BIG = -1e30
M_INIT = -1e29
LOG2E = 1.4426950408889634
KSLAB = 128
PACK16 = 16
I16_MIN = np.int16(-2 ** 15)

F32 = jnp.float32
BF16 = jnp.bfloat16


def _dot(a, b):
    return jnp.dot(a, b, preferred_element_type=F32)


def _dot_nt(a, b):
    return lax.dot_general(a, b, (((1,), (1,)), ((), ())), preferred_element_type=F32)


def _dot_tn(a, b):
    return lax.dot_general(a, b, (((0,), (0,)), ((), ())), preferred_element_type=F32)


def _split2(x):
    hi = x.astype(BF16)
    lo = (x - hi.astype(F32)).astype(BF16)
    return hi, lo


def _split3(x):
    hi = x.astype(BF16)
    r = x - hi.astype(F32)
    mid = r.astype(BF16)
    lo = (r - mid.astype(F32)).astype(BF16)
    return hi, mid, lo


def _sort_key(score):
    bits = pltpu.bitcast(score, jnp.int32)
    return bits ^ ((bits >> 31) & np.int32(0x7FFFFFFF))


def _const_spec(shape):
    nd = len(shape)
    return pl.BlockSpec(shape, lambda *_: (0,) * nd, pipeline_mode=pl.Buffered(1))


def _adaln_kernel(c_ref, w_ref, b_ref, o_ref):
    c = c_ref[...]
    a1, a2, a3 = _split3(c * jax.nn.sigmoid(c))
    w1, w2, w3 = _split3(w_ref[...])
    small = _dot(a1, w3) + _dot(a2, w2) + _dot(a3, w1)
    mid = _dot(a1, w2) + _dot(a2, w1)
    o_ref[...] = (small + mid) + _dot(a1, w1) + b_ref[...]


def _adaln(c_all, w_ada, b_ada):
    rows = c_all.shape[0]
    tn = 512
    return pl.pallas_call(
        _adaln_kernel,
        grid=(3 * D_MODEL // tn,),
        in_specs=[pl.BlockSpec((rows, D_MODEL), lambda j: (0, 0)),
                  pl.BlockSpec((D_MODEL, tn), lambda j: (0, j)),
                  pl.BlockSpec((1, tn), lambda j: (0, j))],
        out_specs=pl.BlockSpec((rows, tn), lambda j: (0, j)),
        out_shape=jax.ShapeDtypeStruct((rows, 3 * D_MODEL), F32),
        name="adaln",
    )(c_all, w_ada, b_ada.reshape(1, -1))


_PROJ_OUTS = (
    ("qa", "row", 512, BF16), ("k32", "row", 512, F32), ("v32", "row", 512, F32), ("kt32", "col", 512, F32),
    ("vt32", "col", 512, F32), ("k16", "row", 512, BF16), ("vt", "slab", 512, BF16),
    ("sza", "row", 512, BF16), ("qcat", "row", 1024, BF16), ("kcat", "row", 256, BF16),
    ("ki32", "row", 64, F32), ("kit32", "col", 64, F32), ("wi", "row", 128, F32), ("wit", "slab", SUBLANES, F32),
    ("qr", "row", 512, BF16), ("kr", "row", 512, BF16), ("vr", "row", 512, BF16), ("szr", "row", 512, BF16),
    ("sga", "row", 1024, BF16), ("sgr", "row", 1024, BF16),
)
_PROJ_SLAB = {"vt": KSLAB, "wit": None}


def _proj_kernel(x_ref, scale_ref, shift_ref, gn_ref, cos_ref, sin_ref, wm_ref, wi1_ref, wi2_ref,
                 qa_ref, k32_ref, v32_ref, kt32_ref, vt32_ref, k16_ref, vt_ref, sza_ref, qcat_ref, kcat_ref,
                 ki32_ref, kit32_ref, wi_ref, wit_ref, qr_ref, kr_ref, vr_ref, szr_ref, sga_ref, sgr_ref):
    x = x_ref[0]
    r = lax.rsqrt(jnp.mean(x * x, axis=-1, keepdims=True) + EPS)
    h = (x * r) * gn_ref[...] * (1.0 + scale_ref[0]) + shift_ref[0]
    h1, h2 = _split2(h)

    def main(g):
        return _dot_nt(h1, wm_ref[g * 512:(g + 1) * 512, :])

    qa_ref[0] = main(0).astype(BF16)
    u = main(1)
    k32_ref[0] = u
    kt32_ref[0] = u.T
    k16_ref[0] = u.astype(BF16)
    u = main(2)
    v32_ref[0] = u
    ut = u.T
    vt32_ref[0] = ut
    for sl in range(u.shape[0] // KSLAB):
        vt_ref[0, sl] = ut[:, KSLAB * sl:KSLAB * (sl + 1)].astype(BF16)
    u = main(3)
    sza_ref[0] = (u * jax.nn.sigmoid(u)).astype(BF16)

    cos = cos_ref[...]
    sin = sin_ref[...]
    for g, o_ref in ((4, qr_ref), (5, kr_ref)):
        u = main(g)
        for grp in range(2):
            x1 = u[:, 256 * grp:256 * grp + 128]
            x2 = u[:, 256 * grp + 128:256 * grp + 256]
            o_ref[0, :, 256 * grp:256 * grp + 128] = (x1 * cos - x2 * sin).astype(BF16)
            o_ref[0, :, 256 * grp + 128:256 * grp + 256] = (x1 * sin + x2 * cos).astype(BF16)
    vr_ref[0] = main(6).astype(BF16)
    u = main(7)
    szr_ref[0] = (u * jax.nn.sigmoid(u)).astype(BF16)
    for j in range(2):
        sga_ref[0, :, 512 * j:512 * (j + 1)] = jax.nn.sigmoid(main(8 + j)).astype(BF16)
        sgr_ref[0, :, 512 * j:512 * (j + 1)] = jax.nn.sigmoid(main(10 + j)).astype(BF16)

    w1 = wi1_ref[...]
    w2 = wi2_ref[...]
    ui = (_dot_nt(h2, w2) + _dot_nt(h2, w1) + _dot_nt(h1, w2)) + _dot_nt(h1, w1)
    qd = ui[:, :512]
    q_hi, q_lo = _split2(qd)
    for hh in range(N_HEADS_IDX):
        qcat_ref[0, :, 256 * hh:256 * hh + 128] = q_hi[:, 128 * hh:128 * hh + 128]
        qcat_ref[0, :, 256 * hh + 128:256 * hh + 256] = q_lo[:, 128 * hh:128 * hh + 128]
    kd = ui[:, 512:640]
    k_hi, k_lo = _split2(kd)
    lane = lax.broadcasted_iota(jnp.int32, kd.shape, 1)
    sel = jnp.where(lane < HEAD_DIM_IDX, k_hi, k_lo)
    kcat_ref[0, :, 0:128] = sel
    kcat_ref[0, :, 128:256] = sel
    ki32_ref[0] = kd[:, :HEAD_DIM_IDX]
    kit32_ref[0] = kd.T[0:HEAD_DIM_IDX, :]
    wi_ref[0] = ui[:, 640:768]
    wit_ref[0, 0] = ui[:, 640:768].T[0:SUBLANES, :]


def _project(x3, scale3, shift3, g_norm, cos_t, sin_t, w_main, wi1, wi2, tm):
    nb, rows, _ = x3.shape
    srows = scale3.shape[1]
    stile = 1 if srows == 1 else tm
    grid = (nb, rows // tm)
    row_spec = lambda w: pl.BlockSpec((1, tm, w), lambda b, i: (b, i, 0))
    s_spec = pl.BlockSpec((1, stile, D_MODEL), (lambda b, i: (b, 0, 0)) if srows == 1 else (lambda b, i: (b, i, 0)))

    def out(name, layout, dim):
        if layout == "row":
            return (nb, rows, dim), pl.BlockSpec((1, tm, dim), lambda b, i: (b, i, 0))
        if layout == "col":
            return (nb, dim, rows), pl.BlockSpec((1, dim, tm), lambda b, i: (b, 0, i))
        w = _PROJ_SLAB[name] or tm
        return (nb, rows // w, dim, w), pl.BlockSpec((1, tm // w, dim, w), lambda b, i: (b, i, 0, 0))

    outs_meta = [out(n, lay, dim) for n, lay, dim, _ in _PROJ_OUTS]
    in_specs = [row_spec(D_MODEL), s_spec, s_spec, _const_spec((1, D_MODEL)),
                pl.BlockSpec((tm, LANES), lambda b, i: (i, 0)), pl.BlockSpec((tm, LANES), lambda b, i: (i, 0)),
                _const_spec((N_MAIN, D_MODEL)), _const_spec((N_IDX, D_MODEL)), _const_spec((N_IDX, D_MODEL))]
    outs = pl.pallas_call(
        _proj_kernel,
        grid=grid,
        in_specs=in_specs,
        out_specs=[spec for _, spec in outs_meta],
        out_shape=[jax.ShapeDtypeStruct(shape, o[3]) for (shape, _), o in zip(outs_meta, _PROJ_OUTS)],
        compiler_params=pltpu.CompilerParams(dimension_semantics=("parallel", "parallel"),
                                             vmem_limit_bytes=52 * 1024 * 1024),
        name="proj",
    )(x3, scale3, shift3, g_norm.reshape(1, -1), cos_t, sin_t, w_main, wi1, wi2)
    return {o[0]: a for o, a in zip(_PROJ_OUTS, outs)}


TQ = 256


def _kth_largest_key(count_ge, shape, k=float(TOPK), nbits=32):
    lowest = -(1 << (nbits - 1))
    prefix = jnp.full(shape, lowest, jnp.int32)
    for b in range(nbits):
        bit = np.int32(lowest if b == 0 else 1 << (nbits - 1 - b))
        cand = prefix ^ bit
        prefix = jnp.where(count_ge(cand) >= k, cand, prefix)
    return prefix


def _attn_prompt_kernel(qcat_ref, wit_ref, qa_ref, kcat_ref, k_ref, vt_ref, tri_ref, o_ref,
                        keys_ref, hi_ref, lo_ref, qz_ref, m_ref, acc_ref):
    qi = pl.program_id(1)
    nch = qi + 1
    wt = wit_ref[0, 0]

    def rows(ref, c, n):
        return ref[0, pl.ds(pl.multiple_of(c * n, n), n), :]

    def score_chunk(c, diagonal, pair, half):
        kc = rows(kcat_ref, c, TQ)
        acc = jnp.zeros((TQ, TQ), F32)
        for hh in range(N_HEADS_IDX):
            s = _dot_nt(kc, qcat_ref[0, :, 256 * hh:256 * (hh + 1)])
            acc = acc + jnp.maximum(s, 0.0) * wt[hh:hh + 1, :]
        if diagonal:
            key = lax.broadcasted_iota(jnp.int32, (TQ, TQ), 0)
            qry = lax.broadcasted_iota(jnp.int32, (TQ, TQ), 1)
            acc = jnp.where(key <= qry, acc, -jnp.inf)
        kk = _sort_key(acc)
        keys_ref[2 * c] = kk[:KSLAB]
        keys_ref[2 * c + 1] = kk[KSLAB:]
        hi_ref[pair, TQ * half:TQ * (half + 1), :] = (kk >> 16).astype(jnp.int16)
        lo_ref[pair, TQ * half:TQ * (half + 1), :] = (((kk ^ np.int32(0x8000)) << 16) >> 16).astype(jnp.int16)

    def score_pair(pair, carry):
        score_chunk(2 * pair, False, pair, 0)
        score_chunk(2 * pair + 1, False, pair, 1)
        return carry

    lax.fori_loop(0, qi // 2, score_pair, 0)
    last = qi // 2
    npairs = last + 1

    @pl.when(qi % 2 == 0)
    def _():
        score_chunk(qi, True, last, 0)
        never = jnp.full((TQ, TQ), I16_MIN, jnp.int16)
        hi_ref[last, TQ:, :] = never
        lo_ref[last, TQ:, :] = never

    @pl.when(qi % 2 == 1)
    def _():
        score_chunk(qi - 1, False, last, 0)
        score_chunk(qi, True, last, 1)

    def count16(ref, pred):
        def body(pair, acc):
            m = jnp.where(pred(ref[pair]), jnp.int16(1), jnp.int16(0))
            parts = [m[PACK16 * i:PACK16 * (i + 1)] for i in range(2 * TQ // PACK16)]
            while len(parts) > 1:
                parts = [parts[i] + parts[i + 1] for i in range(0, len(parts), 2)]
            return acc + parts[0].astype(jnp.int32)
        acc = lax.fori_loop(0, npairs, body, jnp.zeros((PACK16, TQ), jnp.int32))
        return jnp.sum(acc.astype(F32), axis=0, keepdims=True)

    as16 = lambda v: v.astype(jnp.int16)
    hi_thr = _kth_largest_key(lambda cand: count16(hi_ref, lambda x: x >= as16(cand)), (1, TQ), nbits=16)
    above = count16(hi_ref, lambda x: x > as16(hi_thr))

    def keep_tied_lows(pair, carry):
        lo_ref[pair] = jnp.where(hi_ref[pair] == as16(hi_thr), lo_ref[pair], I16_MIN)
        return carry

    lax.fori_loop(0, npairs, keep_tied_lows, 0)
    lo_thr = _kth_largest_key(lambda cand: count16(lo_ref, lambda x: x >= as16(cand)), (1, TQ),
                              k=float(TOPK) - above, nbits=16)
    thr = (hi_thr << 16) | ((lo_thr + 2 ** 15) & np.int32(0xFFFF))
    need = float(TOPK) - above - count16(lo_ref, lambda x: x > as16(lo_thr))

    lane = lax.broadcasted_iota(jnp.int32, (TQ, LANES), 1)
    for h in range(N_HEADS_A):
        pair = qa_ref[0, :, LANES * (h // 2):LANES * (h // 2 + 1)]
        mine = (lane < HEAD_DIM_A) if h % 2 == 0 else (lane >= HEAD_DIM_A)
        qz_ref[h] = jnp.where(mine, pair, jnp.zeros_like(pair))
    m_ref[...] = jnp.full(m_ref.shape, M_INIT, F32)
    acc_ref[...] = jnp.zeros(acc_ref.shape, F32)
    dim16 = lax.broadcasted_iota(jnp.int32, (LANES, KSLAB), 0)

    def attn_slab(sl, ties_before):
        kk = keys_ref[sl]
        eq = kk == thr
        eqf = jnp.where(eq, 1.0, 0.0)
        rank = _dot(tri_ref[...], eqf.astype(BF16)) + ties_before
        sel = ((kk > thr) | (eq & (rank < need))) & (kk != KEY_NEG_INF)
        kc = rows(k_ref, sl, KSLAB)
        vts = vt_ref[0, sl]
        for h in range(N_HEADS_A):
            lo = LANES * (h // 2)
            s = jnp.where(sel, _dot_nt(kc[:, lo:lo + LANES], qz_ref[h]), NEG_BIG)
            m_old = m_ref[h]
            m_new = jnp.maximum(m_old, jnp.max(s, axis=0, keepdims=True))
            p = jnp.exp2(s - m_new)
            mine = (dim16 < HEAD_DIM_A) if h % 2 == 0 else (dim16 >= HEAD_DIM_A)
            v1 = jnp.where(mine, vts[lo:lo + LANES, :], jnp.ones((), BF16))
            acc_ref[h] = jnp.exp2(m_old - m_new) * acc_ref[h] + _dot(v1, p.astype(BF16))
            m_ref[h] = m_new
        return ties_before + jnp.sum(eqf, axis=0, keepdims=True)

    def attn_body(c, ties_before):
        return attn_slab(2 * c + 1, attn_slab(2 * c, ties_before))

    lax.fori_loop(0, nch, attn_body, jnp.zeros((1, TQ), F32))

    dim = lax.broadcasted_iota(jnp.int32, (LANES, TQ), 0)
    for j in range(N_HEADS_A // 2):
        a0 = acc_ref[2 * j]
        a1 = acc_ref[2 * j + 1]
        even = a0 / a0[HEAD_DIM_A:HEAD_DIM_A + 1]
        odd = a1 / a1[0:1]
        o_ref[0, :, LANES * j:LANES * (j + 1)] = jnp.where(dim < HEAD_DIM_A, even, odd).T.astype(BF16)


def _attn_prompt(p, tri):
    nb, t, _ = p["qa"].shape
    nq = t // TQ
    tile = lambda w: pl.BlockSpec((1, TQ, w), lambda b, i: (b, i, 0))
    full = lambda w: pl.BlockSpec((1, t, w), lambda b, i: (b, 0, 0), pipeline_mode=pl.Buffered(1))
    return pl.pallas_call(
        _attn_prompt_kernel,
        grid=(nb, nq),
        in_specs=[tile(1024), pl.BlockSpec((1, 1, SUBLANES, TQ), lambda b, i: (b, i, 0, 0)),
                  tile(WIDTH_A), full(256), full(WIDTH_A),
                  pl.BlockSpec((1, t // KSLAB, WIDTH_A, KSLAB), lambda b, i: (b, 0, 0, 0),
                               pipeline_mode=pl.Buffered(1)),
                  _const_spec((KSLAB, KSLAB))],
        out_specs=tile(WIDTH_A),
        out_shape=jax.ShapeDtypeStruct((nb, t, WIDTH_A), BF16),
        scratch_shapes=[pltpu.VMEM((t // KSLAB, KSLAB, TQ), jnp.int32),
                        pltpu.VMEM((nq // 2, 2 * TQ, TQ), jnp.int16),
                        pltpu.VMEM((nq // 2, 2 * TQ, TQ), jnp.int16),
                        pltpu.VMEM((N_HEADS_A, TQ, LANES), BF16),
                        pltpu.VMEM((N_HEADS_A, 1, TQ), F32),
                        pltpu.VMEM((N_HEADS_A, LANES, TQ), F32)],
        compiler_params=pltpu.CompilerParams(dimension_semantics=("parallel", "arbitrary"),
                                             vmem_limit_bytes=52 * 1024 * 1024),
        name="attn_prompt",
    )(p["qcat"], p["wit"], p["qa"], p["kcat"], p["k16"], p["vt"], tri)


N_CH_S = N_PAGES + 1
ROWS_A = N_HEADS_A * DEC_PAD
ROWS_I = N_HEADS_IDX * DEC_PAD
NBUF = 16
N_SEL = N_CH_S + 1
SCORE_PAGES = 8
ATT_PAGES = 4


def _attn_sample_kernel(pt_ref, qih_ref, qil_ref, wi_ref, qa_ref, kin_ref, kn_ref, vn_ref,
                        tri_ref, before_ref, ckidx_ref, ck_ref, cv_ref, o_ref,
                        kidx_buf, knew_buf, kvnew_buf, k_ring, v_ring, keys_ref, selc_ref, sems):
    b = pl.program_id(0)
    nb = pl.num_programs(0)
    cur = b % 2

    def kidx_copy(seq, pg, half):
        return pltpu.make_async_copy(ckidx_ref.at[pt_ref[seq * N_PAGES + pg]], kidx_buf.at[half, pg], sems.at[half])

    def kv_copies(g):
        slot = g % NBUF
        phys = pt_ref[g]
        return (pltpu.make_async_copy(ck_ref.at[phys], k_ring.at[slot], sems.at[2 + slot]),
                pltpu.make_async_copy(cv_ref.at[phys], v_ring.at[slot], sems.at[2 + NBUF + slot]))

    def for_pages(fn):
        def body(pg, carry):
            fn(pg)
            return carry
        lax.fori_loop(0, N_PAGES, body, 0)

    @pl.when(b == 0)
    def _():
        for_pages(lambda pg: kidx_copy(0, pg, 0).start())
        for g in range(NBUF):
            for cp in kv_copies(g):
                cp.start()
        knew_buf[...] = jnp.zeros(knew_buf.shape, F32)
        kvnew_buf[...] = jnp.zeros(kvnew_buf.shape, F32)
        keys_ref[N_CH_S] = jnp.full((DEC_PAD, PAGE_SIZE), KEY_NEG_INF, jnp.int32)

    knew_buf[0:DEC_PAD, :] = kin_ref[0]
    kvnew_buf[0, 0:DEC_PAD, :] = kn_ref[0]
    kvnew_buf[1, 0:DEC_PAD, :] = vn_ref[0]
    for_pages(lambda pg: kidx_copy(b, pg, cur).wait())

    @pl.when(b + 1 < nb)
    def _():
        for_pages(lambda pg: kidx_copy(b + 1, pg, 1 - cur).start())

    w = wi_ref[0]
    q_hi = qih_ref[0]
    q_lo = qil_ref[0]
    trow = lax.broadcasted_iota(jnp.int32, (DEC_PAD, PAGE_SIZE), 0)
    tcol = lax.broadcasted_iota(jnp.int32, (DEC_PAD, PAGE_SIZE), 1)

    def score_pages(kt, first, new_page):
        n = kt.shape[1] // PAGE_SIZE
        k_hi, k_lo = _split2(kt)
        s = (_dot(q_lo, k_lo) + _dot(q_lo, k_hi) + _dot(q_hi, k_lo)) + _dot(q_hi, k_hi)
        acc = jnp.zeros((DEC_PAD, n * PAGE_SIZE), F32)
        for hh in range(N_HEADS_IDX):
            acc = acc + jnp.maximum(s[DEC_PAD * hh:DEC_PAD * (hh + 1)], 0.0) * w[:, hh:hh + 1]
        if new_page:
            acc = jnp.where((tcol <= trow) & (tcol < DEC_SEQ), acc, -jnp.inf)
        kk = _sort_key(acc)
        for j in range(n):
            keys_ref[first + j] = kk[:, PAGE_SIZE * j:PAGE_SIZE * (j + 1)]

    def score_body(i, carry):
        first = i * SCORE_PAGES
        score_pages(jnp.concatenate([kidx_buf[cur, first + j] for j in range(SCORE_PAGES)], axis=1), first, False)
        return carry

    lax.fori_loop(0, N_PAGES // SCORE_PAGES, score_body, 0)
    score_pages(knew_buf[...].T, N_PAGES, True)

    def count(pred):
        m = jnp.where(pred(keys_ref[...]), 1.0, 0.0)
        return jnp.sum(jnp.sum(m, axis=0), axis=1, keepdims=True)

    thr = _kth_largest_key(lambda cand: count(lambda kk: kk >= cand[None]), (DEC_PAD, 1))
    need = float(TOPK) - count(lambda kk: kk > thr[None])

    kk = keys_ref[...]
    eq = kk == thr[None]
    eq2 = jnp.where(eq, 1.0, 0.0).reshape(N_SEL * DEC_PAD, PAGE_SIZE).astype(BF16)
    in_page = _dot(eq2, tri_ref[...])
    per_page = _dot(eq2, jnp.ones((PAGE_SIZE, PAGE_SIZE), BF16))
    rank = (in_page + _dot(before_ref[...], per_page.astype(BF16))).reshape(N_SEL, DEC_PAD, PAGE_SIZE)
    sel = ((kk > thr[None]) | (eq & (rank < need[None]))) & (kk != KEY_NEG_INF)
    selc_ref[...] = jnp.where(sel, 1.0, 0.0)

    lane5 = lax.broadcasted_iota(jnp.int32, (DEC_PAD, WIDTH_A), 1)
    qa = qa_ref[0].astype(F32)
    qbd = jnp.concatenate(
        [jnp.where(lane5 // HEAD_DIM_A == h, qa, 0.0) for h in range(N_HEADS_A)], axis=0).astype(BF16)

    def attend(carry, kt, vt, selc):
        m_old, l_old, acc = carry
        valid = jnp.concatenate([selc] * N_HEADS_A, axis=0) > 0.5
        s = jnp.where(valid, _dot(qbd, kt.astype(BF16)), NEG_BIG)
        m_new = jnp.maximum(m_old, jnp.max(s, axis=1, keepdims=True))
        p = jnp.exp2(s - m_new)
        alpha = jnp.exp2(m_old - m_new)
        return (m_new, alpha * l_old + jnp.sum(p, axis=1, keepdims=True),
                alpha * acc + _dot_nt(p.astype(BF16), vt.astype(BF16)))

    def group_body(gi, carry):
        pg0 = gi * ATT_PAGES
        g0 = b * N_PAGES + pg0
        for j in range(ATT_PAGES):
            for cp in kv_copies(g0 + j):
                cp.wait()
        pages = lambda ring: jnp.concatenate(
            [ring[(pg0 + j) % NBUF].reshape(WIDTH_A, PAGE_SIZE) for j in range(ATT_PAGES)], axis=1)
        selc = jnp.concatenate([selc_ref[pg0 + j] for j in range(ATT_PAGES)], axis=1)
        carry = attend(carry, pages(k_ring), pages(v_ring), selc)
        for j in range(ATT_PAGES):
            @pl.when(g0 + j + NBUF < nb * N_PAGES)
            def _():
                for cp in kv_copies(g0 + j + NBUF):
                    cp.start()
        return carry

    carry = lax.fori_loop(0, N_PAGES // ATT_PAGES, group_body,
                          (jnp.full((ROWS_A, 1), M_INIT, F32), jnp.zeros((ROWS_A, 1), F32),
                           jnp.zeros((ROWS_A, WIDTH_A), F32)))
    _, l, acc = attend(carry, kvnew_buf[0].T, kvnew_buf[1].T, selc_ref[N_PAGES])
    o = acc / l
    out = jnp.zeros((DEC_PAD, WIDTH_A), F32)
    for h in range(N_HEADS_A):
        out = jnp.where(lane5 // HEAD_DIM_A == h, o[DEC_PAD * h:DEC_PAD * (h + 1)], out)
    o_ref[0] = out


def _attn_sample(p, page_table, cache_kidx_t, cache_k_t, cache_v_t, tri):
    nb = p["qa"].shape[0]
    heads_first = lambda a, nh: a.reshape(nb, DEC_PAD, nh, -1).transpose(0, 2, 1, 3).reshape(nb, nh * DEC_PAD, -1)
    qparts = p["qcat"].reshape(nb, DEC_PAD, N_HEADS_IDX, 4, HEAD_DIM_IDX)
    qih = heads_first(qparts[:, :, :, 0], N_HEADS_IDX)
    qil = heads_first(qparts[:, :, :, 2], N_HEADS_IDX)
    rs = np.arange(N_SEL * DEC_PAD)
    before = jnp.asarray(((rs[:, None] % DEC_PAD == rs[None, :] % DEC_PAD)
                          & (rs[None, :] // DEC_PAD < rs[:, None] // DEC_PAD)).astype(np.float32), dtype=BF16)

    blk = lambda rows, w: pl.BlockSpec((1, rows, w), lambda b, pt: (b, 0, 0))
    const = lambda shape: pl.BlockSpec(shape, lambda b, pt: (0,) * len(shape))
    any_spec = pl.BlockSpec(memory_space=pl.ANY)
    grid_spec = pltpu.PrefetchScalarGridSpec(
        num_scalar_prefetch=1,
        grid=(nb,),
        in_specs=[blk(ROWS_I, HEAD_DIM_IDX), blk(ROWS_I, HEAD_DIM_IDX), blk(DEC_PAD, LANES), blk(DEC_PAD, WIDTH_A),
                  blk(DEC_PAD, HEAD_DIM_IDX), blk(DEC_PAD, WIDTH_A), blk(DEC_PAD, WIDTH_A),
                  const((PAGE_SIZE, PAGE_SIZE)), const((N_SEL * DEC_PAD, N_SEL * DEC_PAD)),
                  any_spec, any_spec, any_spec],
        out_specs=blk(DEC_PAD, WIDTH_A),
        scratch_shapes=[pltpu.VMEM((2, N_PAGES, HEAD_DIM_IDX, PAGE_SIZE), F32),
                        pltpu.VMEM((PAGE_SIZE, HEAD_DIM_IDX), F32),
                        pltpu.VMEM((2, PAGE_SIZE, WIDTH_A), F32),
                        pltpu.VMEM((NBUF, N_HEADS_A, HEAD_DIM_A, PAGE_SIZE), F32),
                        pltpu.VMEM((NBUF, N_HEADS_A, HEAD_DIM_A, PAGE_SIZE), F32),
                        pltpu.VMEM((N_SEL, DEC_PAD, PAGE_SIZE), jnp.int32),
                        pltpu.VMEM((N_SEL, DEC_PAD, PAGE_SIZE), F32),
                        pltpu.SemaphoreType.DMA((2 + 2 * NBUF,))],
    )
    return pl.pallas_call(
        _attn_sample_kernel,
        grid_spec=grid_spec,
        out_shape=jax.ShapeDtypeStruct((nb, DEC_PAD, WIDTH_A), F32),
        compiler_params=pltpu.CompilerParams(dimension_semantics=("arbitrary",),
                                             vmem_limit_bytes=40 * 1024 * 1024),
        name="attn_sample",
    )(page_table.reshape(-1), qih, qil, p["wi"], p["qa"], p["ki32"], p["k32"], p["v32"], tri, before,
      cache_kidx_t, cache_k_t, cache_v_t)


def _retention_kernel(q_ref, k_ref, v_ref, s0_ref, dmat_ref, qdec_ref, kdec_ref, gc_ref, bd_ref, hm_ref,
                      ret_ref, sout_ref, state_ref):
    c = pl.program_id(1)

    @pl.when(c == 0)
    def _():
        state_ref[...] = s0_ref[0]

    q = q_ref[0]
    k = k_ref[0]
    v = v_ref[0]
    rows = q.shape[0]
    vk = (v.astype(F32) * kdec_ref[...]).astype(BF16)
    lane = lax.broadcasted_iota(jnp.int32, (rows, LANES), 1)
    for g in range(2):
        gs = slice(256 * g, 256 * (g + 1))
        qg = q[:, gs]
        kg = k[:, gs]
        sg = state_ref[g]
        cross = _dot(qg, sg.astype(BF16)) * qdec_ref[:, gs]
        for j in range(2):
            pair = 2 * g + j
            vp = v[:, LANES * pair:LANES * (pair + 1)]
            halves = []
            for e in range(2):
                hl = 2 * j + e
                qz = qg * hm_ref[hl]
                sc = _dot_nt(qz, kg) * dmat_ref[4 * g + hl]
                halves.append(_dot(sc.astype(BF16), vp))
            inner = jnp.where(lane < DK_R, halves[0], halves[1])
            ret_ref[0, :, LANES * pair:LANES * (pair + 1)] = inner + cross[:, LANES * j:LANES * (j + 1)]
        kv = _dot_tn(kg, vk[:, gs])
        state_ref[g] = sg * gc_ref[:, gs] + kv * bd_ref[...]

    @pl.when(c == pl.num_programs(1) - 1)
    def _():
        sout_ref[0] = state_ref[...]


def _retention(q, k, v, state0, tables, chunk):
    nb, rows, _ = q.shape
    nchunk = rows // chunk
    dmat, qdec, kdec, gc, bd, hm = tables
    tile = pl.BlockSpec((1, chunk, WIDTH_R), lambda b, c: (b, c, 0))
    st = pl.BlockSpec((1, 2, 256, 256), lambda b, c: (b, 0, 0, 0))
    return pl.pallas_call(
        _retention_kernel,
        grid=(nb, nchunk),
        in_specs=[tile, tile, tile, st,
                  _const_spec((N_HEADS_R, chunk, chunk)), _const_spec((chunk, WIDTH_R)), _const_spec((chunk, WIDTH_R)),
                  _const_spec((1, WIDTH_R)), _const_spec((256, 256)), _const_spec((4, 1, 256))],
        out_specs=[tile, st],
        out_shape=[jax.ShapeDtypeStruct((nb, rows, WIDTH_R), F32),
                   jax.ShapeDtypeStruct((nb, 2, 256, 256), F32)],
        scratch_shapes=[pltpu.VMEM((2, 256, 256), F32)],
        compiler_params=pltpu.CompilerParams(dimension_semantics=("parallel", "arbitrary")),
        name="retention",
    )(q, k, v, state0, dmat, qdec, kdec, gc, bd, hm)


def _retention_tables(chunk, n_real):
    lg = jnp.log1p(-jnp.exp2(-5.0 - jnp.arange(N_HEADS_R, dtype=F32)))
    i = jnp.arange(chunk, dtype=F32)
    real = jnp.arange(chunk) < n_real
    diff = i[:, None] - i[None, :]
    dmat = jnp.where(diff >= 0, jnp.exp(lg[:, None, None] * jnp.maximum(diff, 0.0)), 0.0)
    dmat = jnp.where(real[None, None, :], dmat, 0.0)
    q_decay = jnp.exp(lg[None, :] * (i[:, None] + 1.0))
    k_decay = jnp.where(real[:, None], jnp.exp(lg[None, :] * (n_real - 1.0 - i)[:, None]), 0.0)
    per_lane = lambda a: jnp.repeat(a, WIDTH_R // N_HEADS_R, axis=-1)
    gc = per_lane(jnp.exp(lg * n_real)[None, :])
    r = np.arange(256)
    bd = ((r[:, None] % 128) // 32 == (r[None, :] // 64)).astype(np.float32)
    hm = np.stack([((r % 128) // 32 == hl) for hl in range(4)]).astype(np.float32).reshape(4, 1, 256)
    return (dmat, per_lane(q_decay), per_lane(k_decay), gc, jnp.asarray(bd), jnp.asarray(hm, dtype=BF16))


def _merge_kernel(x_ref, gate_ref, ya_ref, sza_ref, ret_ref, szr_ref, sga_ref, sgr_ref,
                  avg_ref, gret_ref, wpa_ref, wpr_ref, wout_ref, gfin_ref, y_ref):
    ret = ret_ref[0]
    avg = avg_ref[...]
    r1, r2 = _split2(ret)
    dev = ret - (_dot(r2, avg) + _dot(r1, avg))
    e1, e2 = _split2(dev * dev)
    var = _dot(e2, avg) + _dot(e1, avg)
    yr = (dev * lax.rsqrt(var + EPS)) * gret_ref[...] * szr_ref[0].astype(F32)
    ya = (ya_ref[0].astype(F32) * sza_ref[0].astype(F32)).astype(BF16)
    merged = (sga_ref[0].astype(F32) * _dot(ya, wpa_ref[...])
              + sgr_ref[0].astype(F32) * _dot(yr.astype(BF16), wpr_ref[...]))
    xo = x_ref[0] + gate_ref[0] * _dot(merged.astype(BF16), wout_ref[...])
    r = lax.rsqrt(jnp.mean(xo * xo, axis=-1, keepdims=True) + EPS)
    y_ref[0] = (xo * r) * gfin_ref[...]


def _merge(x3, gate3, ya, ret, p, avg, g_ret, w_pa, w_pr, w_out, g_final, tm):
    nb, rows, _ = x3.shape
    srows = gate3.shape[1]
    row = lambda w: pl.BlockSpec((1, tm, w), lambda b, i: (b, i, 0))
    g_spec = pl.BlockSpec((1, 1 if srows == 1 else tm, D_MODEL),
                          (lambda b, i: (b, 0, 0)) if srows == 1 else (lambda b, i: (b, i, 0)))
    return pl.pallas_call(
        _merge_kernel,
        grid=(nb, rows // tm),
        in_specs=[row(D_MODEL), g_spec, row(WIDTH_A), row(WIDTH_A), row(WIDTH_R), row(WIDTH_R), row(D_MODEL),
                  row(D_MODEL),
                  _const_spec((WIDTH_R, WIDTH_R)), _const_spec((1, WIDTH_R)),
                  _const_spec((WIDTH_A, D_MODEL)), _const_spec((WIDTH_R, D_MODEL)), _const_spec((D_MODEL, D_MODEL)),
                  _const_spec((1, D_MODEL))],
        out_specs=row(D_MODEL),
        out_shape=jax.ShapeDtypeStruct((nb, rows, D_MODEL), F32),
        compiler_params=pltpu.CompilerParams(dimension_semantics=("parallel", "parallel"),
                                             vmem_limit_bytes=48 * 1024 * 1024),
        name="merge",
    )(x3, gate3, ya, p["sza"], ret, p["szr"], p["sga"], p["sgr"], avg, g_ret.reshape(1, -1),
      w_pa, w_pr, w_out, g_final.reshape(1, -1))


def _group_rows(w):
    d = w.shape[1]
    return w.reshape(2, 4, 2, 32, d).transpose(0, 2, 1, 3, 4).reshape(512, d)


def _prep_weights(w_in):
    wt = w_in.T
    s = lambda o, n: wt[o:o + n]
    main = jnp.concatenate([
        s(_O_QA, 512) * (HEAD_DIM_A ** -0.5 * LOG2E),
        s(_O_KA, 512), s(_O_VA, 512), s(_O_ZA, 512),
        _group_rows(s(_O_QR, 512)), _group_rows(s(_O_KR, 512)) * DK_R ** -0.5, s(_O_VR, 512), s(_O_ZR, 512),
        s(_O_GA, 1024), s(_O_GR, 1024)], axis=0).astype(BF16)
    qi = jnp.broadcast_to(s(_O_QI, 256).reshape(N_HEADS_IDX, 1, HEAD_DIM_IDX, D_MODEL),
                          (N_HEADS_IDX, 2, HEAD_DIM_IDX, D_MODEL)).reshape(512, D_MODEL)
    ki = s(_O_KI, 64)
    wi = s(_O_WI, N_HEADS_IDX) * (N_HEADS_IDX ** -0.5 * HEAD_DIM_IDX ** -0.5)
    idx = jnp.concatenate([qi, ki, ki, wi, jnp.zeros((LANES - N_HEADS_IDX, D_MODEL), F32)], axis=0)
    i1 = idx.astype(BF16)
    i2 = (idx - i1.astype(F32)).astype(BF16)
    return main, i1, i2


def _rope_tables(pos):
    half = DK_R // 2
    inv = ROPE_BASE ** (-jnp.arange(half, dtype=F32) / half)
    ang = pos.astype(F32)[:, None] * inv[None, :]
    return jnp.tile(jnp.cos(ang), (1, 4)), jnp.tile(jnp.sin(ang), (1, 4))


def _state_to_groups(state):
    nb = state.shape[0]
    st = state.reshape(nb, 2, 4, 2, 32, 64)
    eye = jnp.eye(4, dtype=state.dtype)
    out = st.transpose(0, 1, 3, 2, 4, 5)[:, :, :, :, :, None, :] * eye[None, None, None, :, None, :, None]
    return out.reshape(nb, 2, 256, 256)


def _groups_to_state(sg):
    nb = sg.shape[0]
    s7 = sg.reshape(nb, 2, 2, 4, 32, 4, 64)
    diag = jnp.stack([s7[:, :, :, hl, :, hl, :] for hl in range(4)], axis=2)
    return diag.reshape(nb, N_HEADS_R, DK_R, 64)


def kernel(x_prompt, x_sample, cache_k, cache_v, cache_kidx, state_ret, page_table, c_prompt, c_sample,
           w_ada, b_ada, g_norm, w_in, g_ret, w_pa, w_pr, w_out, g_final):
    nbp, t, _ = x_prompt.shape
    nbs = x_sample.shape[0]
    depth = w_in.shape[0]
    assert depth == 1

    r = np.arange(TQ)
    tri256 = jnp.asarray((r[:, None] < r[None, :]).astype(np.float32), dtype=BF16)
    tri128 = tri256[:PAGE_SIZE, :PAGE_SIZE]
    c5 = np.arange(WIDTH_R)
    avg = jnp.asarray((c5[:, None] // 64 == c5[None, :] // 64).astype(np.float32) / 64.0, dtype=BF16)
    cos_p, sin_p = _rope_tables(jnp.arange(t))
    cos_s, sin_s = _rope_tables(jnp.tile(PAST_LEN + jnp.arange(DEC_PAD), nbs))
    tab_p = _retention_tables(CHUNK_R, CHUNK_R)
    tab_s = _retention_tables(DEC_PAD, DEC_SEQ)

    hp = x_prompt
    hs = jnp.pad(x_sample, ((0, 0), (0, DEC_PAD - DEC_SEQ), (0, 0))).reshape(1, nbs * DEC_PAD, D_MODEL)
    outs_p, outs_s = [], []
    for l in range(depth):
        w_main, wi1, wi2 = _prep_weights(w_in[l])
        wpa, wpr, wout = w_pa[l].astype(BF16), w_pr[l].astype(BF16), w_out[l].astype(BF16)

        c_all = jnp.concatenate([c_prompt, c_sample, jnp.zeros((6, D_MODEL), F32)], axis=0)
        mod = _adaln(c_all, w_ada[l], b_ada[l])
        shift, scale, gate = mod[:, :D_MODEL], mod[:, D_MODEL:2 * D_MODEL], mod[:, 2 * D_MODEL:]
        per_p = lambda a: a[:nbp].reshape(nbp, 1, D_MODEL)
        per_s = lambda a: jnp.broadcast_to(a[nbp:nbp + nbs, None, :], (nbs, DEC_PAD, D_MODEL)).reshape(
            1, nbs * DEC_PAD, D_MODEL)

        pp = _project(hp, per_p(scale), per_p(shift), g_norm[l], cos_p, sin_p, w_main, wi1, wi2, tm=256)
        ya_p = _attn_prompt(pp, tri128.T)
        ret_p, sg_p = _retention(pp["qr"], pp["kr"], pp["vr"], jnp.zeros((nbp, 2, 256, 256), F32), tab_p, CHUNK_R)
        hp = _merge(hp, per_p(gate), ya_p, ret_p, pp, avg, g_ret[l], wpa, wpr, wout, g_final, tm=256)

        ps = _project(hs, per_s(scale), per_s(shift), g_norm[l], cos_s, sin_s, w_main, wi1, wi2, tm=nbs * DEC_PAD)
        ps3 = {o[0]: ps[o[0]].reshape(nbs, DEC_PAD, o[2]) for o in _PROJ_OUTS if o[1] == "row"}
        ya_s = _attn_sample(ps3, page_table, cache_kidx[l].transpose(0, 2, 1),
                            cache_k[l].transpose(0, 2, 3, 1), cache_v[l].transpose(0, 2, 3, 1), tri128)
        ret_s, sg_s = _retention(ps3["qr"], ps3["kr"], ps3["vr"], _state_to_groups(state_ret[l].astype(F32)),
                                 tab_s, DEC_PAD)
        hs = _merge(hs, per_s(gate), ya_s.reshape(1, nbs * DEC_PAD, WIDTH_A),
                    ret_s.reshape(1, nbs * DEC_PAD, WIDTH_R), ps, avg, g_ret[l], wpa, wpr, wout, g_final,
                    tm=nbs * DEC_PAD)

        heads = lambda a, n: a.reshape(n, -1, N_HEADS_A, HEAD_DIM_A)
        tokens_first = lambda a: a.reshape(nbp, N_HEADS_A, HEAD_DIM_A, t).transpose(0, 3, 1, 2)
        outs_p.append((tokens_first(pp["kt32"]), tokens_first(pp["vt32"]), pp["kit32"].transpose(0, 2, 1),
                       _groups_to_state(sg_p)))
        tok = lambda a: a[:, :DEC_SEQ]
        outs_s.append((heads(tok(ps3["k32"]), nbs), heads(tok(ps3["v32"]), nbs), tok(ps3["ki32"]),
                       _groups_to_state(sg_s)))

    y_prompt = hp
    y_sample = hs.reshape(nbs, DEC_PAD, D_MODEL)[:, :DEC_SEQ]
    stack = lambda items, i: jnp.stack([it[i] for it in items])
    return (y_prompt, y_sample,
            stack(outs_p, 0), stack(outs_p, 1), stack(outs_p, 2), stack(outs_p, 3),
            stack(outs_s, 0), stack(outs_s, 1), stack(outs_s, 2), stack(outs_s, 3))
```

```python
import functools

import numpy as np
import jax
import jax.numpy as jnp
from jax import lax
from jax.experimental import pallas as pl
from jax.experimental.pallas import tpu as pltpu

D_MODEL = 1024
SEQ = 8192
DEC_SEQ = 4
PAST_LEN = 8192
PAGE_SIZE = 128
N_HEADS_A = 8
HEAD_DIM_A = 64
WIDTH_A = 512
N_HEADS_IDX = 4
HEAD_DIM_IDX = 64
TOPK = 256
N_HEADS_R = 8
DK_R = 64
WIDTH_R = 512
CHUNK_R = 128
ROPE_BASE = 10000.0
EPS = 1e-6

LANES = 128
SUBLANES = 8
DEC_PAD = SUBLANES
N_PAGES = PAST_LEN // PAGE_SIZE

_O_QA, _O_KA, _O_VA, _O_ZA = 0, 512, 1024, 1536
_O_QI, _O_KI, _O_WI = 2048, 2304, 2368
_O_QR, _O_KR, _O_VR, _O_ZR = 2372, 2884, 3396, 3908
_O_GA, _O_GR, _N_IN = 4420, 5444, 6468

N_MAIN = 6144
N_IDX = 768

INT_MIN = np.int32(-2 ** 31)
KEY_NEG_INF = np.int32(np.array(0xFF800000, np.uint32).view(np.int32) ^ np.int32(0x7FFFFFFF))
NEG_BIG = -1e30
M_INIT = -1e29
LOG2E = 1.4426950408889634
KSLAB = 128
PACK16 = 16
I16_MIN = np.int16(-2 ** 15)

F32 = jnp.float32
BF16 = jnp.bfloat16


def _dot(a, b):
    return jnp.dot(a, b, preferred_element_type=F32)


def _dot_nt(a, b):
    return lax.dot_general(a, b, (((1,), (1,)), ((), ())), preferred_element_type=F32)


def _dot_tn(a, b):
    return lax.dot_general(a, b, (((0,), (0,)), ((), ())), preferred_element_type=F32)


def _split2(x):
    hi = x.astype(BF16)
    lo = (x - hi.astype(F32)).astype(BF16)
    return hi, lo


def _split3(x):
    hi = x.astype(BF16)
    r = x - hi.astype(F32)
    mid = r.astype(BF16)
    lo = (r - mid.astype(F32)).astype(BF16)
    return hi, mid, lo


def _sort_key(score):
    bits = pltpu.bitcast(score, jnp.int32)
    return bits ^ ((bits >> 31) & np.int32(0x7FFFFFFF))


def _const_spec(shape):
    nd = len(shape)
    return pl.BlockSpec(shape, lambda *_: (0,) * nd, pipeline_mode=pl.Buffered(1))


def _adaln_kernel(c_ref, w_ref, b_ref, o_ref):
    c = c_ref[...]
    a1, a2, a3 = _split3(c * jax.nn.sigmoid(c))
    w1, w2, w3 = _split3(w_ref[...])
    small = _dot(a1, w3) + _dot(a2, w2) + _dot(a3, w1)
    mid = _dot(a1, w2) + _dot(a2, w1)
    o_ref[...] = (small + mid) + _dot(a1, w1) + b_ref[...]


def _adaln(c_all, w_ada, b_ada):
    rows = c_all.shape[0]
    tn = 512
    return pl.pallas_call(
        _adaln_kernel,
        grid=(3 * D_MODEL // tn,),
        in_specs=[pl.BlockSpec((rows, D_MODEL), lambda j: (0, 0)),
                  pl.BlockSpec((D_MODEL, tn), lambda j: (0, j)),
                  pl.BlockSpec((1, tn), lambda j: (0, j))],
        out_specs=pl.BlockSpec((rows, tn), lambda j: (0, j)),
        out_shape=jax.ShapeDtypeStruct((rows, 3 * D_MODEL), F32),
        name="adaln",
    )(c_all, w_ada, b_ada.reshape(1, -1))


_PROJ_OUTS = (
    ("qa", "row", 512, BF16), ("k32", "row", 512, F32), ("v32", "row", 512, F32), ("kt32", "col", 512, F32),
    ("vt32", "col", 512, F32), ("k16", "row", 512, BF16), ("vt", "slab", 512, BF16),
    ("sza", "row", 512, BF16), ("qcat", "row", 1024, BF16), ("kcat", "row", 256, BF16),
    ("ki32", "row", 64, F32), ("kit32", "col", 64, F32), ("wi", "row", 128, F32), ("wit", "slab", SUBLANES, F32),
    ("qr", "row", 512, BF16), ("kr", "row", 512, BF16), ("vr", "row", 512, BF16), ("szr", "row", 512, BF16),
    ("sga", "row", 1024, BF16), ("sgr", "row", 1024, BF16),
)
_PROJ_SLAB = {"vt": KSLAB, "wit": None}


def _proj_kernel(x_ref, scale_ref, shift_ref, gn_ref, cos_ref, sin_ref, wm_ref, wi1_ref, wi2_ref,
                 qa_ref, k32_ref, v32_ref, kt32_ref, vt32_ref, k16_ref, vt_ref, sza_ref, qcat_ref, kcat_ref,
                 ki32_ref, kit32_ref, wi_ref, wit_ref, qr_ref, kr_ref, vr_ref, szr_ref, sga_ref, sgr_ref):
    x = x_ref[0]
    r = lax.rsqrt(jnp.mean(x * x, axis=-1, keepdims=True) + EPS)
    h = (x * r) * gn_ref[...] * (1.0 + scale_ref[0]) + shift_ref[0]
    h1, h2 = _split2(h)

    def main(g):
        return _dot_nt(h1, wm_ref[g * 512:(g + 1) * 512, :])

    qa_ref[0] = main(0).astype(BF16)
    u = main(1)
    k32_ref[0] = u
    kt32_ref[0] = u.T
    k16_ref[0] = u.astype(BF16)
    u = main(2)
    v32_ref[0] = u
    ut = u.T
    vt32_ref[0] = ut
    for sl in range(u.shape[0] // KSLAB):
        vt_ref[0, sl] = ut[:, KSLAB * sl:KSLAB * (sl + 1)].astype(BF16)
    u = main(3)
    sza_ref[0] = (u * jax.nn.sigmoid(u)).astype(BF16)

    cos = cos_ref[...]
    sin = sin_ref[...]
    for g, o_ref in ((4, qr_ref), (5, kr_ref)):
        u = main(g)
        for grp in range(2):
            x1 = u[:, 256 * grp:256 * grp + 128]
            x2 = u[:, 256 * grp + 128:256 * grp + 256]
            o_ref[0, :, 256 * grp:256 * grp + 128] = (x1 * cos - x2 * sin).astype(BF16)
            o_ref[0, :, 256 * grp + 128:256 * grp + 256] = (x1 * sin + x2 * cos).astype(BF16)
    vr_ref[0] = main(6).astype(BF16)
    u = main(7)
    szr_ref[0] = (u * jax.nn.sigmoid(u)).astype(BF16)
    for j in range(2):
        sga_ref[0, :, 512 * j:512 * (j + 1)] = jax.nn.sigmoid(main(8 + j)).astype(BF16)
        sgr_ref[0, :, 512 * j:512 * (j + 1)] = jax.nn.sigmoid(main(10 + j)).astype(BF16)

    w1 = wi1_ref[...]
    w2 = wi2_ref[...]
    ui = (_dot_nt(h2, w2) + _dot_nt(h2, w1) + _dot_nt(h1, w2)) + _dot_nt(h1, w1)
    qd = ui[:, :512]
    q_hi, q_lo = _split2(qd)
    for hh in range(N_HEADS_IDX):
        qcat_ref[0, :, 256 * hh:256 * hh + 128] = q_hi[:, 128 * hh:128 * hh + 128]
        qcat_ref[0, :, 256 * hh + 128:256 * hh + 256] = q_lo[:, 128 * hh:128 * hh + 128]
    kd = ui[:, 512:640]
    k_hi, k_lo = _split2(kd)
    lane = lax.broadcasted_iota(jnp.int32, kd.shape, 1)
    sel = jnp.where(lane < HEAD_DIM_IDX, k_hi, k_lo)
    kcat_ref[0, :, 0:128] = sel
    kcat_ref[0, :, 128:256] = sel
    ki32_ref[0] = kd[:, :HEAD_DIM_IDX]
    kit32_ref[0] = kd.T[0:HEAD_DIM_IDX, :]
    wi_ref[0] = ui[:, 640:768]
    wit_ref[0, 0] = ui[:, 640:768].T[0:SUBLANES, :]


def _project(x3, scale3, shift3, g_norm, cos_t, sin_t, w_main, wi1, wi2, tm):
    nb, rows, _ = x3.shape
    srows = scale3.shape[1]
    stile = 1 if srows == 1 else tm
    grid = (nb, rows // tm)
    row_spec = lambda w: pl.BlockSpec((1, tm, w), lambda b, i: (b, i, 0))
    s_spec = pl.BlockSpec((1, stile, D_MODEL), (lambda b, i: (b, 0, 0)) if srows == 1 else (lambda b, i: (b, i, 0)))

    def out(name, layout, dim):
        if layout == "row":
            return (nb, rows, dim), pl.BlockSpec((1, tm, dim), lambda b, i: (b, i, 0))
        if layout == "col":
            return (nb, dim, rows), pl.BlockSpec((1, dim, tm), lambda b, i: (b, 0, i))
        w = _PROJ_SLAB[name] or tm
        return (nb, rows // w, dim, w), pl.BlockSpec((1, tm // w, dim, w), lambda b, i: (b, i, 0, 0))

    outs_meta = [out(n, lay, dim) for n, lay, dim, _ in _PROJ_OUTS]
    in_specs = [row_spec(D_MODEL), s_spec, s_spec, _const_spec((1, D_MODEL)),
                pl.BlockSpec((tm, LANES), lambda b, i: (i, 0)), pl.BlockSpec((tm, LANES), lambda b, i: (i, 0)),
                _const_spec((N_MAIN, D_MODEL)), _const_spec((N_IDX, D_MODEL)), _const_spec((N_IDX, D_MODEL))]
    outs = pl.pallas_call(
        _proj_kernel,
        grid=grid,
        in_specs=in_specs,
        out_specs=[spec for _, spec in outs_meta],
        out_shape=[jax.ShapeDtypeStruct(shape, o[3]) for (shape, _), o in zip(outs_meta, _PROJ_OUTS)],
        compiler_params=pltpu.CompilerParams(dimension_semantics=("parallel", "parallel"),
                                             vmem_limit_bytes=52 * 1024 * 1024),
        name="proj",
    )(x3, scale3, shift3, g_norm.reshape(1, -1), cos_t, sin_t, w_main, wi1, wi2)
    return {o[0]: a for o, a in zip(_PROJ_OUTS, outs)}


TQ = 256


def _kth_largest_key(count_ge, shape, k=float(TOPK), nbits=32):
    lowest = -(1 << (nbits - 1))
    prefix = jnp.full(shape, lowest, jnp.int32)
    for b in range(nbits):
        bit = np.int32(lowest if b == 0 else 1 << (nbits - 1 - b))
        cand = prefix ^ bit
        prefix = jnp.where(count_ge(cand) >= k, cand, prefix)
    return prefix


def _kth_largest_key_radix4(count_ge, shape):
    prefix = jnp.full(shape, INT_MIN, jnp.int32)
    for shift in range(30, -1, -2):
        cands = [prefix + np.uint32(j << shift).astype(np.int32) for j in (1, 2, 3)]
        hits = [count_ge(c) >= float(TOPK) for c in cands]
        for c, hit in zip(cands, hits):
            prefix = jnp.where(hit, c, prefix)
    return prefix


def _attn_prompt_kernel(qcat_ref, wit_ref, qa_ref, kcat_ref, k_ref, vt_ref, tri_ref, o_ref,
                        keys_ref, hi_ref, lo_ref, qz_ref, m_ref, acc_ref):
    qi = pl.program_id(1)
    nch = qi + 1
    wt = wit_ref[0, 0]

    def rows(ref, c, n):
        return ref[0, pl.ds(pl.multiple_of(c * n, n), n), :]

    def score_chunk(c, diagonal, pair, half):
        kc = rows(kcat_ref, c, TQ)
        acc = jnp.zeros((TQ, TQ), F32)
        for hh in range(N_HEADS_IDX):
            s = _dot_nt(kc, qcat_ref[0, :, 256 * hh:256 * (hh + 1)])
            acc = acc + jnp.maximum(s, 0.0) * wt[hh:hh + 1, :]
        if diagonal:
            key = lax.broadcasted_iota(jnp.int32, (TQ, TQ), 0)
            qry = lax.broadcasted_iota(jnp.int32, (TQ, TQ), 1)
            acc = jnp.where(key <= qry, acc, -jnp.inf)
        kk = _sort_key(acc)
        keys_ref[2 * c] = kk[:KSLAB]
        keys_ref[2 * c + 1] = kk[KSLAB:]
        hi_ref[pair, TQ * half:TQ * (half + 1), :] = (kk >> 16).astype(jnp.int16)
        lo_ref[pair, TQ * half:TQ * (half + 1), :] = (((kk ^ np.int32(0x8000)) << 16) >> 16).astype(jnp.int16)

    def score_pair(pair, carry):
        score_chunk(2 * pair, False, pair, 0)
        score_chunk(2 * pair + 1, False, pair, 1)
        return carry

    lax.fori_loop(0, qi // 2, score_pair, 0)
    last = qi // 2
    npairs = last + 1

    @pl.when(qi % 2 == 0)
    def _():
        score_chunk(qi, True, last, 0)
        never = jnp.full((TQ, TQ), I16_MIN, jnp.int16)
        hi_ref[last, TQ:, :] = never
        lo_ref[last, TQ:, :] = never

    @pl.when(qi % 2 == 1)
    def _():
        score_chunk(qi - 1, False, last, 0)
        score_chunk(qi, True, last, 1)

    def count16(ref, pred):
        def body(pair, acc):
            m = jnp.where(pred(ref[pair]), jnp.int16(1), jnp.int16(0))
            parts = [m[PACK16 * i:PACK16 * (i + 1)] for i in range(2 * TQ // PACK16)]
            while len(parts) > 1:
                parts = [parts[i] + parts[i + 1] for i in range(0, len(parts), 2)]
            return acc + parts[0].astype(jnp.int32)
        acc = lax.fori_loop(0, npairs, body, jnp.zeros((PACK16, TQ), jnp.int32))
        return jnp.sum(acc.astype(F32), axis=0, keepdims=True)

    as16 = lambda v: v.astype(jnp.int16)
    hi_thr = _kth_largest_key(lambda cand: count16(hi_ref, lambda x: x >= as16(cand)), (1, TQ), nbits=16)
    above = count16(hi_ref, lambda x: x > as16(hi_thr))

    def keep_tied_lows(pair, carry):
        lo_ref[pair] = jnp.where(hi_ref[pair] == as16(hi_thr), lo_ref[pair], I16_MIN)
        return carry

    lax.fori_loop(0, npairs, keep_tied_lows, 0)
    lo_thr = _kth_largest_key(lambda cand: count16(lo_ref, lambda x: x >= as16(cand)), (1, TQ),
                              k=float(TOPK) - above, nbits=16)
    thr = (hi_thr << 16) | ((lo_thr + 2 ** 15) & np.int32(0xFFFF))
    need = float(TOPK) - above - count16(lo_ref, lambda x: x > as16(lo_thr))

    lane = lax.broadcasted_iota(jnp.int32, (TQ, LANES), 1)
    for h in range(N_HEADS_A):
        pair = qa_ref[0, :, LANES * (h // 2):LANES * (h // 2 + 1)]
        mine = (lane < HEAD_DIM_A) if h % 2 == 0 else (lane >= HEAD_DIM_A)
        qz_ref[h] = jnp.where(mine, pair, jnp.zeros_like(pair))
    m_ref[...] = jnp.full(m_ref.shape, M_INIT, F32)
    acc_ref[...] = jnp.zeros(acc_ref.shape, F32)
    dim16 = lax.broadcasted_iota(jnp.int32, (LANES, KSLAB), 0)

    def attn_slab(sl, ties_before):
        kk = keys_ref[sl]
        eq = kk == thr
        eqf = jnp.where(eq, 1.0, 0.0)
        rank = _dot(tri_ref[...], eqf.astype(BF16)) + ties_before
        sel = ((kk > thr) | (eq & (rank < need))) & (kk != KEY_NEG_INF)
        kc = rows(k_ref, sl, KSLAB)
        vts = vt_ref[0, sl]
        for h in range(N_HEADS_A):
            lo = LANES * (h // 2)
            s = jnp.where(sel, _dot_nt(kc[:, lo:lo + LANES], qz_ref[h]), NEG_BIG)
            m_old = m_ref[h]
            m_new = jnp.maximum(m_old, jnp.max(s, axis=0, keepdims=True))
            p = jnp.exp2(s - m_new)
            mine = (dim16 < HEAD_DIM_A) if h % 2 == 0 else (dim16 >= HEAD_DIM_A)
            v1 = jnp.where(mine, vts[lo:lo + LANES, :], jnp.ones((), BF16))
            acc_ref[h] = jnp.exp2(m_old - m_new) * acc_ref[h] + _dot(v1, p.astype(BF16))
            m_ref[h] = m_new
        return ties_before + jnp.sum(eqf, axis=0, keepdims=True)

    def attn_body(c, ties_before):
        return attn_slab(2 * c + 1, attn_slab(2 * c, ties_before))

    lax.fori_loop(0, nch, attn_body, jnp.zeros((1, TQ), F32))

    dim = lax.broadcasted_iota(jnp.int32, (LANES, TQ), 0)
    for j in range(N_HEADS_A // 2):
        a0 = acc_ref[2 * j]
        a1 = acc_ref[2 * j + 1]
        even = a0 / a0[HEAD_DIM_A:HEAD_DIM_A + 1]
        odd = a1 / a1[0:1]
        o_ref[0, :, LANES * j:LANES * (j + 1)] = jnp.where(dim < HEAD_DIM_A, even, odd).T.astype(BF16)


def _attn_prompt(p, tri):
    nb, t, _ = p["qa"].shape
    nq = t // TQ
    tile = lambda w: pl.BlockSpec((1, TQ, w), lambda b, i: (b, i, 0))
    full = lambda w: pl.BlockSpec((1, t, w), lambda b, i: (b, 0, 0), pipeline_mode=pl.Buffered(1))
    return pl.pallas_call(
        _attn_prompt_kernel,
        grid=(nb, nq),
        in_specs=[tile(1024), pl.BlockSpec((1, 1, SUBLANES, TQ), lambda b, i: (b, i, 0, 0)),
                  tile(WIDTH_A), full(256), full(WIDTH_A),
                  pl.BlockSpec((1, t // KSLAB, WIDTH_A, KSLAB), lambda b, i: (b, 0, 0, 0),
                               pipeline_mode=pl.Buffered(1)),
                  _const_spec((KSLAB, KSLAB))],
        out_specs=tile(WIDTH_A),
        out_shape=jax.ShapeDtypeStruct((nb, t, WIDTH_A), BF16),
        scratch_shapes=[pltpu.VMEM((t // KSLAB, KSLAB, TQ), jnp.int32),
                        pltpu.VMEM((nq // 2, 2 * TQ, TQ), jnp.int16),
                        pltpu.VMEM((nq // 2, 2 * TQ, TQ), jnp.int16),
                        pltpu.VMEM((N_HEADS_A, TQ, LANES), BF16),
                        pltpu.VMEM((N_HEADS_A, 1, TQ), F32),
                        pltpu.VMEM((N_HEADS_A, LANES, TQ), F32)],
        compiler_params=pltpu.CompilerParams(dimension_semantics=("parallel", "arbitrary"),
                                             vmem_limit_bytes=52 * 1024 * 1024),
        name="attn_prompt",
    )(p["qcat"], p["wit"], p["qa"], p["kcat"], p["k16"], p["vt"], tri)


N_CH_S = N_PAGES + 1
ROWS_A = N_HEADS_A * DEC_PAD
ROWS_I = N_HEADS_IDX * DEC_PAD
NBUF = N_PAGES
N_SEL = N_CH_S + 1
SCORE_PAGES = 8
ATT_PAGES = 8


def _attn_sample_kernel(pt_ref, qih_ref, qil_ref, wi_ref, qa_ref, kin_ref, kn_ref, vn_ref,
                        tri_ref, before_ref, ckidx_ref, ck_ref, cv_ref, o_ref,
                        kidx_buf, knew_buf, kvnew_buf, k_ring, v_ring, keys_ref, selc_ref, sems):
    b = pl.program_id(0)
    nb = pl.num_programs(0)
    cur = b % 2

    def kidx_copy(seq, pg, half):
        return pltpu.make_async_copy(ckidx_ref.at[pt_ref[seq * N_PAGES + pg]], kidx_buf.at[half, pg], sems.at[half])

    def kv_copies(g):
        slot = g % NBUF
        phys = pt_ref[g]
        return (pltpu.make_async_copy(ck_ref.at[phys], k_ring.at[slot], sems.at[2 + slot]),
                pltpu.make_async_copy(cv_ref.at[phys], v_ring.at[slot], sems.at[2 + NBUF + slot]))

    def for_pages(fn):
        def body(pg, carry):
            fn(pg)
            return carry
        lax.fori_loop(0, N_PAGES, body, 0)

    @pl.when(b == 0)
    def _():
        for_pages(lambda pg: kidx_copy(0, pg, 0).start())
        def first_pages(g, carry):
            for cp in kv_copies(g):
                cp.start()
            return carry

        lax.fori_loop(0, NBUF, first_pages, 0)
        knew_buf[...] = jnp.zeros(knew_buf.shape, F32)
        kvnew_buf[...] = jnp.zeros(kvnew_buf.shape, F32)
        keys_ref[N_CH_S] = jnp.full((DEC_PAD, PAGE_SIZE), KEY_NEG_INF, jnp.int32)

    knew_buf[0:DEC_PAD, :] = kin_ref[0]
    kvnew_buf[0, 0:DEC_PAD, :] = kn_ref[0]
    kvnew_buf[1, 0:DEC_PAD, :] = vn_ref[0]
    for_pages(lambda pg: kidx_copy(b, pg, cur).wait())

    @pl.when(b + 1 < nb)
    def _():
        for_pages(lambda pg: kidx_copy(b + 1, pg, 1 - cur).start())

    w = wi_ref[0]
    q_hi = qih_ref[0]
    q_lo = qil_ref[0]
    trow = lax.broadcasted_iota(jnp.int32, (DEC_PAD, PAGE_SIZE), 0)
    tcol = lax.broadcasted_iota(jnp.int32, (DEC_PAD, PAGE_SIZE), 1)

    def score_pages(kt, first, new_page):
        n = kt.shape[1] // PAGE_SIZE
        k_hi, k_lo = _split2(kt)
        s = (_dot(q_lo, k_lo) + _dot(q_lo, k_hi) + _dot(q_hi, k_lo)) + _dot(q_hi, k_hi)
        acc = jnp.zeros((DEC_PAD, n * PAGE_SIZE), F32)
        for hh in range(N_HEADS_IDX):
            acc = acc + jnp.maximum(s[DEC_PAD * hh:DEC_PAD * (hh + 1)], 0.0) * w[:, hh:hh + 1]
        if new_page:
            acc = jnp.where((tcol <= trow) & (tcol < DEC_SEQ), acc, -jnp.inf)
        kk = _sort_key(acc)
        for j in range(n):
            keys_ref[first + j] = kk[:, PAGE_SIZE * j:PAGE_SIZE * (j + 1)]

    def score_body(i, carry):
        first = i * SCORE_PAGES
        score_pages(jnp.concatenate([kidx_buf[cur, first + j] for j in range(SCORE_PAGES)], axis=1), first, False)
        return carry

    lax.fori_loop(0, N_PAGES // SCORE_PAGES, score_body, 0)
    score_pages(knew_buf[...].T, N_PAGES, True)

    def count(pred):
        m = jnp.where(pred(keys_ref[...]), 1.0, 0.0)
        return jnp.sum(jnp.sum(m, axis=0), axis=1, keepdims=True)

    thr = _kth_largest_key_radix4(lambda cand: count(lambda kk: kk >= cand[None]), (DEC_PAD, 1))
    need = float(TOPK) - count(lambda kk: kk > thr[None])

    kk = keys_ref[...]
    eq = kk == thr[None]
    eq2 = jnp.where(eq, 1.0, 0.0).reshape(N_SEL * DEC_PAD, PAGE_SIZE).astype(BF16)
    in_page = _dot(eq2, tri_ref[...])
    per_page = _dot(eq2, jnp.ones((PAGE_SIZE, PAGE_SIZE), BF16))
    rank = (in_page + _dot(before_ref[...], per_page.astype(BF16))).reshape(N_SEL, DEC_PAD, PAGE_SIZE)
    sel = ((kk > thr[None]) | (eq & (rank < need[None]))) & (kk != KEY_NEG_INF)
    selc_ref[...] = jnp.where(sel, 1.0, 0.0)

    lane5 = lax.broadcasted_iota(jnp.int32, (DEC_PAD, WIDTH_A), 1)
    qa = qa_ref[0].astype(F32)
    qbd = jnp.concatenate(
        [jnp.where(lane5 // HEAD_DIM_A == h, qa, 0.0) for h in range(N_HEADS_A)], axis=0).astype(BF16)

    def attend(carry, kt, vt, selc):
        m_old, l_old, acc = carry
        valid = jnp.concatenate([selc] * N_HEADS_A, axis=0) > 0.5
        s = jnp.where(valid, _dot(qbd, kt.astype(BF16)), NEG_BIG)
        m_new = jnp.maximum(m_old, jnp.max(s, axis=1, keepdims=True))
        p = jnp.exp2(s - m_new)
        alpha = jnp.exp2(m_old - m_new)
        return (m_new, alpha * l_old + jnp.sum(p, axis=1, keepdims=True),
                alpha * acc + _dot_nt(p.astype(BF16), vt.astype(BF16)))

    def group_body(gi, carry):
        pg0 = gi * ATT_PAGES
        g0 = b * N_PAGES + pg0
        for j in range(ATT_PAGES):
            for cp in kv_copies(g0 + j):
                cp.wait()
        pages = lambda ring: jnp.concatenate(
            [ring[(pg0 + j) % NBUF].reshape(WIDTH_A, PAGE_SIZE) for j in range(ATT_PAGES)], axis=1)
        selc = jnp.concatenate([selc_ref[pg0 + j] for j in range(ATT_PAGES)], axis=1)
        carry = attend(carry, pages(k_ring), pages(v_ring), selc)
        for j in range(ATT_PAGES):
            @pl.when(g0 + j + NBUF < nb * N_PAGES)
            def _():
                for cp in kv_copies(g0 + j + NBUF):
                    cp.start()
        return carry

    carry = lax.fori_loop(0, N_PAGES // ATT_PAGES, group_body,
                          (jnp.full((ROWS_A, 1), M_INIT, F32), jnp.zeros((ROWS_A, 1), F32),
                           jnp.zeros((ROWS_A, WIDTH_A), F32)))
    _, l, acc = attend(carry, kvnew_buf[0].T, kvnew_buf[1].T, selc_ref[N_PAGES])
    o = acc / l
    out = jnp.zeros((DEC_PAD, WIDTH_A), F32)
    for h in range(N_HEADS_A):
        out = jnp.where(lane5 // HEAD_DIM_A == h, o[DEC_PAD * h:DEC_PAD * (h + 1)], out)
    o_ref[0] = out


def _attn_sample(p, page_table, cache_kidx_t, cache_k_t, cache_v_t, tri):
    nb = p["qa"].shape[0]
    heads_first = lambda a, nh: a.reshape(nb, DEC_PAD, nh, -1).transpose(0, 2, 1, 3).reshape(nb, nh * DEC_PAD, -1)
    qparts = p["qcat"].reshape(nb, DEC_PAD, N_HEADS_IDX, 4, HEAD_DIM_IDX)
    qih = heads_first(qparts[:, :, :, 0], N_HEADS_IDX)
    qil = heads_first(qparts[:, :, :, 2], N_HEADS_IDX)
    rs = np.arange(N_SEL * DEC_PAD)
    before = jnp.asarray(((rs[:, None] % DEC_PAD == rs[None, :] % DEC_PAD)
                          & (rs[None, :] // DEC_PAD < rs[:, None] // DEC_PAD)).astype(np.float32), dtype=BF16)

    blk = lambda rows, w: pl.BlockSpec((1, rows, w), lambda b, pt: (b, 0, 0))
    const = lambda shape: pl.BlockSpec(shape, lambda b, pt: (0,) * len(shape))
    any_spec = pl.BlockSpec(memory_space=pl.ANY)
    grid_spec = pltpu.PrefetchScalarGridSpec(
        num_scalar_prefetch=1,
        grid=(nb,),
        in_specs=[blk(ROWS_I, HEAD_DIM_IDX), blk(ROWS_I, HEAD_DIM_IDX), blk(DEC_PAD, LANES), blk(DEC_PAD, WIDTH_A),
                  blk(DEC_PAD, HEAD_DIM_IDX), blk(DEC_PAD, WIDTH_A), blk(DEC_PAD, WIDTH_A),
                  const((PAGE_SIZE, PAGE_SIZE)), const((N_SEL * DEC_PAD, N_SEL * DEC_PAD)),
                  any_spec, any_spec, any_spec],
        out_specs=blk(DEC_PAD, WIDTH_A),
        scratch_shapes=[pltpu.VMEM((2, N_PAGES, HEAD_DIM_IDX, PAGE_SIZE), F32),
                        pltpu.VMEM((PAGE_SIZE, HEAD_DIM_IDX), F32),
                        pltpu.VMEM((2, PAGE_SIZE, WIDTH_A), F32),
                        pltpu.VMEM((NBUF, N_HEADS_A, HEAD_DIM_A, PAGE_SIZE), F32),
                        pltpu.VMEM((NBUF, N_HEADS_A, HEAD_DIM_A, PAGE_SIZE), F32),
                        pltpu.VMEM((N_SEL, DEC_PAD, PAGE_SIZE), jnp.int32),
                        pltpu.VMEM((N_SEL, DEC_PAD, PAGE_SIZE), F32),
                        pltpu.SemaphoreType.DMA((2 + 2 * NBUF,))],
    )
    return pl.pallas_call(
        _attn_sample_kernel,
        grid_spec=grid_spec,
        out_shape=jax.ShapeDtypeStruct((nb, DEC_PAD, WIDTH_A), F32),
        compiler_params=pltpu.CompilerParams(dimension_semantics=("arbitrary",),
                                             vmem_limit_bytes=52 * 1024 * 1024),
        name="attn_sample",
    )(page_table.reshape(-1), qih, qil, p["wi"], p["qa"], p["ki32"], p["k32"], p["v32"], tri, before,
      cache_kidx_t, cache_k_t, cache_v_t)


def _retention_kernel(q_ref, k_ref, v_ref, s0_ref, dmat_ref, qdec_ref, kdec_ref, gc_ref, bd_ref, hm_ref,
                      ret_ref, sout_ref, state_ref):
    c = pl.program_id(1)

    @pl.when(c == 0)
    def _():
        state_ref[...] = s0_ref[...]

    rows = q_ref.shape[1]
    lane = lax.broadcasted_iota(jnp.int32, (rows, LANES), 1)
    for i in range(q_ref.shape[0]):
        q = q_ref[i]
        k = k_ref[i]
        v = v_ref[i]
        vk = (v.astype(F32) * kdec_ref[...]).astype(BF16)
        for g in range(2):
            gs = slice(256 * g, 256 * (g + 1))
            qg = q[:, gs]
            kg = k[:, gs]
            sg = state_ref[i, g]
            cross = _dot(qg, sg.astype(BF16)) * qdec_ref[:, gs]
            for j in range(2):
                pair = 2 * g + j
                vp = v[:, LANES * pair:LANES * (pair + 1)]
                halves = []
                for e in range(2):
                    hl = 2 * j + e
                    qz = qg * hm_ref[hl]
                    sc = _dot_nt(qz, kg) * dmat_ref[4 * g + hl]
                    halves.append(_dot(sc.astype(BF16), vp))
                inner = jnp.where(lane < DK_R, halves[0], halves[1])
                ret_ref[i, :, LANES * pair:LANES * (pair + 1)] = inner + cross[:, LANES * j:LANES * (j + 1)]
            kv = _dot_tn(kg, vk[:, gs])
            state_ref[i, g] = sg * gc_ref[:, gs] + kv * bd_ref[...]

    @pl.when(c == pl.num_programs(1) - 1)
    def _():
        sout_ref[...] = state_ref[...]


def _retention(q, k, v, state0, tables, chunk, bt):
    nb, rows, _ = q.shape
    nchunk = rows // chunk
    dmat, qdec, kdec, gc, bd, hm = tables
    tile = pl.BlockSpec((bt, chunk, WIDTH_R), lambda b, c: (b, c, 0))
    st = pl.BlockSpec((bt, 2, 256, 256), lambda b, c: (b, 0, 0, 0))
    return pl.pallas_call(
        _retention_kernel,
        grid=(nb // bt, nchunk),
        in_specs=[tile, tile, tile, st,
                  _const_spec((N_HEADS_R, chunk, chunk)), _const_spec((chunk, WIDTH_R)), _const_spec((chunk, WIDTH_R)),
                  _const_spec((1, WIDTH_R)), _const_spec((256, 256)), _const_spec((4, 1, 256))],
        out_specs=[tile, st],
        out_shape=[jax.ShapeDtypeStruct((nb, rows, WIDTH_R), F32),
                   jax.ShapeDtypeStruct((nb, 2, 256, 256), F32)],
        scratch_shapes=[pltpu.VMEM((bt, 2, 256, 256), F32)],
        compiler_params=pltpu.CompilerParams(dimension_semantics=("parallel", "arbitrary")),
        name="retention",
    )(q, k, v, state0, dmat, qdec, kdec, gc, bd, hm)


def _retention_tables(chunk, n_real):
    lg = jnp.log1p(-jnp.exp2(-5.0 - jnp.arange(N_HEADS_R, dtype=F32)))
    i = jnp.arange(chunk, dtype=F32)
    real = jnp.arange(chunk) < n_real
    diff = i[:, None] - i[None, :]
    dmat = jnp.where(diff >= 0, jnp.exp(lg[:, None, None] * jnp.maximum(diff, 0.0)), 0.0)
    dmat = jnp.where(real[None, None, :], dmat, 0.0)
    q_decay = jnp.exp(lg[None, :] * (i[:, None] + 1.0))
    k_decay = jnp.where(real[:, None], jnp.exp(lg[None, :] * (n_real - 1.0 - i)[:, None]), 0.0)
    per_lane = lambda a: jnp.repeat(a, WIDTH_R // N_HEADS_R, axis=-1)
    gc = per_lane(jnp.exp(lg * n_real)[None, :])
    r = np.arange(256)
    bd = ((r[:, None] % 128) // 32 == (r[None, :] // 64)).astype(np.float32)
    hm = np.stack([((r % 128) // 32 == hl) for hl in range(4)]).astype(np.float32).reshape(4, 1, 256)
    return (dmat, per_lane(q_decay), per_lane(k_decay), gc, jnp.asarray(bd), jnp.asarray(hm, dtype=BF16))


def _merge_kernel(x_ref, gate_ref, ya_ref, sza_ref, ret_ref, szr_ref, sga_ref, sgr_ref,
                  avg_ref, gret_ref, wpa_ref, wpr_ref, wout_ref, gfin_ref, y_ref):
    ret = ret_ref[0]
    avg = avg_ref[...]
    r1, r2 = _split2(ret)
    dev = ret - (_dot(r2, avg) + _dot(r1, avg))
    e1, e2 = _split2(dev * dev)
    var = _dot(e2, avg) + _dot(e1, avg)
    yr = (dev * lax.rsqrt(var + EPS)) * gret_ref[...] * szr_ref[0].astype(F32)
    ya = (ya_ref[0].astype(F32) * sza_ref[0].astype(F32)).astype(BF16)
    merged = (sga_ref[0].astype(F32) * _dot(ya, wpa_ref[...])
              + sgr_ref[0].astype(F32) * _dot(yr.astype(BF16), wpr_ref[...]))
    xo = x_ref[0] + gate_ref[0] * _dot(merged.astype(BF16), wout_ref[...])
    r = lax.rsqrt(jnp.mean(xo * xo, axis=-1, keepdims=True) + EPS)
    y_ref[0] = (xo * r) * gfin_ref[...]


def _merge(x3, gate3, ya, ret, p, avg, g_ret, w_pa, w_pr, w_out, g_final, tm):
    nb, rows, _ = x3.shape
    srows = gate3.shape[1]
    row = lambda w: pl.BlockSpec((1, tm, w), lambda b, i: (b, i, 0))
    g_spec = pl.BlockSpec((1, 1 if srows == 1 else tm, D_MODEL),
                          (lambda b, i: (b, 0, 0)) if srows == 1 else (lambda b, i: (b, i, 0)))
    return pl.pallas_call(
        _merge_kernel,
        grid=(nb, rows // tm),
        in_specs=[row(D_MODEL), g_spec, row(WIDTH_A), row(WIDTH_A), row(WIDTH_R), row(WIDTH_R), row(D_MODEL),
                  row(D_MODEL),
                  _const_spec((WIDTH_R, WIDTH_R)), _const_spec((1, WIDTH_R)),
                  _const_spec((WIDTH_A, D_MODEL)), _const_spec((WIDTH_R, D_MODEL)), _const_spec((D_MODEL, D_MODEL)),
                  _const_spec((1, D_MODEL))],
        out_specs=row(D_MODEL),
        out_shape=jax.ShapeDtypeStruct((nb, rows, D_MODEL), F32),
        compiler_params=pltpu.CompilerParams(dimension_semantics=("parallel", "parallel"),
                                             vmem_limit_bytes=48 * 1024 * 1024),
        name="merge",
    )(x3, gate3, ya, p["sza"], ret, p["szr"], p["sga"], p["sgr"], avg, g_ret.reshape(1, -1),
      w_pa, w_pr, w_out, g_final.reshape(1, -1))


def _group_rows(w):
    d = w.shape[1]
    return w.reshape(2, 4, 2, 32, d).transpose(0, 2, 1, 3, 4).reshape(512, d)


def _prep_weights(w_in):
    wt = w_in.T
    s = lambda o, n: wt[o:o + n]
    main = jnp.concatenate([
        s(_O_QA, 512) * (HEAD_DIM_A ** -0.5 * LOG2E),
        s(_O_KA, 512), s(_O_VA, 512), s(_O_ZA, 512),
        _group_rows(s(_O_QR, 512)), _group_rows(s(_O_KR, 512)) * DK_R ** -0.5, s(_O_VR, 512), s(_O_ZR, 512),
        s(_O_GA, 1024), s(_O_GR, 1024)], axis=0).astype(BF16)
    qi = jnp.broadcast_to(s(_O_QI, 256).reshape(N_HEADS_IDX, 1, HEAD_DIM_IDX, D_MODEL),
                          (N_HEADS_IDX, 2, HEAD_DIM_IDX, D_MODEL)).reshape(512, D_MODEL)
    ki = s(_O_KI, 64)
    wi = s(_O_WI, N_HEADS_IDX) * (N_HEADS_IDX ** -0.5 * HEAD_DIM_IDX ** -0.5)
    idx = jnp.concatenate([qi, ki, ki, wi, jnp.zeros((LANES - N_HEADS_IDX, D_MODEL), F32)], axis=0)
    i1 = idx.astype(BF16)
    i2 = (idx - i1.astype(F32)).astype(BF16)
    return main, i1, i2


def _rope_tables(pos):
    half = DK_R // 2
    inv = ROPE_BASE ** (-jnp.arange(half, dtype=F32) / half)
    ang = pos.astype(F32)[:, None] * inv[None, :]
    return jnp.tile(jnp.cos(ang), (1, 4)), jnp.tile(jnp.sin(ang), (1, 4))


def _state_to_groups(state):
    nb = state.shape[0]
    st = state.reshape(nb, 2, 4, 2, 32, 64).transpose(0, 1, 3, 2, 4, 5).reshape(nb, 2, 256, 64)
    head_of_row = (np.arange(256) % 128) // 32
    cols = [jnp.where((head_of_row == hl)[None, None, :, None], st, 0.0) for hl in range(4)]
    return jnp.concatenate(cols, axis=-1)


def _groups_to_state(sg):
    nb = sg.shape[0]
    head_of_row = (np.arange(256) % 128) // 32
    own = sum(jnp.where((head_of_row == hl)[None, None, :, None], sg[..., 64 * hl:64 * (hl + 1)], 0.0)
              for hl in range(4))
    return own.reshape(nb, 2, 2, 4, 32, 64).transpose(0, 1, 3, 2, 4, 5).reshape(nb, N_HEADS_R, DK_R, 64)


def kernel(x_prompt, x_sample, cache_k, cache_v, cache_kidx, state_ret, page_table, c_prompt, c_sample,
           w_ada, b_ada, g_norm, w_in, g_ret, w_pa, w_pr, w_out, g_final):
    nbp, t, _ = x_prompt.shape
    nbs = x_sample.shape[0]
    depth = w_in.shape[0]
    assert depth == 1

    r = np.arange(TQ)
    tri256 = jnp.asarray((r[:, None] < r[None, :]).astype(np.float32), dtype=BF16)
    tri128 = tri256[:PAGE_SIZE, :PAGE_SIZE]
    c5 = np.arange(WIDTH_R)
    avg = jnp.asarray((c5[:, None] // 64 == c5[None, :] // 64).astype(np.float32) / 64.0, dtype=BF16)
    cos_p, sin_p = _rope_tables(jnp.arange(t))
    cos_s, sin_s = _rope_tables(jnp.tile(PAST_LEN + jnp.arange(DEC_PAD), nbs))
    tab_p = _retention_tables(CHUNK_R, CHUNK_R)
    tab_s = _retention_tables(DEC_PAD, DEC_SEQ)

    hp = x_prompt
    hs = jnp.pad(x_sample, ((0, 0), (0, DEC_PAD - DEC_SEQ), (0, 0))).reshape(1, nbs * DEC_PAD, D_MODEL)
    outs_p, outs_s = [], []
    for l in range(depth):
        w_main, wi1, wi2 = _prep_weights(w_in[l])
        wpa, wpr, wout = w_pa[l].astype(BF16), w_pr[l].astype(BF16), w_out[l].astype(BF16)

        c_all = jnp.concatenate([c_prompt, c_sample, jnp.zeros((6, D_MODEL), F32)], axis=0)
        mod = _adaln(c_all, w_ada[l], b_ada[l])
        shift, scale, gate = mod[:, :D_MODEL], mod[:, D_MODEL:2 * D_MODEL], mod[:, 2 * D_MODEL:]
        per_p = lambda a: a[:nbp].reshape(nbp, 1, D_MODEL)
        per_s = lambda a: jnp.broadcast_to(a[nbp:nbp + nbs, None, :], (nbs, DEC_PAD, D_MODEL)).reshape(
            1, nbs * DEC_PAD, D_MODEL)

        pp = _project(hp, per_p(scale), per_p(shift), g_norm[l], cos_p, sin_p, w_main, wi1, wi2, tm=256)
        ya_p = _attn_prompt(pp, tri128.T)
        ret_p, sg_p = _retention(pp["qr"], pp["kr"], pp["vr"], jnp.zeros((nbp, 2, 256, 256), F32), tab_p, CHUNK_R,
                                 bt=nbp)
        hp = _merge(hp, per_p(gate), ya_p, ret_p, pp, avg, g_ret[l], wpa, wpr, wout, g_final, tm=256)

        ps = _project(hs, per_s(scale), per_s(shift), g_norm[l], cos_s, sin_s, w_main, wi1, wi2, tm=nbs * DEC_PAD)
        ps3 = {o[0]: ps[o[0]].reshape(nbs, DEC_PAD, o[2]) for o in _PROJ_OUTS if o[1] == "row"}
        ya_s = _attn_sample(ps3, page_table, cache_kidx[l].transpose(0, 2, 1),
                            cache_k[l].transpose(0, 2, 3, 1), cache_v[l].transpose(0, 2, 3, 1), tri128)
        ret_s, sg_s = _retention(ps3["qr"], ps3["kr"], ps3["vr"], _state_to_groups(state_ret[l].astype(F32)),
                                 tab_s, DEC_PAD, bt=4)
        hs = _merge(hs, per_s(gate), ya_s.reshape(1, nbs * DEC_PAD, WIDTH_A),
                    ret_s.reshape(1, nbs * DEC_PAD, WIDTH_R), ps, avg, g_ret[l], wpa, wpr, wout, g_final,
                    tm=nbs * DEC_PAD)

        heads = lambda a, n: a.reshape(n, -1, N_HEADS_A, HEAD_DIM_A)
        tokens_first = lambda a: a.reshape(nbp, N_HEADS_A, HEAD_DIM_A, t).transpose(0, 3, 1, 2)
        outs_p.append((tokens_first(pp["kt32"]), tokens_first(pp["vt32"]), pp["kit32"].transpose(0, 2, 1),
                       _groups_to_state(sg_p)))
        tok = lambda a: a[:, :DEC_SEQ]
        outs_s.append((heads(tok(ps3["k32"]), nbs), heads(tok(ps3["v32"]), nbs), tok(ps3["ki32"]),
                       _groups_to_state(sg_s)))

    y_prompt = hp
    y_sample = hs.reshape(nbs, DEC_PAD, D_MODEL)[:, :DEC_SEQ]
    stack = lambda items, i: jnp.stack([it[i] for it in items])
    return (y_prompt, y_sample,
            stack(outs_p, 0), stack(outs_p, 1), stack(outs_p, 2), stack(outs_p, 3),
            stack(outs_s, 0), stack(outs_s, 1), stack(outs_s, 2), stack(outs_s, 3))
```

```python
import functools

import numpy as np
import jax
import jax.numpy as jnp
from jax import lax
from jax.experimental import pallas as pl
from jax.experimental.pallas import tpu as pltpu

D_MODEL = 1024
SEQ = 8192
DEC_SEQ = 4
PAST_LEN = 8192
PAGE_SIZE = 128
N_HEADS_A = 8
HEAD_DIM_A = 64
WIDTH_A = 512
N_HEADS_IDX = 4
HEAD_DIM_IDX = 64
TOPK = 256
N_HEADS_R = 8
DK_R = 64
WIDTH_R = 512
CHUNK_R = 128
ROPE_BASE = 10000.0
EPS = 1e-6

LANES = 128
SUBLANES = 8
DEC_PAD = SUBLANES
N_PAGES = PAST_LEN // PAGE_SIZE

_O_QA, _O_KA, _O_VA, _O_ZA = 0, 512, 1024, 1536
_O_QI, _O_KI, _O_WI = 2048, 2304, 2368
_O_QR, _O_KR, _O_VR, _O_ZR = 2372, 2884, 3396, 3908
_O_GA, _O_GR, _N_IN = 4420, 5444, 6468

N_MAIN = 6144
N_IDX = 768

INT_MIN = np.int32(-2 ** 31)
KEY_NEG_INF = np.int32(np.array(0xFF800000, np.uint32).view(np.int32) ^ np.int32(0x7FFFFFFF))
NEG_BIG = -1e30
M_INIT = -1e29
LOG2E = 1.4426950408889634
KSLAB = 128
TQ = 256
PACK16 = 16
I16_MIN = np.int16(-2 ** 15)

F32 = jnp.float32
BF16 = jnp.bfloat16


def _dot(a, b):
    return jnp.dot(a, b, preferred_element_type=F32)


def _dot_nt(a, b):
    return lax.dot_general(a, b, (((1,), (1,)), ((), ())), preferred_element_type=F32)


def _dot_tn(a, b):
    return lax.dot_general(a, b, (((0,), (0,)), ((), ())), preferred_element_type=F32)


def _split2(x):
    hi = x.astype(BF16)
    lo = (x - hi.astype(F32)).astype(BF16)
    return hi, lo


def _split3(x):
    hi = x.astype(BF16)
    r = x - hi.astype(F32)
    mid = r.astype(BF16)
    lo = (r - mid.astype(F32)).astype(BF16)
    return hi, mid, lo


def _sort_key(score):
    bits = pltpu.bitcast(score, jnp.int32)
    return bits ^ ((bits >> 31) & np.int32(0x7FFFFFFF))


def _const_spec(shape):
    nd = len(shape)
    return pl.BlockSpec(shape, lambda *_: (0,) * nd, pipeline_mode=pl.Buffered(1))


def _adaln_kernel(c_ref, w_ref, b_ref, o_ref):
    c = c_ref[...]
    a1, a2, a3 = _split3(c * jax.nn.sigmoid(c))
    w1, w2, w3 = _split3(w_ref[...])
    small = _dot(a1, w3) + _dot(a2, w2) + _dot(a3, w1)
    mid = _dot(a1, w2) + _dot(a2, w1)
    o_ref[...] = (small + mid) + _dot(a1, w1) + b_ref[...]


def _adaln(c_all, w_ada, b_ada):
    rows = c_all.shape[0]
    tn = 512
    return pl.pallas_call(
        _adaln_kernel,
        grid=(3 * D_MODEL // tn,),
        in_specs=[pl.BlockSpec((rows, D_MODEL), lambda j: (0, 0)),
                  pl.BlockSpec((D_MODEL, tn), lambda j: (0, j)),
                  pl.BlockSpec((1, tn), lambda j: (0, j))],
        out_specs=pl.BlockSpec((rows, tn), lambda j: (0, j)),
        out_shape=jax.ShapeDtypeStruct((rows, 3 * D_MODEL), F32),
        name="adaln",
    )(c_all, w_ada, b_ada.reshape(1, -1))


_PROJ_OUTS = (
    ("qa", "row", 512, BF16), ("k32", "row", 512, F32), ("v32", "row", 512, F32), ("kt32", "col", 512, F32),
    ("vt32", "col", 512, F32), ("k16", "row", 512, BF16), ("vt", "slab", 512, BF16),
    ("sza", "row", 512, BF16), ("qcat", "row", 1024, BF16), ("kcat", "row", 256, BF16),
    ("ki32", "row", 64, F32), ("kit32", "col", 64, F32), ("wi", "row", 128, F32), ("wit", "slab", SUBLANES, F32),
    ("qr", "row", 512, BF16), ("kr", "row", 512, BF16), ("vr", "row", 512, BF16), ("szr", "row", 512, BF16),
    ("sga", "row", 1024, BF16), ("sgr", "row", 1024, BF16),
)
_PROJ_SLAB = {"vt": KSLAB, "wit": TQ}


def _proj_kernel(names, x_ref, scale_ref, shift_ref, gn_ref, cos_ref, sin_ref, wm_ref, wi1_ref, wi2_ref, *out_refs):
    o = dict(zip(names, out_refs))

    def put(name, value):
        if name in o:
            o[name][0] = value()

    x = x_ref[0]
    r = lax.rsqrt(jnp.mean(x * x, axis=-1, keepdims=True) + EPS)
    h = (x * r) * gn_ref[...] * (1.0 + scale_ref[0]) + shift_ref[0]
    h1, h2 = _split2(h)

    def main(g):
        return _dot_nt(h1, wm_ref[g * 512:(g + 1) * 512, :])

    o["qa"][0] = main(0).astype(BF16)
    u = main(1)
    put("k32", lambda: u)
    put("kt32", lambda: u.T)
    put("k16", lambda: u.astype(BF16))
    u = main(2)
    put("v32", lambda: u)
    if "vt32" in o:
        ut = u.T
        o["vt32"][0] = ut
        for sl in range(u.shape[0] // KSLAB):
            o["vt"][0, sl] = ut[:, KSLAB * sl:KSLAB * (sl + 1)].astype(BF16)
    u = main(3)
    o["sza"][0] = (u * jax.nn.sigmoid(u)).astype(BF16)

    cos = cos_ref[...]
    sin = sin_ref[...]
    for g, o_ref in ((4, o["qr"]), (5, o["kr"])):
        u = main(g)
        for grp in range(2):
            x1 = u[:, 256 * grp:256 * grp + 128]
            x2 = u[:, 256 * grp + 128:256 * grp + 256]
            o_ref[0, :, 256 * grp:256 * grp + 128] = (x1 * cos - x2 * sin).astype(BF16)
            o_ref[0, :, 256 * grp + 128:256 * grp + 256] = (x1 * sin + x2 * cos).astype(BF16)
    o["vr"][0] = main(6).astype(BF16)
    u = main(7)
    o["szr"][0] = (u * jax.nn.sigmoid(u)).astype(BF16)
    for j in range(2):
        o["sga"][0, :, 512 * j:512 * (j + 1)] = jax.nn.sigmoid(main(8 + j)).astype(BF16)
        o["sgr"][0, :, 512 * j:512 * (j + 1)] = jax.nn.sigmoid(main(10 + j)).astype(BF16)

    w1 = wi1_ref[...]
    w2 = wi2_ref[...]
    ui = (_dot_nt(h2, w2) + _dot_nt(h2, w1) + _dot_nt(h1, w2)) + _dot_nt(h1, w1)
    qd = ui[:, :512]
    q_hi, q_lo = _split2(qd)
    for hh in range(N_HEADS_IDX):
        o["qcat"][0, :, 256 * hh:256 * hh + 128] = q_hi[:, 128 * hh:128 * hh + 128]
        o["qcat"][0, :, 256 * hh + 128:256 * hh + 256] = q_lo[:, 128 * hh:128 * hh + 128]
    kd = ui[:, 512:640]
    if "kcat" in o:
        k_hi, k_lo = _split2(kd)
        lane = lax.broadcasted_iota(jnp.int32, kd.shape, 1)
        sel = jnp.where(lane < HEAD_DIM_IDX, k_hi, k_lo)
        o["kcat"][0, :, 0:128] = sel
        o["kcat"][0, :, 128:256] = sel
    put("ki32", lambda: kd[:, :HEAD_DIM_IDX])
    put("kit32", lambda: kd.T[0:HEAD_DIM_IDX, :])
    put("wi", lambda: ui[:, 640:768])
    if "wit" in o:
        wt = ui[:, 640:768].T[0:SUBLANES, :]
        for sl in range(wt.shape[1] // TQ):
            o["wit"][0, sl] = wt[:, TQ * sl:TQ * (sl + 1)]


def _project(x3, scale3, shift3, g_norm, cos_t, sin_t, w_main, wi1, wi2, tm, skip):
    nb, rows, _ = x3.shape
    srows = scale3.shape[1]
    stile = 1 if srows == 1 else tm
    grid = (nb, rows // tm)
    row_spec = lambda w: pl.BlockSpec((1, tm, w), lambda b, i: (b, i, 0))
    s_spec = pl.BlockSpec((1, stile, D_MODEL), (lambda b, i: (b, 0, 0)) if srows == 1 else (lambda b, i: (b, i, 0)))

    def out(name, layout, dim):
        if layout == "row":
            return (nb, rows, dim), pl.BlockSpec((1, tm, dim), lambda b, i: (b, i, 0))
        if layout == "col":
            return (nb, dim, rows), pl.BlockSpec((1, dim, tm), lambda b, i: (b, 0, i))
        w = _PROJ_SLAB[name]
        return (nb, rows // w, dim, w), pl.BlockSpec((1, tm // w, dim, w), lambda b, i: (b, i, 0, 0))

    wanted = [o for o in _PROJ_OUTS if o[0] not in skip]
    outs_meta = [out(n, lay, dim) for n, lay, dim, _ in wanted]
    in_specs = [row_spec(D_MODEL), s_spec, s_spec, _const_spec((1, D_MODEL)),
                pl.BlockSpec((tm, LANES), lambda b, i: (i, 0)), pl.BlockSpec((tm, LANES), lambda b, i: (i, 0)),
                _const_spec((N_MAIN, D_MODEL)), _const_spec((N_IDX, D_MODEL)), _const_spec((N_IDX, D_MODEL))]
    outs = pl.pallas_call(
        functools.partial(_proj_kernel, tuple(o[0] for o in wanted)),
        grid=grid,
        in_specs=in_specs,
        out_specs=[spec for _, spec in outs_meta],
        out_shape=[jax.ShapeDtypeStruct(shape, o[3]) for (shape, _), o in zip(outs_meta, wanted)],
        compiler_params=pltpu.CompilerParams(dimension_semantics=("parallel", "parallel"),
                                             vmem_limit_bytes=52 * 1024 * 1024),
        name="proj",
    )(x3, scale3, shift3, g_norm.reshape(1, -1), cos_t, sin_t, w_main, wi1, wi2)
    return {o[0]: a for o, a in zip(wanted, outs)}


def _kth_largest_key(count_ge, shape, k=float(TOPK), nbits=32):
    lowest = -(1 << (nbits - 1))
    prefix = jnp.full(shape, lowest, jnp.int32)
    for b in range(nbits):
        bit = np.int32(lowest if b == 0 else 1 << (nbits - 1 - b))
        cand = prefix ^ bit
        prefix = jnp.where(count_ge(cand) >= k, cand, prefix)
    return prefix


def _kth_largest_key_radix4(count_ge, shape):
    prefix = jnp.full(shape, INT_MIN, jnp.int32)
    for shift in range(30, -1, -2):
        cands = [prefix + np.uint32(j << shift).astype(np.int32) for j in (1, 2, 3)]
        hits = [count_ge(c) >= float(TOPK) for c in cands]
        for c, hit in zip(cands, hits):
            prefix = jnp.where(hit, c, prefix)
    return prefix


def _attn_prompt_kernel(qcat_ref, wit_ref, qa_ref, kcat_ref, k_ref, vt_ref, tri_ref, o_ref,
                        keys_ref, hi_ref, lo_ref, qz_ref, m_ref, acc_ref):
    qi = pl.program_id(1)
    nch = qi + 1
    wt = wit_ref[0, 0]

    def rows(ref, c, n):
        return ref[0, pl.ds(pl.multiple_of(c * n, n), n), :]

    def score_chunk(c, diagonal, pair, half):
        kc = rows(kcat_ref, c, TQ)
        acc = jnp.zeros((TQ, TQ), F32)
        for hh in range(N_HEADS_IDX):
            s = _dot_nt(kc, qcat_ref[0, :, 256 * hh:256 * (hh + 1)])
            acc = acc + jnp.maximum(s, 0.0) * wt[hh:hh + 1, :]
        if diagonal:
            key = lax.broadcasted_iota(jnp.int32, (TQ, TQ), 0)
            qry = lax.broadcasted_iota(jnp.int32, (TQ, TQ), 1)
            acc = jnp.where(key <= qry, acc, -jnp.inf)
        kk = _sort_key(acc)
        keys_ref[2 * c] = kk[:KSLAB]
        keys_ref[2 * c + 1] = kk[KSLAB:]
        hi_ref[pair, TQ * half:TQ * (half + 1), :] = (kk >> 16).astype(jnp.int16)
        lo_ref[pair, TQ * half:TQ * (half + 1), :] = (((kk ^ np.int32(0x8000)) << 16) >> 16).astype(jnp.int16)

    def score_pair(pair, carry):
        score_chunk(2 * pair, False, pair, 0)
        score_chunk(2 * pair + 1, False, pair, 1)
        return carry

    lax.fori_loop(0, qi // 2, score_pair, 0)
    last = qi // 2
    npairs = last + 1

    @pl.when(qi % 2 == 0)
    def _():
        score_chunk(qi, True, last, 0)
        never = jnp.full((TQ, TQ), I16_MIN, jnp.int16)
        hi_ref[last, TQ:, :] = never
        lo_ref[last, TQ:, :] = never

    @pl.when(qi % 2 == 1)
    def _():
        score_chunk(qi - 1, False, last, 0)
        score_chunk(qi, True, last, 1)

    def count16(ref, pred):
        def body(pair, acc):
            m = jnp.where(pred(ref[pair]), jnp.int16(1), jnp.int16(0))
            parts = [m[PACK16 * i:PACK16 * (i + 1)] for i in range(2 * TQ // PACK16)]
            while len(parts) > 1:
                parts = [parts[i] + parts[i + 1] for i in range(0, len(parts), 2)]
            return acc + parts[0].astype(jnp.int32)
        acc = lax.fori_loop(0, npairs, body, jnp.zeros((PACK16, TQ), jnp.int32))
        return jnp.sum(acc.astype(F32), axis=0, keepdims=True)

    as16 = lambda v: v.astype(jnp.int16)
    hi_thr = _kth_largest_key(lambda cand: count16(hi_ref, lambda x: x >= as16(cand)), (1, TQ), nbits=16)
    above = count16(hi_ref, lambda x: x > as16(hi_thr))

    def keep_tied_lows(pair, carry):
        lo_ref[pair] = jnp.where(hi_ref[pair] == as16(hi_thr), lo_ref[pair], I16_MIN)
        return carry

    lax.fori_loop(0, npairs, keep_tied_lows, 0)
    lo_thr = _kth_largest_key(lambda cand: count16(lo_ref, lambda x: x >= as16(cand)), (1, TQ),
                              k=float(TOPK) - above, nbits=16)
    thr = (hi_thr << 16) | ((lo_thr + 2 ** 15) & np.int32(0xFFFF))
    need = float(TOPK) - above - count16(lo_ref, lambda x: x > as16(lo_thr))

    lane = lax.broadcasted_iota(jnp.int32, (TQ, LANES), 1)
    for h in range(N_HEADS_A):
        pair = qa_ref[0, :, LANES * (h // 2):LANES * (h // 2 + 1)]
        mine = (lane < HEAD_DIM_A) if h % 2 == 0 else (lane >= HEAD_DIM_A)
        qz_ref[h] = jnp.where(mine, pair, jnp.zeros_like(pair))
    m_ref[...] = jnp.full(m_ref.shape, M_INIT, F32)
    acc_ref[...] = jnp.zeros(acc_ref.shape, F32)
    dim16 = lax.broadcasted_iota(jnp.int32, (LANES, KSLAB), 0)

    def attn_slab(sl, ties_before):
        kk = keys_ref[sl]
        eq = kk == thr
        eqf = jnp.where(eq, 1.0, 0.0)
        rank = _dot(tri_ref[...], eqf.astype(BF16)) + ties_before
        sel = ((kk > thr) | (eq & (rank < need))) & (kk != KEY_NEG_INF)
        kc = rows(k_ref, sl, KSLAB)
        vts = vt_ref[0, sl]
        for h in range(N_HEADS_A):
            lo = LANES * (h // 2)
            s = jnp.where(sel, _dot_nt(kc[:, lo:lo + LANES], qz_ref[h]), NEG_BIG)
            m_old = m_ref[h]
            m_new = jnp.maximum(m_old, jnp.max(s, axis=0, keepdims=True))
            p = jnp.exp2(s - m_new)
            mine = (dim16 < HEAD_DIM_A) if h % 2 == 0 else (dim16 >= HEAD_DIM_A)
            v1 = jnp.where(mine, vts[lo:lo + LANES, :], jnp.ones((), BF16))
            acc_ref[h] = jnp.exp2(m_old - m_new) * acc_ref[h] + _dot(v1, p.astype(BF16))
            m_ref[h] = m_new
        return ties_before + jnp.sum(eqf, axis=0, keepdims=True)

    def attn_chunk(c, ties_before):
        return attn_slab(2 * c + 1, attn_slab(2 * c, ties_before))

    def attn_two_chunks(i, ties_before):
        return attn_chunk(2 * i + 1, attn_chunk(2 * i, ties_before))

    ties = lax.fori_loop(0, nch // 2, attn_two_chunks, jnp.zeros((1, TQ), F32))

    @pl.when(nch % 2 == 1)
    def _():
        attn_chunk(nch - 1, ties)

    dim = lax.broadcasted_iota(jnp.int32, (LANES, TQ), 0)
    for j in range(N_HEADS_A // 2):
        a0 = acc_ref[2 * j]
        a1 = acc_ref[2 * j + 1]
        even = a0 / a0[HEAD_DIM_A:HEAD_DIM_A + 1]
        odd = a1 / a1[0:1]
        o_ref[0, :, LANES * j:LANES * (j + 1)] = jnp.where(dim < HEAD_DIM_A, even, odd).T.astype(BF16)


def _attn_prompt(p, tri):
    nb, t, _ = p["qa"].shape
    nq = t // TQ
    tile = lambda w: pl.BlockSpec((1, TQ, w), lambda b, i: (b, i, 0))
    full = lambda w: pl.BlockSpec((1, t, w), lambda b, i: (b, 0, 0), pipeline_mode=pl.Buffered(1))
    return pl.pallas_call(
        _attn_prompt_kernel,
        grid=(nb, nq),
        in_specs=[tile(1024), pl.BlockSpec((1, 1, SUBLANES, TQ), lambda b, i: (b, i, 0, 0)),
                  tile(WIDTH_A), full(256), full(WIDTH_A),
                  pl.BlockSpec((1, t // KSLAB, WIDTH_A, KSLAB), lambda b, i: (b, 0, 0, 0),
                               pipeline_mode=pl.Buffered(1)),
                  _const_spec((KSLAB, KSLAB))],
        out_specs=tile(WIDTH_A),
        out_shape=jax.ShapeDtypeStruct((nb, t, WIDTH_A), BF16),
        scratch_shapes=[pltpu.VMEM((t // KSLAB, KSLAB, TQ), jnp.int32),
                        pltpu.VMEM((nq // 2, 2 * TQ, TQ), jnp.int16),
                        pltpu.VMEM((nq // 2, 2 * TQ, TQ), jnp.int16),
                        pltpu.VMEM((N_HEADS_A, TQ, LANES), BF16),
                        pltpu.VMEM((N_HEADS_A, 1, TQ), F32),
                        pltpu.VMEM((N_HEADS_A, LANES, TQ), F32)],
        compiler_params=pltpu.CompilerParams(dimension_semantics=("parallel", "arbitrary"),
                                             vmem_limit_bytes=52 * 1024 * 1024),
        name="attn_prompt",
    )(p["qcat"], p["wit"], p["qa"], p["kcat"], p["k16"], p["vt"], tri)


N_CH_S = N_PAGES + 1
ROWS_A = N_HEADS_A * DEC_PAD
ROWS_I = N_HEADS_IDX * DEC_PAD
NBUF = N_PAGES
N_SEL = N_CH_S + 1
SCORE_PAGES = 8
ATT_PAGES = 8


def _attn_sample_kernel(pt_ref, qih_ref, qil_ref, wi_ref, qa_ref, kin_ref, kn_ref, vn_ref,
                        tri_ref, before_ref, ckidx_ref, ck_ref, cv_ref, o_ref,
                        kidx_buf, knew_buf, kvnew_buf, k_ring, v_ring, keys_ref, selc_ref, sems):
    b = pl.program_id(0)
    nb = pl.num_programs(0)
    cur = b % 2

    def kidx_copy(seq, pg, half):
        return pltpu.make_async_copy(ckidx_ref.at[pt_ref[seq * N_PAGES + pg]], kidx_buf.at[half, pg], sems.at[half])

    def kv_copies(g):
        slot = g % NBUF
        phys = pt_ref[g]
        return (pltpu.make_async_copy(ck_ref.at[phys], k_ring.at[slot], sems.at[2 + slot]),
                pltpu.make_async_copy(cv_ref.at[phys], v_ring.at[slot], sems.at[2 + NBUF + slot]))

    def for_pages(fn):
        def body(pg, carry):
            fn(pg)
            return carry
        lax.fori_loop(0, N_PAGES, body, 0)

    @pl.when(b == 0)
    def _():
        for_pages(lambda pg: kidx_copy(0, pg, 0).start())
        def first_pages(g, carry):
            for cp in kv_copies(g):
                cp.start()
            return carry

        lax.fori_loop(0, NBUF, first_pages, 0)
        knew_buf[...] = jnp.zeros(knew_buf.shape, F32)
        kvnew_buf[...] = jnp.zeros(kvnew_buf.shape, F32)
        keys_ref[N_CH_S] = jnp.full((DEC_PAD, PAGE_SIZE), KEY_NEG_INF, jnp.int32)

    knew_buf[0:DEC_PAD, :] = kin_ref[0]
    kvnew_buf[0, 0:DEC_PAD, :] = kn_ref[0]
    kvnew_buf[1, 0:DEC_PAD, :] = vn_ref[0]
    for_pages(lambda pg: kidx_copy(b, pg, cur).wait())

    @pl.when(b + 1 < nb)
    def _():
        for_pages(lambda pg: kidx_copy(b + 1, pg, 1 - cur).start())

    w = wi_ref[0]
    q_hi = qih_ref[0]
    q_lo = qil_ref[0]
    trow = lax.broadcasted_iota(jnp.int32, (DEC_PAD, PAGE_SIZE), 0)
    tcol = lax.broadcasted_iota(jnp.int32, (DEC_PAD, PAGE_SIZE), 1)

    def score_pages(kt, first, new_page):
        n = kt.shape[1] // PAGE_SIZE
        k_hi, k_lo = _split2(kt)
        s = (_dot(q_lo, k_lo) + _dot(q_lo, k_hi) + _dot(q_hi, k_lo)) + _dot(q_hi, k_hi)
        acc = jnp.zeros((DEC_PAD, n * PAGE_SIZE), F32)
        for hh in range(N_HEADS_IDX):
            acc = acc + jnp.maximum(s[DEC_PAD * hh:DEC_PAD * (hh + 1)], 0.0) * w[:, hh:hh + 1]
        if new_page:
            acc = jnp.where((tcol <= trow) & (tcol < DEC_SEQ), acc, -jnp.inf)
        kk = _sort_key(acc)
        for j in range(n):
            keys_ref[first + j] = kk[:, PAGE_SIZE * j:PAGE_SIZE * (j + 1)]

    def score_body(i, carry):
        first = i * SCORE_PAGES
        score_pages(jnp.concatenate([kidx_buf[cur, first + j] for j in range(SCORE_PAGES)], axis=1), first, False)
        return carry

    lax.fori_loop(0, N_PAGES // SCORE_PAGES, score_body, 0)
    score_pages(knew_buf[...].T, N_PAGES, True)

    def count(pred):
        m = jnp.where(pred(keys_ref[...]), 1.0, 0.0)
        return jnp.sum(jnp.sum(m, axis=0), axis=1, keepdims=True)

    thr = _kth_largest_key_radix4(lambda cand: count(lambda kk: kk >= cand[None]), (DEC_PAD, 1))
    need = float(TOPK) - count(lambda kk: kk > thr[None])

    kk = keys_ref[...]
    eq = kk == thr[None]
    eq2 = jnp.where(eq, 1.0, 0.0).reshape(N_SEL * DEC_PAD, PAGE_SIZE).astype(BF16)
    in_page = _dot(eq2, tri_ref[...])
    per_page = _dot(eq2, jnp.ones((PAGE_SIZE, PAGE_SIZE), BF16))
    rank = (in_page + _dot(before_ref[...], per_page.astype(BF16))).reshape(N_SEL, DEC_PAD, PAGE_SIZE)
    sel = ((kk > thr[None]) | (eq & (rank < need[None]))) & (kk != KEY_NEG_INF)
    selc_ref[...] = jnp.where(sel, 1.0, 0.0)

    lane5 = lax.broadcasted_iota(jnp.int32, (DEC_PAD, WIDTH_A), 1)
    qa = qa_ref[0].astype(F32)
    qbd = jnp.concatenate(
        [jnp.where(lane5 // HEAD_DIM_A == h, qa, 0.0) for h in range(N_HEADS_A)], axis=0).astype(BF16)

    def attend(carry, kt, vt, selc):
        m_old, l_old, acc = carry
        valid = jnp.concatenate([selc] * N_HEADS_A, axis=0) > 0.5
        s = jnp.where(valid, _dot(qbd, kt.astype(BF16)), NEG_BIG)
        m_new = jnp.maximum(m_old, jnp.max(s, axis=1, keepdims=True))
        p = jnp.exp2(s - m_new)
        alpha = jnp.exp2(m_old - m_new)
        return (m_new, alpha * l_old + jnp.sum(p, axis=1, keepdims=True),
                alpha * acc + _dot_nt(p.astype(BF16), vt.astype(BF16)))

    def group_body(gi, carry):
        pg0 = gi * ATT_PAGES
        g0 = b * N_PAGES + pg0
        for j in range(ATT_PAGES):
            for cp in kv_copies(g0 + j):
                cp.wait()
        pages = lambda ring: jnp.concatenate(
            [ring[(pg0 + j) % NBUF].reshape(WIDTH_A, PAGE_SIZE) for j in range(ATT_PAGES)], axis=1)
        selc = jnp.concatenate([selc_ref[pg0 + j] for j in range(ATT_PAGES)], axis=1)
        carry = attend(carry, pages(k_ring), pages(v_ring), selc)
        for j in range(ATT_PAGES):
            @pl.when(g0 + j + NBUF < nb * N_PAGES)
            def _():
                for cp in kv_copies(g0 + j + NBUF):
                    cp.start()
        return carry

    carry = lax.fori_loop(0, N_PAGES // ATT_PAGES, group_body,
                          (jnp.full((ROWS_A, 1), M_INIT, F32), jnp.zeros((ROWS_A, 1), F32),
                           jnp.zeros((ROWS_A, WIDTH_A), F32)))
    _, l, acc = attend(carry, kvnew_buf[0].T, kvnew_buf[1].T, selc_ref[N_PAGES])
    o = acc / l
    out = jnp.zeros((DEC_PAD, WIDTH_A), F32)
    for h in range(N_HEADS_A):
        out = jnp.where(lane5 // HEAD_DIM_A == h, o[DEC_PAD * h:DEC_PAD * (h + 1)], out)
    o_ref[0] = out


def _attn_sample(p, page_table, cache_kidx_t, cache_k_t, cache_v_t, tri):
    nb = p["qa"].shape[0]
    heads_first = lambda a, nh: a.reshape(nb, DEC_PAD, nh, -1).transpose(0, 2, 1, 3).reshape(nb, nh * DEC_PAD, -1)
    qparts = p["qcat"].reshape(nb, DEC_PAD, N_HEADS_IDX, 4, HEAD_DIM_IDX)
    qih = heads_first(qparts[:, :, :, 0], N_HEADS_IDX)
    qil = heads_first(qparts[:, :, :, 2], N_HEADS_IDX)
    rs = np.arange(N_SEL * DEC_PAD)
    before = jnp.asarray(((rs[:, None] % DEC_PAD == rs[None, :] % DEC_PAD)
                          & (rs[None, :] // DEC_PAD < rs[:, None] // DEC_PAD)).astype(np.float32), dtype=BF16)

    blk = lambda rows, w: pl.BlockSpec((1, rows, w), lambda b, pt: (b, 0, 0))
    const = lambda shape: pl.BlockSpec(shape, lambda b, pt: (0,) * len(shape))
    any_spec = pl.BlockSpec(memory_space=pl.ANY)
    grid_spec = pltpu.PrefetchScalarGridSpec(
        num_scalar_prefetch=1,
        grid=(nb,),
        in_specs=[blk(ROWS_I, HEAD_DIM_IDX), blk(ROWS_I, HEAD_DIM_IDX), blk(DEC_PAD, LANES), blk(DEC_PAD, WIDTH_A),
                  blk(DEC_PAD, HEAD_DIM_IDX), blk(DEC_PAD, WIDTH_A), blk(DEC_PAD, WIDTH_A),
                  const((PAGE_SIZE, PAGE_SIZE)), const((N_SEL * DEC_PAD, N_SEL * DEC_PAD)),
                  any_spec, any_spec, any_spec],
        out_specs=blk(DEC_PAD, WIDTH_A),
        scratch_shapes=[pltpu.VMEM((2, N_PAGES, HEAD_DIM_IDX, PAGE_SIZE), F32),
                        pltpu.VMEM((PAGE_SIZE, HEAD_DIM_IDX), F32),
                        pltpu.VMEM((2, PAGE_SIZE, WIDTH_A), F32),
                        pltpu.VMEM((NBUF, N_HEADS_A, HEAD_DIM_A, PAGE_SIZE), F32),
                        pltpu.VMEM((NBUF, N_HEADS_A, HEAD_DIM_A, PAGE_SIZE), F32),
                        pltpu.VMEM((N_SEL, DEC_PAD, PAGE_SIZE), jnp.int32),
                        pltpu.VMEM((N_SEL, DEC_PAD, PAGE_SIZE), F32),
                        pltpu.SemaphoreType.DMA((2 + 2 * NBUF,))],
    )
    return pl.pallas_call(
        _attn_sample_kernel,
        grid_spec=grid_spec,
        out_shape=jax.ShapeDtypeStruct((nb, DEC_PAD, WIDTH_A), F32),
        compiler_params=pltpu.CompilerParams(dimension_semantics=("arbitrary",),
                                             vmem_limit_bytes=52 * 1024 * 1024),
        name="attn_sample",
    )(page_table.reshape(-1), qih, qil, p["wi"], p["qa"], p["ki32"], p["k32"], p["v32"], tri, before,
      cache_kidx_t, cache_k_t, cache_v_t)


def _retention_kernel(q_ref, k_ref, v_ref, s0_ref, dmat_ref, qdec_ref, kdec_ref, gc_ref, bd_ref, hm_ref,
                      ret_ref, sout_ref, state_ref):
    c = pl.program_id(1)

    @pl.when(c == 0)
    def _():
        state_ref[...] = s0_ref[...]

    rows = q_ref.shape[1]
    lane = lax.broadcasted_iota(jnp.int32, (rows, LANES), 1)
    for i in range(q_ref.shape[0]):
        q = q_ref[i]
        k = k_ref[i]
        v = v_ref[i]
        vk = (v.astype(F32) * kdec_ref[...]).astype(BF16)
        for g in range(2):
            gs = slice(256 * g, 256 * (g + 1))
            qg = q[:, gs]
            kg = k[:, gs]
            sg = state_ref[i, g]
            cross = _dot(qg, sg.astype(BF16)) * qdec_ref[:, gs]
            for j in range(2):
                pair = 2 * g + j
                vp = v[:, LANES * pair:LANES * (pair + 1)]
                halves = []
                for e in range(2):
                    hl = 2 * j + e
                    qz = qg * hm_ref[hl]
                    sc = _dot_nt(qz, kg) * dmat_ref[4 * g + hl]
                    halves.append(_dot(sc.astype(BF16), vp))
                inner = jnp.where(lane < DK_R, halves[0], halves[1])
                ret_ref[i, :, LANES * pair:LANES * (pair + 1)] = inner + cross[:, LANES * j:LANES * (j + 1)]
            kv = _dot_tn(kg, vk[:, gs])
            state_ref[i, g] = sg * gc_ref[:, gs] + kv * bd_ref[...]

    @pl.when(c == pl.num_programs(1) - 1)
    def _():
        sout_ref[...] = state_ref[...]


def _retention(q, k, v, state0, tables, chunk, bt):
    nb, rows, _ = q.shape
    nchunk = rows // chunk
    dmat, qdec, kdec, gc, bd, hm = tables
    tile = pl.BlockSpec((bt, chunk, WIDTH_R), lambda b, c: (b, c, 0))
    st = pl.BlockSpec((bt, 2, 256, 256), lambda b, c: (b, 0, 0, 0))
    return pl.pallas_call(
        _retention_kernel,
        grid=(nb // bt, nchunk),
        in_specs=[tile, tile, tile, st,
                  _const_spec((N_HEADS_R, chunk, chunk)), _const_spec((chunk, WIDTH_R)), _const_spec((chunk, WIDTH_R)),
                  _const_spec((1, WIDTH_R)), _const_spec((256, 256)), _const_spec((4, 1, 256))],
        out_specs=[tile, st],
        out_shape=[jax.ShapeDtypeStruct((nb, rows, WIDTH_R), F32),
                   jax.ShapeDtypeStruct((nb, 2, 256, 256), F32)],
        scratch_shapes=[pltpu.VMEM((bt, 2, 256, 256), F32)],
        compiler_params=pltpu.CompilerParams(dimension_semantics=("parallel", "arbitrary")),
        name="retention",
    )(q, k, v, state0, dmat, qdec, kdec, gc, bd, hm)


def _retention_tables(chunk, n_real):
    lg = jnp.log1p(-jnp.exp2(-5.0 - jnp.arange(N_HEADS_R, dtype=F32)))
    i = jnp.arange(chunk, dtype=F32)
    real = jnp.arange(chunk) < n_real
    diff = i[:, None] - i[None, :]
    dmat = jnp.where(diff >= 0, jnp.exp(lg[:, None, None] * jnp.maximum(diff, 0.0)), 0.0)
    dmat = jnp.where(real[None, None, :], dmat, 0.0)
    q_decay = jnp.exp(lg[None, :] * (i[:, None] + 1.0))
    k_decay = jnp.where(real[:, None], jnp.exp(lg[None, :] * (n_real - 1.0 - i)[:, None]), 0.0)
    per_lane = lambda a: jnp.repeat(a, WIDTH_R // N_HEADS_R, axis=-1)
    gc = per_lane(jnp.exp(lg * n_real)[None, :])
    r = np.arange(256)
    bd = ((r[:, None] % 128) // 32 == (r[None, :] // 64)).astype(np.float32)
    hm = np.stack([((r % 128) // 32 == hl) for hl in range(4)]).astype(np.float32).reshape(4, 1, 256)
    return (dmat, per_lane(q_decay), per_lane(k_decay), gc, jnp.asarray(bd), jnp.asarray(hm, dtype=BF16))


def _merge_kernel(x_ref, gate_ref, ya_ref, sza_ref, ret_ref, szr_ref, sga_ref, sgr_ref,
                  avg_ref, gret_ref, wpa_ref, wpr_ref, wout_ref, gfin_ref, y_ref):
    ret = ret_ref[0]
    avg = avg_ref[...]
    r1, r2 = _split2(ret)
    dev = ret - (_dot(r2, avg) + _dot(r1, avg))
    e1, e2 = _split2(dev * dev)
    var = _dot(e2, avg) + _dot(e1, avg)
    yr = (dev * lax.rsqrt(var + EPS)) * gret_ref[...] * szr_ref[0].astype(F32)
    ya = (ya_ref[0].astype(F32) * sza_ref[0].astype(F32)).astype(BF16)
    merged = (sga_ref[0].astype(F32) * _dot(ya, wpa_ref[...])
              + sgr_ref[0].astype(F32) * _dot(yr.astype(BF16), wpr_ref[...]))
    xo = x_ref[0] + gate_ref[0] * _dot(merged.astype(BF16), wout_ref[...])
    r = lax.rsqrt(jnp.mean(xo * xo, axis=-1, keepdims=True) + EPS)
    y_ref[0] = (xo * r) * gfin_ref[...]


def _merge(x3, gate3, ya, ret, p, avg, g_ret, w_pa, w_pr, w_out, g_final, tm):
    nb, rows, _ = x3.shape
    srows = gate3.shape[1]
    row = lambda w: pl.BlockSpec((1, tm, w), lambda b, i: (b, i, 0))
    g_spec = pl.BlockSpec((1, 1 if srows == 1 else tm, D_MODEL),
                          (lambda b, i: (b, 0, 0)) if srows == 1 else (lambda b, i: (b, i, 0)))
    return pl.pallas_call(
        _merge_kernel,
        grid=(nb, rows // tm),
        in_specs=[row(D_MODEL), g_spec, row(WIDTH_A), row(WIDTH_A), row(WIDTH_R), row(WIDTH_R), row(D_MODEL),
                  row(D_MODEL),
                  _const_spec((WIDTH_R, WIDTH_R)), _const_spec((1, WIDTH_R)),
                  _const_spec((WIDTH_A, D_MODEL)), _const_spec((WIDTH_R, D_MODEL)), _const_spec((D_MODEL, D_MODEL)),
                  _const_spec((1, D_MODEL))],
        out_specs=row(D_MODEL),
        out_shape=jax.ShapeDtypeStruct((nb, rows, D_MODEL), F32),
        compiler_params=pltpu.CompilerParams(dimension_semantics=("parallel", "parallel"),
                                             vmem_limit_bytes=48 * 1024 * 1024),
        name="merge",
    )(x3, gate3, ya, p["sza"], ret, p["szr"], p["sga"], p["sgr"], avg, g_ret.reshape(1, -1),
      w_pa, w_pr, w_out, g_final.reshape(1, -1))


def _group_rows(w):
    d = w.shape[1]
    return w.reshape(2, 4, 2, 32, d).transpose(0, 2, 1, 3, 4).reshape(512, d)


def _prep_weights(w_in):
    wt = w_in.T
    s = lambda o, n: wt[o:o + n]
    main = jnp.concatenate([
        s(_O_QA, 512) * (HEAD_DIM_A ** -0.5 * LOG2E),
        s(_O_KA, 512), s(_O_VA, 512), s(_O_ZA, 512),
        _group_rows(s(_O_QR, 512)), _group_rows(s(_O_KR, 512)) * DK_R ** -0.5, s(_O_VR, 512), s(_O_ZR, 512),
        s(_O_GA, 1024), s(_O_GR, 1024)], axis=0).astype(BF16)
    qi = jnp.broadcast_to(s(_O_QI, 256).reshape(N_HEADS_IDX, 1, HEAD_DIM_IDX, D_MODEL),
                          (N_HEADS_IDX, 2, HEAD_DIM_IDX, D_MODEL)).reshape(512, D_MODEL)
    ki = s(_O_KI, 64)
    wi = s(_O_WI, N_HEADS_IDX) * (N_HEADS_IDX ** -0.5 * HEAD_DIM_IDX ** -0.5)
    idx = jnp.concatenate([qi, ki, ki, wi, jnp.zeros((LANES - N_HEADS_IDX, D_MODEL), F32)], axis=0)
    i1 = idx.astype(BF16)
    i2 = (idx - i1.astype(F32)).astype(BF16)
    return main, i1, i2


def _rope_tables(pos):
    half = DK_R // 2
    inv = ROPE_BASE ** (-jnp.arange(half, dtype=F32) / half)
    ang = pos.astype(F32)[:, None] * inv[None, :]
    return jnp.tile(jnp.cos(ang), (1, 4)), jnp.tile(jnp.sin(ang), (1, 4))


def _state_to_groups(state):
    nb = state.shape[0]
    st = state.reshape(nb, 2, 4, 2, 32, 64).transpose(0, 1, 3, 2, 4, 5).reshape(nb, 2, 256, 64)
    head_of_row = (np.arange(256) % 128) // 32
    cols = [jnp.where((head_of_row == hl)[None, None, :, None], st, 0.0) for hl in range(4)]
    return jnp.concatenate(cols, axis=-1)


def _groups_to_state(sg):
    nb = sg.shape[0]
    head_of_row = (np.arange(256) % 128) // 32
    own = sum(jnp.where((head_of_row == hl)[None, None, :, None], sg[..., 64 * hl:64 * (hl + 1)], 0.0)
              for hl in range(4))
    return own.reshape(nb, 2, 2, 4, 32, 64).transpose(0, 1, 3, 2, 4, 5).reshape(nb, N_HEADS_R, DK_R, 64)


def kernel(x_prompt, x_sample, cache_k, cache_v, cache_kidx, state_ret, page_table, c_prompt, c_sample,
           w_ada, b_ada, g_norm, w_in, g_ret, w_pa, w_pr, w_out, g_final):
    nbp, t, _ = x_prompt.shape
    nbs = x_sample.shape[0]
    depth = w_in.shape[0]
    assert depth == 1

    r = np.arange(TQ)
    tri256 = jnp.asarray((r[:, None] < r[None, :]).astype(np.float32), dtype=BF16)
    tri128 = tri256[:PAGE_SIZE, :PAGE_SIZE]
    c5 = np.arange(WIDTH_R)
    avg = jnp.asarray((c5[:, None] // 64 == c5[None, :] // 64).astype(np.float32) / 64.0, dtype=BF16)
    cos_p, sin_p = _rope_tables(jnp.arange(t))
    cos_s, sin_s = _rope_tables(jnp.tile(PAST_LEN + jnp.arange(DEC_PAD), nbs))
    tab_p = _retention_tables(CHUNK_R, CHUNK_R)
    tab_s = _retention_tables(DEC_PAD, DEC_SEQ)

    hp = x_prompt
    hs = jnp.pad(x_sample, ((0, 0), (0, DEC_PAD - DEC_SEQ), (0, 0))).reshape(1, nbs * DEC_PAD, D_MODEL)
    outs_p, outs_s = [], []
    for l in range(depth):
        w_main, wi1, wi2 = _prep_weights(w_in[l])
        wpa, wpr, wout = w_pa[l].astype(BF16), w_pr[l].astype(BF16), w_out[l].astype(BF16)

        c_all = jnp.concatenate([c_prompt, c_sample, jnp.zeros((6, D_MODEL), F32)], axis=0)
        mod = _adaln(c_all, w_ada[l], b_ada[l])
        shift, scale, gate = mod[:, :D_MODEL], mod[:, D_MODEL:2 * D_MODEL], mod[:, 2 * D_MODEL:]
        per_p = lambda a: a[:nbp].reshape(nbp, 1, D_MODEL)
        per_s = lambda a: jnp.broadcast_to(a[nbp:nbp + nbs, None, :], (nbs, DEC_PAD, D_MODEL)).reshape(
            1, nbs * DEC_PAD, D_MODEL)

        pp = _project(hp, per_p(scale), per_p(shift), g_norm[l], cos_p, sin_p, w_main, wi1, wi2, tm=512,
                      skip=("k32", "v32", "ki32", "wi"))
        ya_p = _attn_prompt(pp, tri128.T)
        ret_p, sg_p = _retention(pp["qr"], pp["kr"], pp["vr"], jnp.zeros((nbp, 2, 256, 256), F32), tab_p, CHUNK_R,
                                 bt=nbp)
        hp = _merge(hp, per_p(gate), ya_p, ret_p, pp, avg, g_ret[l], wpa, wpr, wout, g_final, tm=256)

        ps = _project(hs, per_s(scale), per_s(shift), g_norm[l], cos_s, sin_s, w_main, wi1, wi2, tm=nbs * DEC_PAD,
                      skip=("kt32", "vt32", "kit32", "vt", "wit", "k16", "kcat"))
        ps3 = {o[0]: ps[o[0]].reshape(nbs, DEC_PAD, o[2]) for o in _PROJ_OUTS if o[0] in ps and o[1] == "row"}
        ya_s = _attn_sample(ps3, page_table, cache_kidx[l].transpose(0, 2, 1),
                            cache_k[l].transpose(0, 2, 3, 1), cache_v[l].transpose(0, 2, 3, 1), tri128)
        ret_s, sg_s = _retention(ps3["qr"], ps3["kr"], ps3["vr"], _state_to_groups(state_ret[l].astype(F32)),
                                 tab_s, DEC_PAD, bt=4)
        hs = _merge(hs, per_s(gate), ya_s.reshape(1, nbs * DEC_PAD, WIDTH_A),
                    ret_s.reshape(1, nbs * DEC_PAD, WIDTH_R), ps, avg, g_ret[l], wpa, wpr, wout, g_final,
                    tm=nbs * DEC_PAD)

        heads = lambda a, n: a.reshape(n, -1, N_HEADS_A, HEAD_DIM_A)
        tokens_first = lambda a: a.reshape(nbp, N_HEADS_A, HEAD_DIM_A, t).transpose(0, 3, 1, 2)
        outs_p.append((tokens_first(pp["kt32"]), tokens_first(pp["vt32"]), pp["kit32"].transpose(0, 2, 1),
                       _groups_to_state(sg_p)))
        tok = lambda a: a[:, :DEC_SEQ]
        outs_s.append((heads(tok(ps3["k32"]), nbs), heads(tok(ps3["v32"]), nbs), tok(ps3["ki32"]),
                       _groups_to_state(sg_s)))

    y_prompt = hp
    y_sample = hs.reshape(nbs, DEC_PAD, D_MODEL)[:, :DEC_SEQ]
    stack = lambda items, i: jnp.stack([it[i] for it in items])
    return (y_prompt, y_sample,
            stack(outs_p, 0), stack(outs_p, 1), stack(outs_p, 2), stack(outs_p, 3),
            stack(outs_s, 0), stack(outs_s, 1), stack(outs_s, 2), stack(outs_s, 3))
```

```python
import functools

import numpy as np
import jax
import jax.numpy as jnp
from jax import lax
from jax.experimental import pallas as pl
from jax.experimental.pallas import tpu as pltpu

D_MODEL = 1024
SEQ = 8192
DEC_SEQ = 4
PAST_LEN = 8192
PAGE_SIZE = 128
N_HEADS_A = 8
HEAD_DIM_A = 64
WIDTH_A = 512
N_HEADS_IDX = 4
HEAD_DIM_IDX = 64
TOPK = 256
N_HEADS_R = 8
DK_R = 64
WIDTH_R = 512
CHUNK_R = 128
ROPE_BASE = 10000.0
EPS = 1e-6

LANES = 128
SUBLANES = 8
DEC_PAD = SUBLANES
N_PAGES = PAST_LEN // PAGE_SIZE

_O_QA, _O_KA, _O_VA, _O_ZA = 0, 512, 1024, 1536
_O_QI, _O_KI, _O_WI = 2048, 2304, 2368
_O_QR, _O_KR, _O_VR, _O_ZR = 2372, 2884, 3396, 3908
_O_GA, _O_GR, _N_IN = 4420, 5444, 6468

N_MAIN = 6144
N_IDX = 768

INT_MIN = np.int32(-2 ** 31)
KEY_NEG_INF = np.int32(np.array(0xFF800000, np.uint32).view(np.int32) ^ np.int32(0x7FFFFFFF))
NEG_BIG = -1e30
M_INIT = -1e29
LOG2E = 1.4426950408889634
KSLAB = 128
TQ = 256
PACK16 = 16
I16_MIN = np.int16(-2 ** 15)

F32 = jnp.float32
BF16 = jnp.bfloat16


def _dot(a, b):
    return jnp.dot(a, b, preferred_element_type=F32)


def _dot_nt(a, b):
    return lax.dot_general(a, b, (((1,), (1,)), ((), ())), preferred_element_type=F32)


def _dot_tn(a, b):
    return lax.dot_general(a, b, (((0,), (0,)), ((), ())), preferred_element_type=F32)


def _split2(x):
    hi = x.astype(BF16)
    lo = (x - hi.astype(F32)).astype(BF16)
    return hi, lo


def _split3(x):
    hi = x.astype(BF16)
    r = x - hi.astype(F32)
    mid = r.astype(BF16)
    lo = (r - mid.astype(F32)).astype(BF16)
    return hi, mid, lo


def _sort_key(score):
    bits = pltpu.bitcast(score, jnp.int32)
    return bits ^ ((bits >> 31) & np.int32(0x7FFFFFFF))


def _const_spec(shape):
    nd = len(shape)
    return pl.BlockSpec(shape, lambda *_: (0,) * nd, pipeline_mode=pl.Buffered(1))


def _adaln_kernel(c_ref, w_ref, b_ref, o_ref):
    c = c_ref[...]
    a1, a2, a3 = _split3(c * jax.nn.sigmoid(c))
    w1, w2, w3 = _split3(w_ref[...])
    small = _dot(a1, w3) + _dot(a2, w2) + _dot(a3, w1)
    mid = _dot(a1, w2) + _dot(a2, w1)
    o_ref[...] = (small + mid) + _dot(a1, w1) + b_ref[...]


def _adaln(c_all, w_ada, b_ada):
    rows = c_all.shape[0]
    tn = 512
    return pl.pallas_call(
        _adaln_kernel,
        grid=(3 * D_MODEL // tn,),
        in_specs=[pl.BlockSpec((rows, D_MODEL), lambda j: (0, 0)),
                  pl.BlockSpec((D_MODEL, tn), lambda j: (0, j)),
                  pl.BlockSpec((1, tn), lambda j: (0, j))],
        out_specs=pl.BlockSpec((rows, tn), lambda j: (0, j)),
        out_shape=jax.ShapeDtypeStruct((rows, 3 * D_MODEL), F32),
        name="adaln",
    )(c_all, w_ada, b_ada.reshape(1, -1))


_PROJ_OUTS = (
    ("qa", "row", 512, BF16), ("k32", "row", 512, F32), ("v32", "row", 512, F32), ("kt32", "col", 512, F32),
    ("vt32", "col", 512, F32), ("k16", "row", 512, BF16), ("vt", "slab", 512, BF16),
    ("sza", "row", 512, BF16), ("qcat", "row", 1024, BF16), ("kcat", "row", 256, BF16),
    ("ki32", "row", 64, F32), ("kit32", "col", 64, F32), ("wi", "row", 128, F32), ("wit", "slab", SUBLANES, F32),
    ("qr", "row", 512, BF16), ("kr", "row", 512, BF16), ("vr", "row", 512, BF16), ("szr", "row", 512, BF16),
    ("sga", "row", 1024, BF16), ("sgr", "row", 1024, BF16),
)
_PROJ_SLAB = {"vt": KSLAB, "wit": TQ}


def _proj_kernel(names, x_ref, scale_ref, shift_ref, gn_ref, cos_ref, sin_ref, wm_ref, wi1_ref, wi2_ref, *out_refs):
    o = dict(zip(names, out_refs))

    def put(name, value):
        if name in o:
            o[name][0] = value()

    x = x_ref[0]
    r = lax.rsqrt(jnp.mean(x * x, axis=-1, keepdims=True) + EPS)
    h = (x * r) * gn_ref[...] * (1.0 + scale_ref[0]) + shift_ref[0]
    h1, h2 = _split2(h)

    def main(g):
        return _dot_nt(h1, wm_ref[g * 512:(g + 1) * 512, :])

    o["qa"][0] = main(0).astype(BF16)
    u = main(1)
    put("k32", lambda: u)
    put("kt32", lambda: u.T)
    put("k16", lambda: u.astype(BF16))
    u = main(2)
    put("v32", lambda: u)
    if "vt32" in o:
        ut = u.T
        o["vt32"][0] = ut
        for sl in range(u.shape[0] // KSLAB):
            o["vt"][0, sl] = ut[:, KSLAB * sl:KSLAB * (sl + 1)].astype(BF16)
    u = main(3)
    o["sza"][0] = (u * jax.nn.sigmoid(u)).astype(BF16)

    cos = cos_ref[...]
    sin = sin_ref[...]
    for g, o_ref in ((4, o["qr"]), (5, o["kr"])):
        u = main(g)
        for grp in range(2):
            x1 = u[:, 256 * grp:256 * grp + 128]
            x2 = u[:, 256 * grp + 128:256 * grp + 256]
            o_ref[0, :, 256 * grp:256 * grp + 128] = (x1 * cos - x2 * sin).astype(BF16)
            o_ref[0, :, 256 * grp + 128:256 * grp + 256] = (x1 * sin + x2 * cos).astype(BF16)
    o["vr"][0] = main(6).astype(BF16)
    u = main(7)
    o["szr"][0] = (u * jax.nn.sigmoid(u)).astype(BF16)
    for j in range(2):
        o["sga"][0, :, 512 * j:512 * (j + 1)] = jax.nn.sigmoid(main(8 + j)).astype(BF16)
        o["sgr"][0, :, 512 * j:512 * (j + 1)] = jax.nn.sigmoid(main(10 + j)).astype(BF16)

    w1 = wi1_ref[...]
    w2 = wi2_ref[...]
    ui = (_dot_nt(h2, w2) + _dot_nt(h2, w1) + _dot_nt(h1, w2)) + _dot_nt(h1, w1)
    qd = ui[:, :512]
    q_hi, q_lo = _split2(qd)
    for hh in range(N_HEADS_IDX):
        o["qcat"][0, :, 256 * hh:256 * hh + 128] = q_hi[:, 128 * hh:128 * hh + 128]
        o["qcat"][0, :, 256 * hh + 128:256 * hh + 256] = q_lo[:, 128 * hh:128 * hh + 128]
    kd = ui[:, 512:640]
    if "kcat" in o:
        k_hi, k_lo = _split2(kd)
        lane = lax.broadcasted_iota(jnp.int32, kd.shape, 1)
        sel = jnp.where(lane < HEAD_DIM_IDX, k_hi, k_lo)
        o["kcat"][0, :, 0:128] = sel
        o["kcat"][0, :, 128:256] = sel
    put("ki32", lambda: kd[:, :HEAD_DIM_IDX])
    put("kit32", lambda: kd.T[0:HEAD_DIM_IDX, :])
    put("wi", lambda: ui[:, 640:768])
    if "wit" in o:
        wt = ui[:, 640:768].T[0:SUBLANES, :]
        for sl in range(wt.shape[1] // TQ):
            o["wit"][0, sl] = wt[:, TQ * sl:TQ * (sl + 1)]


def _project(x3, scale3, shift3, g_norm, cos_t, sin_t, w_main, wi1, wi2, tm, skip):
    nb, rows, _ = x3.shape
    srows = scale3.shape[1]
    stile = 1 if srows == 1 else tm
    grid = (nb, rows // tm)
    row_spec = lambda w: pl.BlockSpec((1, tm, w), lambda b, i: (b, i, 0))
    s_spec = pl.BlockSpec((1, stile, D_MODEL), (lambda b, i: (b, 0, 0)) if srows == 1 else (lambda b, i: (b, i, 0)))

    def out(name, layout, dim):
        if layout == "row":
            return (nb, rows, dim), pl.BlockSpec((1, tm, dim), lambda b, i: (b, i, 0))
        if layout == "col":
            return (nb, dim, rows), pl.BlockSpec((1, dim, tm), lambda b, i: (b, 0, i))
        w = _PROJ_SLAB[name]
        return (nb, rows // w, dim, w), pl.BlockSpec((1, tm // w, dim, w), lambda b, i: (b, i, 0, 0))

    wanted = [o for o in _PROJ_OUTS if o[0] not in skip]
    outs_meta = [out(n, lay, dim) for n, lay, dim, _ in wanted]
    in_specs = [row_spec(D_MODEL), s_spec, s_spec, _const_spec((1, D_MODEL)),
                pl.BlockSpec((tm, LANES), lambda b, i: (i, 0)), pl.BlockSpec((tm, LANES), lambda b, i: (i, 0)),
                _const_spec((N_MAIN, D_MODEL)), _const_spec((N_IDX, D_MODEL)), _const_spec((N_IDX, D_MODEL))]
    outs = pl.pallas_call(
        functools.partial(_proj_kernel, tuple(o[0] for o in wanted)),
        grid=grid,
        in_specs=in_specs,
        out_specs=[spec for _, spec in outs_meta],
        out_shape=[jax.ShapeDtypeStruct(shape, o[3]) for (shape, _), o in zip(outs_meta, wanted)],
        compiler_params=pltpu.CompilerParams(dimension_semantics=("parallel", "parallel"),
                                             vmem_limit_bytes=52 * 1024 * 1024),
        name="proj",
    )(x3, scale3, shift3, g_norm.reshape(1, -1), cos_t, sin_t, w_main, wi1, wi2)
    return {o[0]: a for o, a in zip(wanted, outs)}


def _kth_largest_key(count_ge, shape, k=float(TOPK), nbits=32):
    lowest = -(1 << (nbits - 1))
    prefix = jnp.full(shape, lowest, jnp.int32)
    for b in range(nbits):
        bit = np.int32(lowest if b == 0 else 1 << (nbits - 1 - b))
        cand = prefix ^ bit
        prefix = jnp.where(count_ge(cand) >= k, cand, prefix)
    return prefix


def _kth_largest_key_radix4(count_ge, shape):
    prefix = jnp.full(shape, INT_MIN, jnp.int32)
    for shift in range(30, -1, -2):
        cands = [prefix + np.uint32(j << shift).astype(np.int32) for j in (1, 2, 3)]
        hits = [count_ge(c) >= float(TOPK) for c in cands]
        for c, hit in zip(cands, hits):
            prefix = jnp.where(hit, c, prefix)
    return prefix


def _attn_prompt_kernel(qcat_ref, wit_ref, qa_ref, kcat_ref, k_ref, vt_ref, tri_ref, o_ref,
                        keys_ref, hi_ref, lo_ref, qz_ref, m_ref, acc_ref):
    qi = pl.program_id(1)
    nch = qi + 1
    wt = wit_ref[0, 0]

    def rows(ref, c, n):
        return ref[0, pl.ds(pl.multiple_of(c * n, n), n), :]

    def score_chunk(c, diagonal, pair, half):
        kc = rows(kcat_ref, c, TQ)
        acc = jnp.zeros((TQ, TQ), F32)
        for hh in range(N_HEADS_IDX):
            s = _dot_nt(kc, qcat_ref[0, :, 256 * hh:256 * (hh + 1)])
            acc = acc + jnp.maximum(s, 0.0) * wt[hh:hh + 1, :]
        if diagonal:
            key = lax.broadcasted_iota(jnp.int32, (TQ, TQ), 0)
            qry = lax.broadcasted_iota(jnp.int32, (TQ, TQ), 1)
            acc = jnp.where(key <= qry, acc, -jnp.inf)
        kk = _sort_key(acc)
        keys_ref[2 * c] = kk[:KSLAB]
        keys_ref[2 * c + 1] = kk[KSLAB:]
        hi_ref[pair, TQ * half:TQ * (half + 1), :] = (kk >> 16).astype(jnp.int16)
        lo_ref[pair, TQ * half:TQ * (half + 1), :] = (((kk ^ np.int32(0x8000)) << 16) >> 16).astype(jnp.int16)

    def score_pair(pair, carry):
        score_chunk(2 * pair, False, pair, 0)
        score_chunk(2 * pair + 1, False, pair, 1)
        return carry

    lax.fori_loop(0, qi // 2, score_pair, 0)
    last = qi // 2
    npairs = last + 1

    @pl.when(qi % 2 == 0)
    def _():
        score_chunk(qi, True, last, 0)
        never = jnp.full((TQ, TQ), I16_MIN, jnp.int16)
        hi_ref[last, TQ:, :] = never
        lo_ref[last, TQ:, :] = never

    @pl.when(qi % 2 == 1)
    def _():
        score_chunk(qi - 1, False, last, 0)
        score_chunk(qi, True, last, 1)

    def count16(ref, pred):
        def body(pair, acc):
            m = jnp.where(pred(ref[pair]), jnp.int16(1), jnp.int16(0))
            parts = [m[PACK16 * i:PACK16 * (i + 1)] for i in range(2 * TQ // PACK16)]
            while len(parts) > 1:
                parts = [parts[i] + parts[i + 1] for i in range(0, len(parts), 2)]
            return acc + parts[0].astype(jnp.int32)
        acc = lax.fori_loop(0, npairs, body, jnp.zeros((PACK16, TQ), jnp.int32))
        return jnp.sum(acc.astype(F32), axis=0, keepdims=True)

    as16 = lambda v: v.astype(jnp.int16)
    hi_thr = _kth_largest_key(lambda cand: count16(hi_ref, lambda x: x >= as16(cand)), (1, TQ), nbits=16)
    above = count16(hi_ref, lambda x: x > as16(hi_thr))

    def keep_tied_lows(pair, carry):
        lo_ref[pair] = jnp.where(hi_ref[pair] == as16(hi_thr), lo_ref[pair], I16_MIN)
        return carry

    lax.fori_loop(0, npairs, keep_tied_lows, 0)
    lo_thr = _kth_largest_key(lambda cand: count16(lo_ref, lambda x: x >= as16(cand)), (1, TQ),
                              k=float(TOPK) - above, nbits=16)
    thr = (hi_thr << 16) | ((lo_thr + 2 ** 15) & np.int32(0xFFFF))
    need = float(TOPK) - above - count16(lo_ref, lambda x: x > as16(lo_thr))

    lane = lax.broadcasted_iota(jnp.int32, (TQ, LANES), 1)
    for h in range(N_HEADS_A):
        pair = qa_ref[0, :, LANES * (h // 2):LANES * (h // 2 + 1)]
        mine = (lane < HEAD_DIM_A) if h % 2 == 0 else (lane >= HEAD_DIM_A)
        qz_ref[h] = jnp.where(mine, pair, jnp.zeros_like(pair))
    m_ref[...] = jnp.full(m_ref.shape, M_INIT, F32)
    acc_ref[...] = jnp.zeros(acc_ref.shape, F32)
    dim16 = lax.broadcasted_iota(jnp.int32, (LANES, KSLAB), 0)

    def attn_slab(sl, ties_before):
        kk = keys_ref[sl]
        eq = kk == thr
        eqf = jnp.where(eq, 1.0, 0.0)
        rank = _dot(tri_ref[...], eqf.astype(BF16)) + ties_before
        sel = ((kk > thr) | (eq & (rank < need))) & (kk != KEY_NEG_INF)
        kc = rows(k_ref, sl, KSLAB)
        vts = vt_ref[0, sl]
        for h in range(N_HEADS_A):
            lo = LANES * (h // 2)
            s = jnp.where(sel, _dot_nt(kc[:, lo:lo + LANES], qz_ref[h]), NEG_BIG)
            m_old = m_ref[h]
            m_new = jnp.maximum(m_old, jnp.max(s, axis=0, keepdims=True))
            p = jnp.exp2(s - m_new)
            mine = (dim16 < HEAD_DIM_A) if h % 2 == 0 else (dim16 >= HEAD_DIM_A)
            v1 = jnp.where(mine, vts[lo:lo + LANES, :], jnp.ones((), BF16))
            acc_ref[h] = jnp.exp2(m_old - m_new) * acc_ref[h] + _dot(v1, p.astype(BF16))
            m_ref[h] = m_new
        return ties_before + jnp.sum(eqf, axis=0, keepdims=True)

    def attn_chunk(c, ties_before):
        return attn_slab(2 * c + 1, attn_slab(2 * c, ties_before))

    def attn_two_chunks(i, ties_before):
        return attn_chunk(2 * i + 1, attn_chunk(2 * i, ties_before))

    ties = lax.fori_loop(0, nch // 2, attn_two_chunks, jnp.zeros((1, TQ), F32))

    @pl.when(nch % 2 == 1)
    def _():
        attn_chunk(nch - 1, ties)

    dim = lax.broadcasted_iota(jnp.int32, (LANES, TQ), 0)
    for j in range(N_HEADS_A // 2):
        a0 = acc_ref[2 * j]
        a1 = acc_ref[2 * j + 1]
        even = a0 / a0[HEAD_DIM_A:HEAD_DIM_A + 1]
        odd = a1 / a1[0:1]
        o_ref[0, :, LANES * j:LANES * (j + 1)] = jnp.where(dim < HEAD_DIM_A, even, odd).T.astype(BF16)


def _attn_prompt(p, tri):
    nb, t, _ = p["qa"].shape
    nq = t // TQ
    tile = lambda w: pl.BlockSpec((1, TQ, w), lambda b, i: (b, i, 0))
    full = lambda w: pl.BlockSpec((1, t, w), lambda b, i: (b, 0, 0), pipeline_mode=pl.Buffered(1))
    return pl.pallas_call(
        _attn_prompt_kernel,
        grid=(nb, nq),
        in_specs=[tile(1024), pl.BlockSpec((1, 1, SUBLANES, TQ), lambda b, i: (b, i, 0, 0)),
                  tile(WIDTH_A), full(256), full(WIDTH_A),
                  pl.BlockSpec((1, t // KSLAB, WIDTH_A, KSLAB), lambda b, i: (b, 0, 0, 0),
                               pipeline_mode=pl.Buffered(1)),
                  _const_spec((KSLAB, KSLAB))],
        out_specs=tile(WIDTH_A),
        out_shape=jax.ShapeDtypeStruct((nb, t, WIDTH_A), BF16),
        scratch_shapes=[pltpu.VMEM((t // KSLAB, KSLAB, TQ), jnp.int32),
                        pltpu.VMEM((nq // 2, 2 * TQ, TQ), jnp.int16),
                        pltpu.VMEM((nq // 2, 2 * TQ, TQ), jnp.int16),
                        pltpu.VMEM((N_HEADS_A, TQ, LANES), BF16),
                        pltpu.VMEM((N_HEADS_A, 1, TQ), F32),
                        pltpu.VMEM((N_HEADS_A, LANES, TQ), F32)],
        compiler_params=pltpu.CompilerParams(dimension_semantics=("parallel", "arbitrary"),
                                             vmem_limit_bytes=52 * 1024 * 1024),
        name="attn_prompt",
    )(p["qcat"], p["wit"], p["qa"], p["kcat"], p["k16"], p["vt"], tri)


N_CH_S = N_PAGES + 1
ROWS_A = N_HEADS_A * DEC_PAD
ROWS_I = N_HEADS_IDX * DEC_PAD
NBUF = N_PAGES
N_SEL = N_CH_S + 1
SCORE_PAGES = 8
ATT_PAGES = 16


def _attn_sample_kernel(pt_ref, qih_ref, qil_ref, wi_ref, qa_ref, kin_ref, kn_ref, vn_ref,
                        tri_ref, before_ref, ckidx_ref, ck_ref, cv_ref, o_ref,
                        kidx_buf, knew_buf, kvnew_buf, k_ring, v_ring, keys_ref, selc_ref, sems):
    b = pl.program_id(0)
    nb = pl.num_programs(0)
    cur = b % 2

    def kidx_copy(seq, pg, half):
        return pltpu.make_async_copy(ckidx_ref.at[pt_ref[seq * N_PAGES + pg]], kidx_buf.at[half, pg], sems.at[half])

    def kv_copies(g):
        slot = g % NBUF
        phys = pt_ref[g]
        return (pltpu.make_async_copy(ck_ref.at[phys], k_ring.at[slot], sems.at[2 + slot]),
                pltpu.make_async_copy(cv_ref.at[phys], v_ring.at[slot], sems.at[2 + NBUF + slot]))

    def for_pages(fn):
        def body(pg, carry):
            fn(pg)
            return carry
        lax.fori_loop(0, N_PAGES, body, 0)

    @pl.when(b == 0)
    def _():
        for_pages(lambda pg: kidx_copy(0, pg, 0).start())
        def first_pages(g, carry):
            for cp in kv_copies(g):
                cp.start()
            return carry

        lax.fori_loop(0, NBUF, first_pages, 0)
        knew_buf[...] = jnp.zeros(knew_buf.shape, F32)
        kvnew_buf[...] = jnp.zeros(kvnew_buf.shape, F32)
        keys_ref[N_CH_S] = jnp.full((DEC_PAD, PAGE_SIZE), KEY_NEG_INF, jnp.int32)

    knew_buf[0:DEC_PAD, :] = kin_ref[0]
    kvnew_buf[0, 0:DEC_PAD, :] = kn_ref[0]
    kvnew_buf[1, 0:DEC_PAD, :] = vn_ref[0]
    for_pages(lambda pg: kidx_copy(b, pg, cur).wait())

    @pl.when(b + 1 < nb)
    def _():
        for_pages(lambda pg: kidx_copy(b + 1, pg, 1 - cur).start())

    w = wi_ref[0]
    q_hi = qih_ref[0]
    q_lo = qil_ref[0]
    trow = lax.broadcasted_iota(jnp.int32, (DEC_PAD, PAGE_SIZE), 0)
    tcol = lax.broadcasted_iota(jnp.int32, (DEC_PAD, PAGE_SIZE), 1)

    def score_pages(kt, first, new_page):
        n = kt.shape[1] // PAGE_SIZE
        k_hi, k_lo = _split2(kt)
        s = (_dot(q_lo, k_lo) + _dot(q_lo, k_hi) + _dot(q_hi, k_lo)) + _dot(q_hi, k_hi)
        acc = jnp.zeros((DEC_PAD, n * PAGE_SIZE), F32)
        for hh in range(N_HEADS_IDX):
            acc = acc + jnp.maximum(s[DEC_PAD * hh:DEC_PAD * (hh + 1)], 0.0) * w[:, hh:hh + 1]
        if new_page:
            acc = jnp.where((tcol <= trow) & (tcol < DEC_SEQ), acc, -jnp.inf)
        kk = _sort_key(acc)
        for j in range(n):
            keys_ref[first + j] = kk[:, PAGE_SIZE * j:PAGE_SIZE * (j + 1)]

    def score_body(i, carry):
        first = i * SCORE_PAGES
        score_pages(jnp.concatenate([kidx_buf[cur, first + j] for j in range(SCORE_PAGES)], axis=1), first, False)
        return carry

    lax.fori_loop(0, N_PAGES // SCORE_PAGES, score_body, 0)
    score_pages(knew_buf[...].T, N_PAGES, True)

    def count(pred):
        m = jnp.where(pred(keys_ref[...]), 1.0, 0.0)
        return jnp.sum(jnp.sum(m, axis=0), axis=1, keepdims=True)

    thr = _kth_largest_key_radix4(lambda cand: count(lambda kk: kk >= cand[None]), (DEC_PAD, 1))
    need = float(TOPK) - count(lambda kk: kk > thr[None])

    kk = keys_ref[...]
    eq = kk == thr[None]
    eq2 = jnp.where(eq, 1.0, 0.0).reshape(N_SEL * DEC_PAD, PAGE_SIZE).astype(BF16)
    in_page = _dot(eq2, tri_ref[...])
    per_page = _dot(eq2, jnp.ones((PAGE_SIZE, PAGE_SIZE), BF16))
    rank = (in_page + _dot(before_ref[...], per_page.astype(BF16))).reshape(N_SEL, DEC_PAD, PAGE_SIZE)
    sel = ((kk > thr[None]) | (eq & (rank < need[None]))) & (kk != KEY_NEG_INF)
    selc_ref[...] = jnp.where(sel, 1.0, 0.0)

    lane5 = lax.broadcasted_iota(jnp.int32, (DEC_PAD, WIDTH_A), 1)
    qa = qa_ref[0].astype(F32)
    qbd = jnp.concatenate(
        [jnp.where(lane5 // HEAD_DIM_A == h, qa, 0.0) for h in range(N_HEADS_A)], axis=0).astype(BF16)

    def attend(carry, kt, vt, selc):
        m_old, l_old, acc = carry
        valid = jnp.concatenate([selc] * N_HEADS_A, axis=0) > 0.5
        s = jnp.where(valid, _dot(qbd, kt.astype(BF16)), NEG_BIG)
        m_new = jnp.maximum(m_old, jnp.max(s, axis=1, keepdims=True))
        p = jnp.exp2(s - m_new)
        alpha = jnp.exp2(m_old - m_new)
        return (m_new, alpha * l_old + jnp.sum(p, axis=1, keepdims=True),
                alpha * acc + _dot_nt(p.astype(BF16), vt.astype(BF16)))

    def group_body(gi, carry):
        pg0 = gi * ATT_PAGES
        g0 = b * N_PAGES + pg0
        for j in range(ATT_PAGES):
            for cp in kv_copies(g0 + j):
                cp.wait()
        pages = lambda ring: jnp.concatenate(
            [ring[(pg0 + j) % NBUF].reshape(WIDTH_A, PAGE_SIZE) for j in range(ATT_PAGES)], axis=1)
        selc = jnp.concatenate([selc_ref[pg0 + j] for j in range(ATT_PAGES)], axis=1)
        carry = attend(carry, pages(k_ring), pages(v_ring), selc)
        for j in range(ATT_PAGES):
            @pl.when(g0 + j + NBUF < nb * N_PAGES)
            def _():
                for cp in kv_copies(g0 + j + NBUF):
                    cp.start()
        return carry

    carry = lax.fori_loop(0, N_PAGES // ATT_PAGES, group_body,
                          (jnp.full((ROWS_A, 1), M_INIT, F32), jnp.zeros((ROWS_A, 1), F32),
                           jnp.zeros((ROWS_A, WIDTH_A), F32)))
    _, l, acc = attend(carry, kvnew_buf[0].T, kvnew_buf[1].T, selc_ref[N_PAGES])
    o = acc / l
    out = jnp.zeros((DEC_PAD, WIDTH_A), F32)
    for h in range(N_HEADS_A):
        out = jnp.where(lane5 // HEAD_DIM_A == h, o[DEC_PAD * h:DEC_PAD * (h + 1)], out)
    o_ref[0] = out


def _attn_sample(p, page_table, cache_kidx_t, cache_k_t, cache_v_t, tri):
    nb = p["qa"].shape[0]
    heads_first = lambda a, nh: a.reshape(nb, DEC_PAD, nh, -1).transpose(0, 2, 1, 3).reshape(nb, nh * DEC_PAD, -1)
    qparts = p["qcat"].reshape(nb, DEC_PAD, N_HEADS_IDX, 4, HEAD_DIM_IDX)
    qih = heads_first(qparts[:, :, :, 0], N_HEADS_IDX)
    qil = heads_first(qparts[:, :, :, 2], N_HEADS_IDX)
    rs = np.arange(N_SEL * DEC_PAD)
    before = jnp.asarray(((rs[:, None] % DEC_PAD == rs[None, :] % DEC_PAD)
                          & (rs[None, :] // DEC_PAD < rs[:, None] // DEC_PAD)).astype(np.float32), dtype=BF16)

    blk = lambda rows, w: pl.BlockSpec((1, rows, w), lambda b, pt: (b, 0, 0))
    const = lambda shape: pl.BlockSpec(shape, lambda b, pt: (0,) * len(shape))
    any_spec = pl.BlockSpec(memory_space=pl.ANY)
    grid_spec = pltpu.PrefetchScalarGridSpec(
        num_scalar_prefetch=1,
        grid=(nb,),
        in_specs=[blk(ROWS_I, HEAD_DIM_IDX), blk(ROWS_I, HEAD_DIM_IDX), blk(DEC_PAD, LANES), blk(DEC_PAD, WIDTH_A),
                  blk(DEC_PAD, HEAD_DIM_IDX), blk(DEC_PAD, WIDTH_A), blk(DEC_PAD, WIDTH_A),
                  const((PAGE_SIZE, PAGE_SIZE)), const((N_SEL * DEC_PAD, N_SEL * DEC_PAD)),
                  any_spec, any_spec, any_spec],
        out_specs=blk(DEC_PAD, WIDTH_A),
        scratch_shapes=[pltpu.VMEM((2, N_PAGES, HEAD_DIM_IDX, PAGE_SIZE), F32),
                        pltpu.VMEM((PAGE_SIZE, HEAD_DIM_IDX), F32),
                        pltpu.VMEM((2, PAGE_SIZE, WIDTH_A), F32),
                        pltpu.VMEM((NBUF, N_HEADS_A, HEAD_DIM_A, PAGE_SIZE), F32),
                        pltpu.VMEM((NBUF, N_HEADS_A, HEAD_DIM_A, PAGE_SIZE), F32),
                        pltpu.VMEM((N_SEL, DEC_PAD, PAGE_SIZE), jnp.int32),
                        pltpu.VMEM((N_SEL, DEC_PAD, PAGE_SIZE), F32),
                        pltpu.SemaphoreType.DMA((2 + 2 * NBUF,))],
    )
    return pl.pallas_call(
        _attn_sample_kernel,
        grid_spec=grid_spec,
        out_shape=jax.ShapeDtypeStruct((nb, DEC_PAD, WIDTH_A), F32),
        compiler_params=pltpu.CompilerParams(dimension_semantics=("arbitrary",),
                                             vmem_limit_bytes=52 * 1024 * 1024),
        name="attn_sample",
    )(page_table.reshape(-1), qih, qil, p["wi"], p["qa"], p["ki32"], p["k32"], p["v32"], tri, before,
      cache_kidx_t, cache_k_t, cache_v_t)


def _retention_kernel(q_ref, k_ref, v_ref, s0_ref, dmat_ref, qdec_ref, kdec_ref, gc_ref, bd_ref, hm_ref,
                      ret_ref, sout_ref, state_ref):
    c = pl.program_id(1)

    @pl.when(c == 0)
    def _():
        state_ref[...] = s0_ref[...]

    rows = q_ref.shape[1]
    lane = lax.broadcasted_iota(jnp.int32, (rows, LANES), 1)
    for i in range(q_ref.shape[0]):
        q = q_ref[i]
        k = k_ref[i]
        v = v_ref[i]
        vk = (v.astype(F32) * kdec_ref[...]).astype(BF16)
        for g in range(2):
            gs = slice(256 * g, 256 * (g + 1))
            qg = q[:, gs]
            kg = k[:, gs]
            sg = state_ref[i, g]
            cross = _dot(qg, sg.astype(BF16)) * qdec_ref[:, gs]
            for j in range(2):
                pair = 2 * g + j
                vp = v[:, LANES * pair:LANES * (pair + 1)]
                halves = []
                for e in range(2):
                    hl = 2 * j + e
                    qz = qg * hm_ref[hl]
                    sc = _dot_nt(qz, kg) * dmat_ref[4 * g + hl]
                    halves.append(_dot(sc.astype(BF16), vp))
                inner = jnp.where(lane < DK_R, halves[0], halves[1])
                ret_ref[i, :, LANES * pair:LANES * (pair + 1)] = inner + cross[:, LANES * j:LANES * (j + 1)]
            kv = _dot_tn(kg, vk[:, gs])
            state_ref[i, g] = sg * gc_ref[:, gs] + kv * bd_ref[...]

    @pl.when(c == pl.num_programs(1) - 1)
    def _():
        sout_ref[...] = state_ref[...]


def _retention(q, k, v, state0, tables, chunk, bt):
    nb, rows, _ = q.shape
    nchunk = rows // chunk
    dmat, qdec, kdec, gc, bd, hm = tables
    tile = pl.BlockSpec((bt, chunk, WIDTH_R), lambda b, c: (b, c, 0))
    st = pl.BlockSpec((bt, 2, 256, 256), lambda b, c: (b, 0, 0, 0))
    return pl.pallas_call(
        _retention_kernel,
        grid=(nb // bt, nchunk),
        in_specs=[tile, tile, tile, st,
                  _const_spec((N_HEADS_R, chunk, chunk)), _const_spec((chunk, WIDTH_R)), _const_spec((chunk, WIDTH_R)),
                  _const_spec((1, WIDTH_R)), _const_spec((256, 256)), _const_spec((4, 1, 256))],
        out_specs=[tile, st],
        out_shape=[jax.ShapeDtypeStruct((nb, rows, WIDTH_R), F32),
                   jax.ShapeDtypeStruct((nb, 2, 256, 256), F32)],
        scratch_shapes=[pltpu.VMEM((bt, 2, 256, 256), F32)],
        compiler_params=pltpu.CompilerParams(dimension_semantics=("parallel", "arbitrary")),
        name="retention",
    )(q, k, v, state0, dmat, qdec, kdec, gc, bd, hm)


def _retention_tables(chunk, n_real):
    lg = jnp.log1p(-jnp.exp2(-5.0 - jnp.arange(N_HEADS_R, dtype=F32)))
    i = jnp.arange(chunk, dtype=F32)
    real = jnp.arange(chunk) < n_real
    diff = i[:, None] - i[None, :]
    dmat = jnp.where(diff >= 0, jnp.exp(lg[:, None, None] * jnp.maximum(diff, 0.0)), 0.0)
    dmat = jnp.where(real[None, None, :], dmat, 0.0)
    q_decay = jnp.exp(lg[None, :] * (i[:, None] + 1.0))
    k_decay = jnp.where(real[:, None], jnp.exp(lg[None, :] * (n_real - 1.0 - i)[:, None]), 0.0)
    per_lane = lambda a: jnp.repeat(a, WIDTH_R // N_HEADS_R, axis=-1)
    gc = per_lane(jnp.exp(lg * n_real)[None, :])
    r = np.arange(256)
    bd = ((r[:, None] % 128) // 32 == (r[None, :] // 64)).astype(np.float32)
    hm = np.stack([((r % 128) // 32 == hl) for hl in range(4)]).astype(np.float32).reshape(4, 1, 256)
    return (dmat, per_lane(q_decay), per_lane(k_decay), gc, jnp.asarray(bd), jnp.asarray(hm, dtype=BF16))


def _merge_kernel(x_ref, gate_ref, ya_ref, sza_ref, ret_ref, szr_ref, sga_ref, sgr_ref,
                  avg_ref, gret_ref, wpa_ref, wpr_ref, wout_ref, gfin_ref, y_ref):
    ret = ret_ref[0]
    avg = avg_ref[...]
    r1, r2 = _split2(ret)
    dev = ret - (_dot(r2, avg) + _dot(r1, avg))
    e1, e2 = _split2(dev * dev)
    var = _dot(e2, avg) + _dot(e1, avg)
    yr = (dev * lax.rsqrt(var + EPS)) * gret_ref[...] * szr_ref[0].astype(F32)
    ya = (ya_ref[0].astype(F32) * sza_ref[0].astype(F32)).astype(BF16)
    merged = (sga_ref[0].astype(F32) * _dot(ya, wpa_ref[...])
              + sgr_ref[0].astype(F32) * _dot(yr.astype(BF16), wpr_ref[...]))
    xo = x_ref[0] + gate_ref[0] * _dot(merged.astype(BF16), wout_ref[...])
    r = lax.rsqrt(jnp.mean(xo * xo, axis=-1, keepdims=True) + EPS)
    y_ref[0] = (xo * r) * gfin_ref[...]


def _merge(x3, gate3, ya, ret, p, avg, g_ret, w_pa, w_pr, w_out, g_final, tm):
    nb, rows, _ = x3.shape
    srows = gate3.shape[1]
    row = lambda w: pl.BlockSpec((1, tm, w), lambda b, i: (b, i, 0))
    g_spec = pl.BlockSpec((1, 1 if srows == 1 else tm, D_MODEL),
                          (lambda b, i: (b, 0, 0)) if srows == 1 else (lambda b, i: (b, i, 0)))
    return pl.pallas_call(
        _merge_kernel,
        grid=(nb, rows // tm),
        in_specs=[row(D_MODEL), g_spec, row(WIDTH_A), row(WIDTH_A), row(WIDTH_R), row(WIDTH_R), row(D_MODEL),
                  row(D_MODEL),
                  _const_spec((WIDTH_R, WIDTH_R)), _const_spec((1, WIDTH_R)),
                  _const_spec((WIDTH_A, D_MODEL)), _const_spec((WIDTH_R, D_MODEL)), _const_spec((D_MODEL, D_MODEL)),
                  _const_spec((1, D_MODEL))],
        out_specs=row(D_MODEL),
        out_shape=jax.ShapeDtypeStruct((nb, rows, D_MODEL), F32),
        compiler_params=pltpu.CompilerParams(dimension_semantics=("parallel", "parallel"),
                                             vmem_limit_bytes=48 * 1024 * 1024),
        name="merge",
    )(x3, gate3, ya, p["sza"], ret, p["szr"], p["sga"], p["sgr"], avg, g_ret.reshape(1, -1),
      w_pa, w_pr, w_out, g_final.reshape(1, -1))


def _group_rows(w):
    d = w.shape[1]
    return w.reshape(2, 4, 2, 32, d).transpose(0, 2, 1, 3, 4).reshape(512, d)


def _prep_weights(w_in):
    wt = w_in.T
    s = lambda o, n: wt[o:o + n]
    main = jnp.concatenate([
        s(_O_QA, 512) * (HEAD_DIM_A ** -0.5 * LOG2E),
        s(_O_KA, 512), s(_O_VA, 512), s(_O_ZA, 512),
        _group_rows(s(_O_QR, 512)), _group_rows(s(_O_KR, 512)) * DK_R ** -0.5, s(_O_VR, 512), s(_O_ZR, 512),
        s(_O_GA, 1024), s(_O_GR, 1024)], axis=0).astype(BF16)
    qi = jnp.broadcast_to(s(_O_QI, 256).reshape(N_HEADS_IDX, 1, HEAD_DIM_IDX, D_MODEL),
                          (N_HEADS_IDX, 2, HEAD_DIM_IDX, D_MODEL)).reshape(512, D_MODEL)
    ki = s(_O_KI, 64)
    wi = s(_O_WI, N_HEADS_IDX) * (N_HEADS_IDX ** -0.5 * HEAD_DIM_IDX ** -0.5)
    idx = jnp.concatenate([qi, ki, ki, wi, jnp.zeros((LANES - N_HEADS_IDX, D_MODEL), F32)], axis=0)
    i1 = idx.astype(BF16)
    i2 = (idx - i1.astype(F32)).astype(BF16)
    return main, i1, i2


def _rope_tables(pos):
    half = DK_R // 2
    inv = ROPE_BASE ** (-jnp.arange(half, dtype=F32) / half)
    ang = pos.astype(F32)[:, None] * inv[None, :]
    return jnp.tile(jnp.cos(ang), (1, 4)), jnp.tile(jnp.sin(ang), (1, 4))


def _state_to_groups(state):
    nb = state.shape[0]
    st = state.reshape(nb, 2, 4, 2, 32, 64).transpose(0, 1, 3, 2, 4, 5).reshape(nb, 2, 256, 64)
    head_of_row = (np.arange(256) % 128) // 32
    cols = [jnp.where((head_of_row == hl)[None, None, :, None], st, 0.0) for hl in range(4)]
    return jnp.concatenate(cols, axis=-1)


def _groups_to_state(sg):
    nb = sg.shape[0]
    head_of_row = (np.arange(256) % 128) // 32
    own = sum(jnp.where((head_of_row == hl)[None, None, :, None], sg[..., 64 * hl:64 * (hl + 1)], 0.0)
              for hl in range(4))
    return own.reshape(nb, 2, 2, 4, 32, 64).transpose(0, 1, 3, 2, 4, 5).reshape(nb, N_HEADS_R, DK_R, 64)


def kernel(x_prompt, x_sample, cache_k, cache_v, cache_kidx, state_ret, page_table, c_prompt, c_sample,
           w_ada, b_ada, g_norm, w_in, g_ret, w_pa, w_pr, w_out, g_final):
    nbp, t, _ = x_prompt.shape
    nbs = x_sample.shape[0]
    depth = w_in.shape[0]
    assert depth == 1

    r = np.arange(TQ)
    tri256 = jnp.asarray((r[:, None] < r[None, :]).astype(np.float32), dtype=BF16)
    tri128 = tri256[:PAGE_SIZE, :PAGE_SIZE]
    c5 = np.arange(WIDTH_R)
    avg = jnp.asarray((c5[:, None] // 64 == c5[None, :] // 64).astype(np.float32) / 64.0, dtype=BF16)
    cos_p, sin_p = _rope_tables(jnp.arange(t))
    cos_s, sin_s = _rope_tables(jnp.tile(PAST_LEN + jnp.arange(DEC_PAD), nbs))
    tab_p = _retention_tables(CHUNK_R, CHUNK_R)
    tab_s = _retention_tables(DEC_PAD, DEC_SEQ)

    hp = x_prompt
    hs = jnp.pad(x_sample, ((0, 0), (0, DEC_PAD - DEC_SEQ), (0, 0))).reshape(1, nbs * DEC_PAD, D_MODEL)
    outs_p, outs_s = [], []
    for l in range(depth):
        w_main, wi1, wi2 = _prep_weights(w_in[l])
        wpa, wpr, wout = w_pa[l].astype(BF16), w_pr[l].astype(BF16), w_out[l].astype(BF16)

        c_all = jnp.concatenate([c_prompt, c_sample, jnp.zeros((6, D_MODEL), F32)], axis=0)
        mod = _adaln(c_all, w_ada[l], b_ada[l])
        shift, scale, gate = mod[:, :D_MODEL], mod[:, D_MODEL:2 * D_MODEL], mod[:, 2 * D_MODEL:]
        per_p = lambda a: a[:nbp].reshape(nbp, 1, D_MODEL)
        per_s = lambda a: jnp.broadcast_to(a[nbp:nbp + nbs, None, :], (nbs, DEC_PAD, D_MODEL)).reshape(
            1, nbs * DEC_PAD, D_MODEL)

        pp = _project(hp, per_p(scale), per_p(shift), g_norm[l], cos_p, sin_p, w_main, wi1, wi2, tm=512,
                      skip=("k32", "v32", "ki32", "wi"))
        ya_p = _attn_prompt(pp, tri128.T)
        ret_p, sg_p = _retention(pp["qr"], pp["kr"], pp["vr"], jnp.zeros((nbp, 2, 256, 256), F32), tab_p, CHUNK_R,
                                 bt=nbp)
        hp = _merge(hp, per_p(gate), ya_p, ret_p, pp, avg, g_ret[l], wpa, wpr, wout, g_final, tm=512)

        ps = _project(hs, per_s(scale), per_s(shift), g_norm[l], cos_s, sin_s, w_main, wi1, wi2, tm=nbs * DEC_PAD,
                      skip=("kt32", "vt32", "kit32", "vt", "wit", "k16", "kcat"))
        ps3 = {o[0]: ps[o[0]].reshape(nbs, DEC_PAD, o[2]) for o in _PROJ_OUTS if o[0] in ps and o[1] == "row"}
        ya_s = _attn_sample(ps3, page_table, cache_kidx[l].transpose(0, 2, 1),
                            cache_k[l].transpose(0, 2, 3, 1), cache_v[l].transpose(0, 2, 3, 1), tri128)
        ret_s, sg_s = _retention(ps3["qr"], ps3["kr"], ps3["vr"], _state_to_groups(state_ret[l].astype(F32)),
                                 tab_s, DEC_PAD, bt=4)
        hs = _merge(hs, per_s(gate), ya_s.reshape(1, nbs * DEC_PAD, WIDTH_A),
                    ret_s.reshape(1, nbs * DEC_PAD, WIDTH_R), ps, avg, g_ret[l], wpa, wpr, wout, g_final,
                    tm=nbs * DEC_PAD)

        heads = lambda a, n: a.reshape(n, -1, N_HEADS_A, HEAD_DIM_A)
        tokens_first = lambda a: a.reshape(nbp, N_HEADS_A, HEAD_DIM_A, t).transpose(0, 3, 1, 2)
        outs_p.append((tokens_first(pp["kt32"]), tokens_first(pp["vt32"]), pp["kit32"].transpose(0, 2, 1),
                       _groups_to_state(sg_p)))
        tok = lambda a: a[:, :DEC_SEQ]
        outs_s.append((heads(tok(ps3["k32"]), nbs), heads(tok(ps3["v32"]), nbs), tok(ps3["ki32"]),
                       _groups_to_state(sg_s)))

    y_prompt = hp
    y_sample = hs.reshape(nbs, DEC_PAD, D_MODEL)[:, :DEC_SEQ]
    stack = lambda items, i: jnp.stack([it[i] for it in items])
    return (y_prompt, y_sample,
            stack(outs_p, 0), stack(outs_p, 1), stack(outs_p, 2), stack(outs_p, 3),
            stack(outs_s, 0), stack(outs_s, 1), stack(outs_s, 2), stack(outs_s, 3))
```

```python
import functools

import numpy as np
import jax
import jax.numpy as jnp
from jax import lax
from jax.experimental import pallas as pl
from jax.experimental.pallas import tpu as pltpu

D_MODEL = 1024
SEQ = 8192
DEC_SEQ = 4
PAST_LEN = 8192
PAGE_SIZE = 128
N_HEADS_A = 8
HEAD_DIM_A = 64
WIDTH_A = 512
N_HEADS_IDX = 4
HEAD_DIM_IDX = 64
TOPK = 256
N_HEADS_R = 8
DK_R = 64
WIDTH_R = 512
CHUNK_R = 128
ROPE_BASE = 10000.0
EPS = 1e-6

LANES = 128
SUBLANES = 8
DEC_PAD = SUBLANES
N_PAGES = PAST_LEN // PAGE_SIZE

_O_QA, _O_KA, _O_VA, _O_ZA = 0, 512, 1024, 1536
_O_QI, _O_KI, _O_WI = 2048, 2304, 2368
_O_QR, _O_KR, _O_VR, _O_ZR = 2372, 2884, 3396, 3908
_O_GA, _O_GR, _N_IN = 4420, 5444, 6468

N_MAIN = 6144
N_IDX = 768

INT_MIN = np.int32(-2 ** 31)
KEY_NEG_INF = np.int32(np.array(0xFF800000, np.uint32).view(np.int32) ^ np.int32(0x7FFFFFFF))
NEG_BIG = -1e30
M_INIT = -1e29
LOG2E = 1.4426950408889634
KSLAB = 128
TQ = 256
PACK16 = 16
I16_MIN = np.int16(-2 ** 15)

F32 = jnp.float32
BF16 = jnp.bfloat16


def _dot(a, b):
    return jnp.dot(a, b, preferred_element_type=F32)


def _dot_nt(a, b):
    return lax.dot_general(a, b, (((1,), (1,)), ((), ())), preferred_element_type=F32)


def _dot_tn(a, b):
    return lax.dot_general(a, b, (((0,), (0,)), ((), ())), preferred_element_type=F32)


def _split2(x):
    hi = x.astype(BF16)
    lo = (x - hi.astype(F32)).astype(BF16)
    return hi, lo


def _split3(x):
    hi = x.astype(BF16)
    r = x - hi.astype(F32)
    mid = r.astype(BF16)
    lo = (r - mid.astype(F32)).astype(BF16)
    return hi, mid, lo


def _sort_key(score):
    bits = pltpu.bitcast(score, jnp.int32)
    return bits ^ ((bits >> 31) & np.int32(0x7FFFFFFF))


def _const_spec(shape):
    nd = len(shape)
    return pl.BlockSpec(shape, lambda *_: (0,) * nd, pipeline_mode=pl.Buffered(1))


def _adaln_kernel(c_ref, w_ref, b_ref, o_ref):
    c = c_ref[...]
    a1, a2, a3 = _split3(c * jax.nn.sigmoid(c))
    w1, w2, w3 = _split3(w_ref[...])
    small = _dot(a1, w3) + _dot(a2, w2) + _dot(a3, w1)
    mid = _dot(a1, w2) + _dot(a2, w1)
    o_ref[...] = (small + mid) + _dot(a1, w1) + b_ref[...]


def _adaln(c_all, w_ada, b_ada):
    rows = c_all.shape[0]
    tn = 512
    return pl.pallas_call(
        _adaln_kernel,
        grid=(3 * D_MODEL // tn,),
        in_specs=[pl.BlockSpec((rows, D_MODEL), lambda j: (0, 0)),
                  pl.BlockSpec((D_MODEL, tn), lambda j: (0, j)),
                  pl.BlockSpec((1, tn), lambda j: (0, j))],
        out_specs=pl.BlockSpec((rows, tn), lambda j: (0, j)),
        out_shape=jax.ShapeDtypeStruct((rows, 3 * D_MODEL), F32),
        name="adaln",
    )(c_all, w_ada, b_ada.reshape(1, -1))


_PROJ_OUTS = (
    ("qa", "row", 512, BF16), ("k32", "row", 512, F32), ("v32", "row", 512, F32), ("kt32", "col", 512, F32),
    ("vt32", "col", 512, F32), ("k16", "row", 512, BF16), ("vt", "slab", 512, BF16),
    ("sza", "row", 512, BF16), ("qcat", "row", 1024, BF16), ("kcat", "row", 256, BF16),
    ("ki32", "row", 64, F32), ("kit32", "col", 64, F32), ("wi", "row", 128, F32), ("wit", "slab", SUBLANES, F32),
    ("qr", "row", 512, BF16), ("kr", "row", 512, BF16), ("vr", "row", 512, BF16), ("szr", "row", 512, BF16),
    ("sga", "row", 1024, BF16), ("sgr", "row", 1024, BF16),
)
_PROJ_SLAB = {"vt": KSLAB, "wit": TQ}


def _proj_kernel(names, x_ref, scale_ref, shift_ref, gn_ref, cos_ref, sin_ref, wm_ref, wi1_ref, wi2_ref, *out_refs):
    o = dict(zip(names, out_refs))

    def put(name, value):
        if name in o:
            o[name][0] = value()

    x = x_ref[0]
    r = lax.rsqrt(jnp.mean(x * x, axis=-1, keepdims=True) + EPS)
    h = (x * r) * gn_ref[...] * (1.0 + scale_ref[0]) + shift_ref[0]
    h1, h2 = _split2(h)

    def main(g):
        return _dot_nt(h1, wm_ref[g * 512:(g + 1) * 512, :])

    o["qa"][0] = main(0).astype(BF16)
    u = main(1)
    put("k32", lambda: u)
    put("kt32", lambda: u.T)
    put("k16", lambda: u.astype(BF16))
    u = main(2)
    put("v32", lambda: u)
    if "vt32" in o:
        ut = u.T
        o["vt32"][0] = ut
        for sl in range(u.shape[0] // KSLAB):
            o["vt"][0, sl] = ut[:, KSLAB * sl:KSLAB * (sl + 1)].astype(BF16)
    u = main(3)
    o["sza"][0] = (u * jax.nn.sigmoid(u)).astype(BF16)

    cos = cos_ref[...]
    sin = sin_ref[...]
    for g, o_ref in ((4, o["qr"]), (5, o["kr"])):
        u = main(g)
        for grp in range(2):
            x1 = u[:, 256 * grp:256 * grp + 128]
            x2 = u[:, 256 * grp + 128:256 * grp + 256]
            o_ref[0, :, 256 * grp:256 * grp + 128] = (x1 * cos - x2 * sin).astype(BF16)
            o_ref[0, :, 256 * grp + 128:256 * grp + 256] = (x1 * sin + x2 * cos).astype(BF16)
    o["vr"][0] = main(6).astype(BF16)
    u = main(7)
    o["szr"][0] = (u * jax.nn.sigmoid(u)).astype(BF16)
    for j in range(2):
        o["sga"][0, :, 512 * j:512 * (j + 1)] = jax.nn.sigmoid(main(8 + j)).astype(BF16)
        o["sgr"][0, :, 512 * j:512 * (j + 1)] = jax.nn.sigmoid(main(10 + j)).astype(BF16)

    w1 = wi1_ref[...]
    w2 = wi2_ref[...]
    ui = (_dot_nt(h2, w2) + _dot_nt(h2, w1) + _dot_nt(h1, w2)) + _dot_nt(h1, w1)
    qd = ui[:, :512]
    q_hi, q_lo = _split2(qd)
    for hh in range(N_HEADS_IDX):
        o["qcat"][0, :, 256 * hh:256 * hh + 128] = q_hi[:, 128 * hh:128 * hh + 128]
        o["qcat"][0, :, 256 * hh + 128:256 * hh + 256] = q_lo[:, 128 * hh:128 * hh + 128]
    kd = ui[:, 512:640]
    if "kcat" in o:
        k_hi, k_lo = _split2(kd)
        lane = lax.broadcasted_iota(jnp.int32, kd.shape, 1)
        sel = jnp.where(lane < HEAD_DIM_IDX, k_hi, k_lo)
        o["kcat"][0, :, 0:128] = sel
        o["kcat"][0, :, 128:256] = sel
    put("ki32", lambda: kd[:, :HEAD_DIM_IDX])
    put("kit32", lambda: kd.T[0:HEAD_DIM_IDX, :])
    put("wi", lambda: ui[:, 640:768])
    if "wit" in o:
        wt = ui[:, 640:768].T[0:SUBLANES, :]
        for sl in range(wt.shape[1] // TQ):
            o["wit"][0, sl] = wt[:, TQ * sl:TQ * (sl + 1)]


def _project(x3, scale3, shift3, g_norm, cos_t, sin_t, w_main, wi1, wi2, tm, skip):
    nb, rows, _ = x3.shape
    srows = scale3.shape[1]
    stile = 1 if srows == 1 else tm
    grid = (nb, rows // tm)
    row_spec = lambda w: pl.BlockSpec((1, tm, w), lambda b, i: (b, i, 0))
    s_spec = pl.BlockSpec((1, stile, D_MODEL), (lambda b, i: (b, 0, 0)) if srows == 1 else (lambda b, i: (b, i, 0)))

    def out(name, layout, dim):
        if layout == "row":
            return (nb, rows, dim), pl.BlockSpec((1, tm, dim), lambda b, i: (b, i, 0))
        if layout == "col":
            return (nb, dim, rows), pl.BlockSpec((1, dim, tm), lambda b, i: (b, 0, i))
        w = _PROJ_SLAB[name]
        return (nb, rows // w, dim, w), pl.BlockSpec((1, tm // w, dim, w), lambda b, i: (b, i, 0, 0))

    wanted = [o for o in _PROJ_OUTS if o[0] not in skip]
    outs_meta = [out(n, lay, dim) for n, lay, dim, _ in wanted]
    in_specs = [row_spec(D_MODEL), s_spec, s_spec, _const_spec((1, D_MODEL)),
                pl.BlockSpec((tm, LANES), lambda b, i: (i, 0)), pl.BlockSpec((tm, LANES), lambda b, i: (i, 0)),
                _const_spec((N_MAIN, D_MODEL)), _const_spec((N_IDX, D_MODEL)), _const_spec((N_IDX, D_MODEL))]
    outs = pl.pallas_call(
        functools.partial(_proj_kernel, tuple(o[0] for o in wanted)),
        grid=grid,
        in_specs=in_specs,
        out_specs=[spec for _, spec in outs_meta],
        out_shape=[jax.ShapeDtypeStruct(shape, o[3]) for (shape, _), o in zip(outs_meta, wanted)],
        compiler_params=pltpu.CompilerParams(dimension_semantics=("parallel", "parallel"),
                                             vmem_limit_bytes=52 * 1024 * 1024),
        name="proj",
    )(x3, scale3, shift3, g_norm.reshape(1, -1), cos_t, sin_t, w_main, wi1, wi2)
    return {o[0]: a for o, a in zip(wanted, outs)}


def _kth_largest_key(count_ge, shape, k=float(TOPK), nbits=32):
    lowest = -(1 << (nbits - 1))
    prefix = jnp.full(shape, lowest, jnp.int32)
    for b in range(nbits):
        bit = np.int32(lowest if b == 0 else 1 << (nbits - 1 - b))
        cand = prefix ^ bit
        prefix = jnp.where(count_ge(cand) >= k, cand, prefix)
    return prefix


def _kth_largest_key_radix4(count_ge, shape):
    prefix = jnp.full(shape, INT_MIN, jnp.int32)
    for shift in range(30, -1, -2):
        cands = [prefix + np.uint32(j << shift).astype(np.int32) for j in (1, 2, 3)]
        hits = [count_ge(c) >= float(TOPK) for c in cands]
        for c, hit in zip(cands, hits):
            prefix = jnp.where(hit, c, prefix)
    return prefix


def _attn_prompt_kernel(qcat_ref, wit_ref, qa_ref, kcat_ref, k_ref, vt_ref, tri_ref, o_ref,
                        keys_ref, hi_ref, lo_ref, qz_ref, m_ref, acc_ref):
    qi = pl.program_id(1)
    nch = qi + 1
    wt = wit_ref[0, 0]

    def rows(ref, c, n):
        return ref[0, pl.ds(pl.multiple_of(c * n, n), n), :]

    def score_chunk(c, diagonal, pair, half):
        kc = rows(kcat_ref, c, TQ)
        acc = jnp.zeros((TQ, TQ), F32)
        for hh in range(N_HEADS_IDX):
            s = _dot_nt(kc, qcat_ref[0, :, 256 * hh:256 * (hh + 1)])
            acc = acc + jnp.maximum(s, 0.0) * wt[hh:hh + 1, :]
        if diagonal:
            key = lax.broadcasted_iota(jnp.int32, (TQ, TQ), 0)
            qry = lax.broadcasted_iota(jnp.int32, (TQ, TQ), 1)
            acc = jnp.where(key <= qry, acc, -jnp.inf)
        kk = _sort_key(acc)
        keys_ref[2 * c] = kk[:KSLAB]
        keys_ref[2 * c + 1] = kk[KSLAB:]
        hi_ref[pair, TQ * half:TQ * (half + 1), :] = (kk >> 16).astype(jnp.int16)
        lo_ref[pair, TQ * half:TQ * (half + 1), :] = (((kk ^ np.int32(0x8000)) << 16) >> 16).astype(jnp.int16)

    def score_pair(pair, carry):
        score_chunk(2 * pair, False, pair, 0)
        score_chunk(2 * pair + 1, False, pair, 1)
        return carry

    lax.fori_loop(0, qi // 2, score_pair, 0)
    last = qi // 2
    npairs = last + 1

    @pl.when(qi % 2 == 0)
    def _():
        score_chunk(qi, True, last, 0)
        never = jnp.full((TQ, TQ), I16_MIN, jnp.int16)
        hi_ref[last, TQ:, :] = never
        lo_ref[last, TQ:, :] = never

    @pl.when(qi % 2 == 1)
    def _():
        score_chunk(qi - 1, False, last, 0)
        score_chunk(qi, True, last, 1)

    def count16(ref, pred):
        def body(pair, acc):
            m = jnp.where(pred(ref[pair]), jnp.int16(1), jnp.int16(0))
            parts = [m[PACK16 * i:PACK16 * (i + 1)] for i in range(2 * TQ // PACK16)]
            while len(parts) > 1:
                parts = [parts[i] + parts[i + 1] for i in range(0, len(parts), 2)]
            return acc + parts[0].astype(jnp.int32)
        acc = lax.fori_loop(0, npairs, body, jnp.zeros((PACK16, TQ), jnp.int32))
        return jnp.sum(acc.astype(F32), axis=0, keepdims=True)

    as16 = lambda v: v.astype(jnp.int16)
    hi_thr = _kth_largest_key(lambda cand: count16(hi_ref, lambda x: x >= as16(cand)), (1, TQ), nbits=16)
    above = count16(hi_ref, lambda x: x > as16(hi_thr))

    def keep_tied_lows(pair, carry):
        lo_ref[pair] = jnp.where(hi_ref[pair] == as16(hi_thr), lo_ref[pair], I16_MIN)
        return carry

    lax.fori_loop(0, npairs, keep_tied_lows, 0)
    lo_thr = _kth_largest_key(lambda cand: count16(lo_ref, lambda x: x >= as16(cand)), (1, TQ),
                              k=float(TOPK) - above, nbits=16)
    thr = (hi_thr << 16) | ((lo_thr + 2 ** 15) & np.int32(0xFFFF))
    need = float(TOPK) - above - count16(lo_ref, lambda x: x > as16(lo_thr))

    lane = lax.broadcasted_iota(jnp.int32, (TQ, LANES), 1)
    for h in range(N_HEADS_A):
        pair = qa_ref[0, :, LANES * (h // 2):LANES * (h // 2 + 1)]
        mine = (lane < HEAD_DIM_A) if h % 2 == 0 else (lane >= HEAD_DIM_A)
        qz_ref[h] = jnp.where(mine, pair, jnp.zeros_like(pair))
    m_ref[...] = jnp.full(m_ref.shape, M_INIT, F32)
    acc_ref[...] = jnp.zeros(acc_ref.shape, F32)
    dim16 = lax.broadcasted_iota(jnp.int32, (LANES, KSLAB), 0)

    def attn_slab(sl, ties_before):
        kk = keys_ref[sl]
        eq = kk == thr
        eqf = jnp.where(eq, 1.0, 0.0)
        rank = _dot(tri_ref[...], eqf.astype(BF16)) + ties_before
        sel = ((kk > thr) | (eq & (rank < need))) & (kk != KEY_NEG_INF)
        kc = rows(k_ref, sl, KSLAB)
        vts = vt_ref[0, sl]
        for h in range(N_HEADS_A):
            lo = LANES * (h // 2)
            s = jnp.where(sel, _dot_nt(kc[:, lo:lo + LANES], qz_ref[h]), NEG_BIG)
            m_old = m_ref[h]
            m_new = jnp.maximum(m_old, jnp.max(s, axis=0, keepdims=True))
            p = jnp.exp2(s - m_new)
            mine = (dim16 < HEAD_DIM_A) if h % 2 == 0 else (dim16 >= HEAD_DIM_A)
            v1 = jnp.where(mine, vts[lo:lo + LANES, :], jnp.ones((), BF16))
            acc_ref[h] = jnp.exp2(m_old - m_new) * acc_ref[h] + _dot(v1, p.astype(BF16))
            m_ref[h] = m_new
        return ties_before + jnp.sum(eqf, axis=0, keepdims=True)

    def attn_chunk(c, ties_before):
        return attn_slab(2 * c + 1, attn_slab(2 * c, ties_before))

    def attn_two_chunks(i, ties_before):
        return attn_chunk(2 * i + 1, attn_chunk(2 * i, ties_before))

    ties = lax.fori_loop(0, nch // 2, attn_two_chunks, jnp.zeros((1, TQ), F32))

    @pl.when(nch % 2 == 1)
    def _():
        attn_chunk(nch - 1, ties)

    dim = lax.broadcasted_iota(jnp.int32, (LANES, TQ), 0)
    for j in range(N_HEADS_A // 2):
        a0 = acc_ref[2 * j]
        a1 = acc_ref[2 * j + 1]
        even = a0 / a0[HEAD_DIM_A:HEAD_DIM_A + 1]
        odd = a1 / a1[0:1]
        o_ref[0, :, LANES * j:LANES * (j + 1)] = jnp.where(dim < HEAD_DIM_A, even, odd).T.astype(BF16)


def _attn_prompt(p, tri):
    nb, t, _ = p["qa"].shape
    nq = t // TQ
    tile = lambda w: pl.BlockSpec((1, TQ, w), lambda b, i: (b, i, 0))
    full = lambda w: pl.BlockSpec((1, t, w), lambda b, i: (b, 0, 0), pipeline_mode=pl.Buffered(1))
    return pl.pallas_call(
        _attn_prompt_kernel,
        grid=(nb, nq),
        in_specs=[tile(1024), pl.BlockSpec((1, 1, SUBLANES, TQ), lambda b, i: (b, i, 0, 0)),
                  tile(WIDTH_A), full(256), full(WIDTH_A),
                  pl.BlockSpec((1, t // KSLAB, WIDTH_A, KSLAB), lambda b, i: (b, 0, 0, 0),
                               pipeline_mode=pl.Buffered(1)),
                  _const_spec((KSLAB, KSLAB))],
        out_specs=tile(WIDTH_A),
        out_shape=jax.ShapeDtypeStruct((nb, t, WIDTH_A), BF16),
        scratch_shapes=[pltpu.VMEM((t // KSLAB, KSLAB, TQ), jnp.int32),
                        pltpu.VMEM((nq // 2, 2 * TQ, TQ), jnp.int16),
                        pltpu.VMEM((nq // 2, 2 * TQ, TQ), jnp.int16),
                        pltpu.VMEM((N_HEADS_A, TQ, LANES), BF16),
                        pltpu.VMEM((N_HEADS_A, 1, TQ), F32),
                        pltpu.VMEM((N_HEADS_A, LANES, TQ), F32)],
        compiler_params=pltpu.CompilerParams(dimension_semantics=("parallel", "arbitrary"),
                                             vmem_limit_bytes=52 * 1024 * 1024),
        name="attn_prompt",
    )(p["qcat"], p["wit"], p["qa"], p["kcat"], p["k16"], p["vt"], tri)


N_CH_S = N_PAGES + 1
ROWS_A = N_HEADS_A * DEC_PAD
ROWS_I = N_HEADS_IDX * DEC_PAD
NBUF = N_PAGES
N_SEL = N_CH_S + 1
SCORE_PAGES = 16
ATT_PAGES = 16


def _attn_sample_kernel(pt_ref, qih_ref, qil_ref, wi_ref, qa_ref, kin_ref, kn_ref, vn_ref,
                        tri_ref, before_ref, ckidx_ref, ck_ref, cv_ref, o_ref,
                        kidx_buf, knew_buf, kvnew_buf, k_ring, v_ring, keys_ref, selc_ref, sems):
    b = pl.program_id(0)
    nb = pl.num_programs(0)
    cur = b % 2

    def kidx_copy(seq, pg, half):
        return pltpu.make_async_copy(ckidx_ref.at[pt_ref[seq * N_PAGES + pg]], kidx_buf.at[half, pg], sems.at[half])

    def kv_copies(g):
        slot = g % NBUF
        phys = pt_ref[g]
        return (pltpu.make_async_copy(ck_ref.at[phys], k_ring.at[slot], sems.at[2 + slot]),
                pltpu.make_async_copy(cv_ref.at[phys], v_ring.at[slot], sems.at[2 + NBUF + slot]))

    def for_pages(fn):
        def body(pg, carry):
            fn(pg)
            return carry
        lax.fori_loop(0, N_PAGES, body, 0)

    @pl.when(b == 0)
    def _():
        for_pages(lambda pg: kidx_copy(0, pg, 0).start())
        def first_pages(g, carry):
            for cp in kv_copies(g):
                cp.start()
            return carry

        lax.fori_loop(0, NBUF, first_pages, 0)
        knew_buf[...] = jnp.zeros(knew_buf.shape, F32)
        kvnew_buf[...] = jnp.zeros(kvnew_buf.shape, F32)
        keys_ref[N_CH_S] = jnp.full((DEC_PAD, PAGE_SIZE), KEY_NEG_INF, jnp.int32)

    knew_buf[0:DEC_PAD, :] = kin_ref[0]
    kvnew_buf[0, 0:DEC_PAD, :] = kn_ref[0]
    kvnew_buf[1, 0:DEC_PAD, :] = vn_ref[0]
    for_pages(lambda pg: kidx_copy(b, pg, cur).wait())

    @pl.when(b + 1 < nb)
    def _():
        for_pages(lambda pg: kidx_copy(b + 1, pg, 1 - cur).start())

    w = wi_ref[0]
    q_hi = qih_ref[0]
    q_lo = qil_ref[0]
    trow = lax.broadcasted_iota(jnp.int32, (DEC_PAD, PAGE_SIZE), 0)
    tcol = lax.broadcasted_iota(jnp.int32, (DEC_PAD, PAGE_SIZE), 1)

    def score_pages(kt, first, new_page):
        n = kt.shape[1] // PAGE_SIZE
        k_hi, k_lo = _split2(kt)
        s = (_dot(q_lo, k_lo) + _dot(q_lo, k_hi) + _dot(q_hi, k_lo)) + _dot(q_hi, k_hi)
        acc = jnp.zeros((DEC_PAD, n * PAGE_SIZE), F32)
        for hh in range(N_HEADS_IDX):
            acc = acc + jnp.maximum(s[DEC_PAD * hh:DEC_PAD * (hh + 1)], 0.0) * w[:, hh:hh + 1]
        if new_page:
            acc = jnp.where((tcol <= trow) & (tcol < DEC_SEQ), acc, -jnp.inf)
        kk = _sort_key(acc)
        for j in range(n):
            keys_ref[first + j] = kk[:, PAGE_SIZE * j:PAGE_SIZE * (j + 1)]

    def score_body(i, carry):
        first = i * SCORE_PAGES
        score_pages(jnp.concatenate([kidx_buf[cur, first + j] for j in range(SCORE_PAGES)], axis=1), first, False)
        return carry

    lax.fori_loop(0, N_PAGES // SCORE_PAGES, score_body, 0)
    score_pages(knew_buf[...].T, N_PAGES, True)

    def count(pred):
        m = jnp.where(pred(keys_ref[...]), 1.0, 0.0)
        return jnp.sum(jnp.sum(m, axis=0), axis=1, keepdims=True)

    thr = _kth_largest_key_radix4(lambda cand: count(lambda kk: kk >= cand[None]), (DEC_PAD, 1))
    need = float(TOPK) - count(lambda kk: kk > thr[None])

    kk = keys_ref[...]
    eq = kk == thr[None]
    eq2 = jnp.where(eq, 1.0, 0.0).reshape(N_SEL * DEC_PAD, PAGE_SIZE).astype(BF16)
    in_page = _dot(eq2, tri_ref[...])
    per_page = _dot(eq2, jnp.ones((PAGE_SIZE, PAGE_SIZE), BF16))
    rank = (in_page + _dot(before_ref[...], per_page.astype(BF16))).reshape(N_SEL, DEC_PAD, PAGE_SIZE)
    sel = ((kk > thr[None]) | (eq & (rank < need[None]))) & (kk != KEY_NEG_INF)
    selc_ref[...] = jnp.where(sel, 1.0, 0.0)

    lane5 = lax.broadcasted_iota(jnp.int32, (DEC_PAD, WIDTH_A), 1)
    qa = qa_ref[0].astype(F32)
    qbd = jnp.concatenate(
        [jnp.where(lane5 // HEAD_DIM_A == h, qa, 0.0) for h in range(N_HEADS_A)], axis=0).astype(BF16)

    def attend(carry, kt, vt, selc):
        m_old, l_old, acc = carry
        valid = jnp.concatenate([selc] * N_HEADS_A, axis=0) > 0.5
        s = jnp.where(valid, _dot(qbd, kt.astype(BF16)), NEG_BIG)
        m_new = jnp.maximum(m_old, jnp.max(s, axis=1, keepdims=True))
        p = jnp.exp2(s - m_new)
        alpha = jnp.exp2(m_old - m_new)
        return (m_new, alpha * l_old + jnp.sum(p, axis=1, keepdims=True),
                alpha * acc + _dot_nt(p.astype(BF16), vt.astype(BF16)))

    def group_body(gi, carry):
        pg0 = gi * ATT_PAGES
        g0 = b * N_PAGES + pg0
        for j in range(ATT_PAGES):
            for cp in kv_copies(g0 + j):
                cp.wait()
        pages = lambda ring: jnp.concatenate(
            [ring[(pg0 + j) % NBUF].reshape(WIDTH_A, PAGE_SIZE) for j in range(ATT_PAGES)], axis=1)
        selc = jnp.concatenate([selc_ref[pg0 + j] for j in range(ATT_PAGES)], axis=1)
        carry = attend(carry, pages(k_ring), pages(v_ring), selc)
        for j in range(ATT_PAGES):
            @pl.when(g0 + j + NBUF < nb * N_PAGES)
            def _():
                for cp in kv_copies(g0 + j + NBUF):
                    cp.start()
        return carry

    carry = lax.fori_loop(0, N_PAGES // ATT_PAGES, group_body,
                          (jnp.full((ROWS_A, 1), M_INIT, F32), jnp.zeros((ROWS_A, 1), F32),
                           jnp.zeros((ROWS_A, WIDTH_A), F32)))
    _, l, acc = attend(carry, kvnew_buf[0].T, kvnew_buf[1].T, selc_ref[N_PAGES])
    o = acc / l
    out = jnp.zeros((DEC_PAD, WIDTH_A), F32)
    for h in range(N_HEADS_A):
        out = jnp.where(lane5 // HEAD_DIM_A == h, o[DEC_PAD * h:DEC_PAD * (h + 1)], out)
    o_ref[0] = out


def _attn_sample(p, page_table, cache_kidx_t, cache_k_t, cache_v_t, tri):
    nb = p["qa"].shape[0]
    heads_first = lambda a, nh: a.reshape(nb, DEC_PAD, nh, -1).transpose(0, 2, 1, 3).reshape(nb, nh * DEC_PAD, -1)
    qparts = p["qcat"].reshape(nb, DEC_PAD, N_HEADS_IDX, 4, HEAD_DIM_IDX)
    qih = heads_first(qparts[:, :, :, 0], N_HEADS_IDX)
    qil = heads_first(qparts[:, :, :, 2], N_HEADS_IDX)
    rs = np.arange(N_SEL * DEC_PAD)
    before = jnp.asarray(((rs[:, None] % DEC_PAD == rs[None, :] % DEC_PAD)
                          & (rs[None, :] // DEC_PAD < rs[:, None] // DEC_PAD)).astype(np.float32), dtype=BF16)

    blk = lambda rows, w: pl.BlockSpec((1, rows, w), lambda b, pt: (b, 0, 0))
    const = lambda shape: pl.BlockSpec(shape, lambda b, pt: (0,) * len(shape))
    any_spec = pl.BlockSpec(memory_space=pl.ANY)
    grid_spec = pltpu.PrefetchScalarGridSpec(
        num_scalar_prefetch=1,
        grid=(nb,),
        in_specs=[blk(ROWS_I, HEAD_DIM_IDX), blk(ROWS_I, HEAD_DIM_IDX), blk(DEC_PAD, LANES), blk(DEC_PAD, WIDTH_A),
                  blk(DEC_PAD, HEAD_DIM_IDX), blk(DEC_PAD, WIDTH_A), blk(DEC_PAD, WIDTH_A),
                  const((PAGE_SIZE, PAGE_SIZE)), const((N_SEL * DEC_PAD, N_SEL * DEC_PAD)),
                  any_spec, any_spec, any_spec],
        out_specs=blk(DEC_PAD, WIDTH_A),
        scratch_shapes=[pltpu.VMEM((2, N_PAGES, HEAD_DIM_IDX, PAGE_SIZE), F32),
                        pltpu.VMEM((PAGE_SIZE, HEAD_DIM_IDX), F32),
                        pltpu.VMEM((2, PAGE_SIZE, WIDTH_A), F32),
                        pltpu.VMEM((NBUF, N_HEADS_A, HEAD_DIM_A, PAGE_SIZE), F32),
                        pltpu.VMEM((NBUF, N_HEADS_A, HEAD_DIM_A, PAGE_SIZE), F32),
                        pltpu.VMEM((N_SEL, DEC_PAD, PAGE_SIZE), jnp.int32),
                        pltpu.VMEM((N_SEL, DEC_PAD, PAGE_SIZE), F32),
                        pltpu.SemaphoreType.DMA((2 + 2 * NBUF,))],
    )
    return pl.pallas_call(
        _attn_sample_kernel,
        grid_spec=grid_spec,
        out_shape=jax.ShapeDtypeStruct((nb, DEC_PAD, WIDTH_A), F32),
        compiler_params=pltpu.CompilerParams(dimension_semantics=("arbitrary",),
                                             vmem_limit_bytes=52 * 1024 * 1024),
        name="attn_sample",
    )(page_table.reshape(-1), qih, qil, p["wi"], p["qa"], p["ki32"], p["k32"], p["v32"], tri, before,
      cache_kidx_t, cache_k_t, cache_v_t)


def _retention_kernel(q_ref, k_ref, v_ref, s0_ref, dmat_ref, qdec_ref, kdec_ref, gc_ref, bd_ref, hm_ref,
                      ret_ref, sout_ref, state_ref):
    c = pl.program_id(1)

    @pl.when(c == 0)
    def _():
        state_ref[...] = s0_ref[...]

    rows = q_ref.shape[1]
    lane = lax.broadcasted_iota(jnp.int32, (rows, LANES), 1)
    for i in range(q_ref.shape[0]):
        q = q_ref[i]
        k = k_ref[i]
        v = v_ref[i]
        vk = (v.astype(F32) * kdec_ref[...]).astype(BF16)
        for g in range(2):
            gs = slice(256 * g, 256 * (g + 1))
            qg = q[:, gs]
            kg = k[:, gs]
            sg = state_ref[i, g]
            cross = _dot(qg, sg.astype(BF16)) * qdec_ref[:, gs]
            for j in range(2):
                pair = 2 * g + j
                vp = v[:, LANES * pair:LANES * (pair + 1)]
                halves = []
                for e in range(2):
                    hl = 2 * j + e
                    qz = qg * hm_ref[hl]
                    sc = _dot_nt(qz, kg) * dmat_ref[4 * g + hl]
                    halves.append(_dot(sc.astype(BF16), vp))
                inner = jnp.where(lane < DK_R, halves[0], halves[1])
                ret_ref[i, :, LANES * pair:LANES * (pair + 1)] = inner + cross[:, LANES * j:LANES * (j + 1)]
            kv = _dot_tn(kg, vk[:, gs])
            state_ref[i, g] = sg * gc_ref[:, gs] + kv * bd_ref[...]

    @pl.when(c == pl.num_programs(1) - 1)
    def _():
        sout_ref[...] = state_ref[...]


def _retention(q, k, v, state0, tables, chunk, bt):
    nb, rows, _ = q.shape
    nchunk = rows // chunk
    dmat, qdec, kdec, gc, bd, hm = tables
    tile = pl.BlockSpec((bt, chunk, WIDTH_R), lambda b, c: (b, c, 0))
    st = pl.BlockSpec((bt, 2, 256, 256), lambda b, c: (b, 0, 0, 0))
    return pl.pallas_call(
        _retention_kernel,
        grid=(nb // bt, nchunk),
        in_specs=[tile, tile, tile, st,
                  _const_spec((N_HEADS_R, chunk, chunk)), _const_spec((chunk, WIDTH_R)), _const_spec((chunk, WIDTH_R)),
                  _const_spec((1, WIDTH_R)), _const_spec((256, 256)), _const_spec((4, 1, 256))],
        out_specs=[tile, st],
        out_shape=[jax.ShapeDtypeStruct((nb, rows, WIDTH_R), F32),
                   jax.ShapeDtypeStruct((nb, 2, 256, 256), F32)],
        scratch_shapes=[pltpu.VMEM((bt, 2, 256, 256), F32)],
        compiler_params=pltpu.CompilerParams(dimension_semantics=("parallel", "arbitrary")),
        name="retention",
    )(q, k, v, state0, dmat, qdec, kdec, gc, bd, hm)


def _retention_tables(chunk, n_real):
    lg = jnp.log1p(-jnp.exp2(-5.0 - jnp.arange(N_HEADS_R, dtype=F32)))
    i = jnp.arange(chunk, dtype=F32)
    real = jnp.arange(chunk) < n_real
    diff = i[:, None] - i[None, :]
    dmat = jnp.where(diff >= 0, jnp.exp(lg[:, None, None] * jnp.maximum(diff, 0.0)), 0.0)
    dmat = jnp.where(real[None, None, :], dmat, 0.0)
    q_decay = jnp.exp(lg[None, :] * (i[:, None] + 1.0))
    k_decay = jnp.where(real[:, None], jnp.exp(lg[None, :] * (n_real - 1.0 - i)[:, None]), 0.0)
    per_lane = lambda a: jnp.repeat(a, WIDTH_R // N_HEADS_R, axis=-1)
    gc = per_lane(jnp.exp(lg * n_real)[None, :])
    r = np.arange(256)
    bd = ((r[:, None] % 128) // 32 == (r[None, :] // 64)).astype(np.float32)
    hm = np.stack([((r % 128) // 32 == hl) for hl in range(4)]).astype(np.float32).reshape(4, 1, 256)
    return (dmat, per_lane(q_decay), per_lane(k_decay), gc, jnp.asarray(bd), jnp.asarray(hm, dtype=BF16))


def _merge_kernel(x_ref, gate_ref, ya_ref, sza_ref, ret_ref, szr_ref, sga_ref, sgr_ref,
                  avg_ref, gret_ref, wpa_ref, wpr_ref, wout_ref, gfin_ref, y_ref):
    ret = ret_ref[0]
    avg = avg_ref[...]
    r1, r2 = _split2(ret)
    dev = ret - (_dot(r2, avg) + _dot(r1, avg))
    e1, e2 = _split2(dev * dev)
    var = _dot(e2, avg) + _dot(e1, avg)
    yr = (dev * lax.rsqrt(var + EPS)) * gret_ref[...] * szr_ref[0].astype(F32)
    ya = (ya_ref[0].astype(F32) * sza_ref[0].astype(F32)).astype(BF16)
    merged = (sga_ref[0].astype(F32) * _dot(ya, wpa_ref[...])
              + sgr_ref[0].astype(F32) * _dot(yr.astype(BF16), wpr_ref[...]))
    xo = x_ref[0] + gate_ref[0] * _dot(merged.astype(BF16), wout_ref[...])
    r = lax.rsqrt(jnp.mean(xo * xo, axis=-1, keepdims=True) + EPS)
    y_ref[0] = (xo * r) * gfin_ref[...]


def _merge(x3, gate3, ya, ret, p, avg, g_ret, w_pa, w_pr, w_out, g_final, tm):
    nb, rows, _ = x3.shape
    srows = gate3.shape[1]
    row = lambda w: pl.BlockSpec((1, tm, w), lambda b, i: (b, i, 0))
    g_spec = pl.BlockSpec((1, 1 if srows == 1 else tm, D_MODEL),
                          (lambda b, i: (b, 0, 0)) if srows == 1 else (lambda b, i: (b, i, 0)))
    return pl.pallas_call(
        _merge_kernel,
        grid=(nb, rows // tm),
        in_specs=[row(D_MODEL), g_spec, row(WIDTH_A), row(WIDTH_A), row(WIDTH_R), row(WIDTH_R), row(D_MODEL),
                  row(D_MODEL),
                  _const_spec((WIDTH_R, WIDTH_R)), _const_spec((1, WIDTH_R)),
                  _const_spec((WIDTH_A, D_MODEL)), _const_spec((WIDTH_R, D_MODEL)), _const_spec((D_MODEL, D_MODEL)),
                  _const_spec((1, D_MODEL))],
        out_specs=row(D_MODEL),
        out_shape=jax.ShapeDtypeStruct((nb, rows, D_MODEL), F32),
        compiler_params=pltpu.CompilerParams(dimension_semantics=("parallel", "parallel"),
                                             vmem_limit_bytes=48 * 1024 * 1024),
        name="merge",
    )(x3, gate3, ya, p["sza"], ret, p["szr"], p["sga"], p["sgr"], avg, g_ret.reshape(1, -1),
      w_pa, w_pr, w_out, g_final.reshape(1, -1))


def _group_rows(w):
    d = w.shape[1]
    return w.reshape(2, 4, 2, 32, d).transpose(0, 2, 1, 3, 4).reshape(512, d)


def _prep_weights(w_in):
    wt = w_in.T
    s = lambda o, n: wt[o:o + n]
    main = jnp.concatenate([
        s(_O_QA, 512) * (HEAD_DIM_A ** -0.5 * LOG2E),
        s(_O_KA, 512), s(_O_VA, 512), s(_O_ZA, 512),
        _group_rows(s(_O_QR, 512)), _group_rows(s(_O_KR, 512)) * DK_R ** -0.5, s(_O_VR, 512), s(_O_ZR, 512),
        s(_O_GA, 1024), s(_O_GR, 1024)], axis=0).astype(BF16)
    qi = jnp.broadcast_to(s(_O_QI, 256).reshape(N_HEADS_IDX, 1, HEAD_DIM_IDX, D_MODEL),
                          (N_HEADS_IDX, 2, HEAD_DIM_IDX, D_MODEL)).reshape(512, D_MODEL)
    ki = s(_O_KI, 64)
    wi = s(_O_WI, N_HEADS_IDX) * (N_HEADS_IDX ** -0.5 * HEAD_DIM_IDX ** -0.5)
    idx = jnp.concatenate([qi, ki, ki, wi, jnp.zeros((LANES - N_HEADS_IDX, D_MODEL), F32)], axis=0)
    i1 = idx.astype(BF16)
    i2 = (idx - i1.astype(F32)).astype(BF16)
    return main, i1, i2


def _rope_tables(pos):
    half = DK_R // 2
    inv = ROPE_BASE ** (-jnp.arange(half, dtype=F32) / half)
    ang = pos.astype(F32)[:, None] * inv[None, :]
    return jnp.tile(jnp.cos(ang), (1, 4)), jnp.tile(jnp.sin(ang), (1, 4))


def _state_to_groups(state):
    nb = state.shape[0]
    st = state.reshape(nb, 2, 4, 2, 32, 64).transpose(0, 1, 3, 2, 4, 5).reshape(nb, 2, 256, 64)
    head_of_row = (np.arange(256) % 128) // 32
    cols = [jnp.where((head_of_row == hl)[None, None, :, None], st, 0.0) for hl in range(4)]
    return jnp.concatenate(cols, axis=-1)


def _groups_to_state(sg):
    nb = sg.shape[0]
    head_of_row = (np.arange(256) % 128) // 32
    own = sum(jnp.where((head_of_row == hl)[None, None, :, None], sg[..., 64 * hl:64 * (hl + 1)], 0.0)
              for hl in range(4))
    return own.reshape(nb, 2, 2, 4, 32, 64).transpose(0, 1, 3, 2, 4, 5).reshape(nb, N_HEADS_R, DK_R, 64)


def kernel(x_prompt, x_sample, cache_k, cache_v, cache_kidx, state_ret, page_table, c_prompt, c_sample,
           w_ada, b_ada, g_norm, w_in, g_ret, w_pa, w_pr, w_out, g_final):
    nbp, t, _ = x_prompt.shape
    nbs = x_sample.shape[0]
    depth = w_in.shape[0]
    assert depth == 1

    r = np.arange(TQ)
    tri256 = jnp.asarray((r[:, None] < r[None, :]).astype(np.float32), dtype=BF16)
    tri128 = tri256[:PAGE_SIZE, :PAGE_SIZE]
    c5 = np.arange(WIDTH_R)
    avg = jnp.asarray((c5[:, None] // 64 == c5[None, :] // 64).astype(np.float32) / 64.0, dtype=BF16)
    cos_p, sin_p = _rope_tables(jnp.arange(t))
    cos_s, sin_s = _rope_tables(jnp.tile(PAST_LEN + jnp.arange(DEC_PAD), nbs))
    tab_p = _retention_tables(CHUNK_R, CHUNK_R)
    tab_s = _retention_tables(DEC_PAD, DEC_SEQ)

    hp = x_prompt
    hs = jnp.pad(x_sample, ((0, 0), (0, DEC_PAD - DEC_SEQ), (0, 0))).reshape(1, nbs * DEC_PAD, D_MODEL)
    outs_p, outs_s = [], []
    for l in range(depth):
        w_main, wi1, wi2 = _prep_weights(w_in[l])
        wpa, wpr, wout = w_pa[l].astype(BF16), w_pr[l].astype(BF16), w_out[l].astype(BF16)

        c_all = jnp.concatenate([c_prompt, c_sample, jnp.zeros((6, D_MODEL), F32)], axis=0)
        mod = _adaln(c_all, w_ada[l], b_ada[l])
        shift, scale, gate = mod[:, :D_MODEL], mod[:, D_MODEL:2 * D_MODEL], mod[:, 2 * D_MODEL:]
        per_p = lambda a: a[:nbp].reshape(nbp, 1, D_MODEL)
        per_s = lambda a: jnp.broadcast_to(a[nbp:nbp + nbs, None, :], (nbs, DEC_PAD, D_MODEL)).reshape(
            1, nbs * DEC_PAD, D_MODEL)

        pp = _project(hp, per_p(scale), per_p(shift), g_norm[l], cos_p, sin_p, w_main, wi1, wi2, tm=512,
                      skip=("k32", "v32", "ki32", "wi"))
        ya_p = _attn_prompt(pp, tri128.T)
        ret_p, sg_p = _retention(pp["qr"], pp["kr"], pp["vr"], jnp.zeros((nbp, 2, 256, 256), F32), tab_p, CHUNK_R,
                                 bt=nbp)
        hp = _merge(hp, per_p(gate), ya_p, ret_p, pp, avg, g_ret[l], wpa, wpr, wout, g_final, tm=512)

        ps = _project(hs, per_s(scale), per_s(shift), g_norm[l], cos_s, sin_s, w_main, wi1, wi2, tm=nbs * DEC_PAD,
                      skip=("kt32", "vt32", "kit32", "vt", "wit", "k16", "kcat"))
        ps3 = {o[0]: ps[o[0]].reshape(nbs, DEC_PAD, o[2]) for o in _PROJ_OUTS if o[0] in ps and o[1] == "row"}
        ya_s = _attn_sample(ps3, page_table, cache_kidx[l].transpose(0, 2, 1),
                            cache_k[l].transpose(0, 2, 3, 1), cache_v[l].transpose(0, 2, 3, 1), tri128)
        ret_s, sg_s = _retention(ps3["qr"], ps3["kr"], ps3["vr"], _state_to_groups(state_ret[l].astype(F32)),
                                 tab_s, DEC_PAD, bt=4)
        hs = _merge(hs, per_s(gate), ya_s.reshape(1, nbs * DEC_PAD, WIDTH_A),
                    ret_s.reshape(1, nbs * DEC_PAD, WIDTH_R), ps, avg, g_ret[l], wpa, wpr, wout, g_final,
                    tm=nbs * DEC_PAD)

        heads = lambda a, n: a.reshape(n, -1, N_HEADS_A, HEAD_DIM_A)
        tokens_first = lambda a: a.reshape(nbp, N_HEADS_A, HEAD_DIM_A, t).transpose(0, 3, 1, 2)
        outs_p.append((tokens_first(pp["kt32"]), tokens_first(pp["vt32"]), pp["kit32"].transpose(0, 2, 1),
                       _groups_to_state(sg_p)))
        tok = lambda a: a[:, :DEC_SEQ]
        outs_s.append((heads(tok(ps3["k32"]), nbs), heads(tok(ps3["v32"]), nbs), tok(ps3["ki32"]),
                       _groups_to_state(sg_s)))

    y_prompt = hp
    y_sample = hs.reshape(nbs, DEC_PAD, D_MODEL)[:, :DEC_SEQ]
    stack = lambda items, i: jnp.stack([it[i] for it in items])
    return (y_prompt, y_sample,
            stack(outs_p, 0), stack(outs_p, 1), stack(outs_p, 2), stack(outs_p, 3),
            stack(outs_s, 0), stack(outs_s, 1), stack(outs_s, 2), stack(outs_s, 3))
```
